```python
import math
import numpy as np
import jax, jax.numpy as jnp
from jax import lax


D_MODEL = 2048
BATCH = 8
SEQ = 4096
DEPTH = 4

RWKV_HEAD_DIM = 64
RWKV_WIDTH = D_MODEL // 2
RWKV_HEADS = RWKV_WIDTH // RWKV_HEAD_DIM
DIFF_QK_DIM = 64
DIFF_V_DIM = 2 * DIFF_QK_DIM
DIFF_WIDTH = D_MODEL - RWKV_WIDTH
DIFF_HEADS = DIFF_WIDTH // DIFF_V_DIM
DECAY_LORA = 64
AAA_LORA = 64
MV_LORA = 32
GATE_LORA = 160
RWKV_COLS = 3 * RWKV_WIDTH + DECAY_LORA + AAA_LORA + GATE_LORA
DIFF_COLS = 2 * DIFF_HEADS * 2 * DIFF_QK_DIM + DIFF_WIDTH
IN_COLS = RWKV_COLS + DIFF_COLS
D_FF = 5632
CONV_WIDTH = 3
ROPE_THETA = 500000.0
ROPE_DIM = DIFF_QK_DIM // 4
Q_BLOCK = 128
NORM_EPS = 1e-6
GN_EPS = 64e-5
SUBLN_EPS = 1e-5

kernel_name = 'hybrid_rwkv7_diffattn_sandwich'


def rms_norm(x, g, eps=NORM_EPS):
    xf = x.astype(jnp.float32)
    y = xf * lax.rsqrt(jnp.mean(xf * xf, axis=-1, keepdims=True) + eps)
    return (y * g.astype(jnp.float32)).astype(x.dtype)


def token_shift(p, mu):
    prev = jnp.pad(p, ((0, 0), (1, 0), (0, 0)))[:, :-1]
    return p + mu * (prev - p)


def rwkv7_scan(r, decay, k, v, kk, a):
    B, S, H, N = r.shape

    def step(state, inp):
        r_t, w_t, k_t, v_t, kk_t, a_t = inp
        sa = jnp.einsum('bhij,bhj->bhi', state, -kk_t)
        state = (state * w_t[:, :, None, :]
                 + sa[..., None] * (kk_t * a_t)[:, :, None, :]
                 + v_t[..., None] * k_t[:, :, None, :])
        o_t = jnp.einsum('bhij,bhj->bhi', state, r_t)
        return state, o_t

    xs = tuple(jnp.moveaxis(t, 1, 0) for t in (r, decay, k, v, kk, a))
    state0 = jnp.zeros((B, H, N, N), jnp.float32)
    _, out = lax.scan(step, state0, xs)
    return jnp.moveaxis(out, 0, 1)


def rwkv7_group(p_main, p_mv, mu_main, mu_mv, w0, w2, a0, a2, g2, k_k, k_a, r_k,
                gn_w, gn_b, v0, v2, v_first):
    B, S, _ = p_main.shape
    f32 = jnp.float32
    p_main = token_shift(p_main.astype(f32), mu_main)
    sizes = (RWKV_WIDTH, DECAY_LORA, RWKV_WIDTH, RWKV_WIDTH, AAA_LORA, GATE_LORA)
    r, wl, k, v, al, gl = jnp.split(p_main, np.cumsum(sizes)[:-1].tolist(), axis=-1)
    w_log = -jax.nn.softplus(-(w0 + jnp.tanh(wl) @ w2)) - 0.5
    decay = jnp.exp(-jnp.exp(w_log.astype(f32)))
    a = jax.nn.sigmoid(a0 + al @ a2)
    g = jax.nn.sigmoid(gl) @ g2
    if v_first is None:
        v_first = v
    else:
        vl = token_shift(p_mv.astype(f32), mu_mv)
        v = v + (v_first - v) * jax.nn.sigmoid(v0 + vl @ v2)
    heads = lambda t: t.reshape(B, S, RWKV_HEADS, RWKV_HEAD_DIM).astype(f32)
    kk = heads(k * k_k)
    kk = kk / jnp.maximum(jnp.sqrt(jnp.sum(kk * kk, axis=-1, keepdims=True)), 1e-12)
    k = k * (1.0 + (a - 1.0) * k_a)
    r_h, k_h, v_h = heads(r), heads(k), heads(v)
    o = rwkv7_scan(r_h, heads(decay), k_h, v_h, kk, heads(a))
    mean = jnp.mean(o, axis=-1, keepdims=True)
    var = jnp.mean(jnp.square(o - mean), axis=-1, keepdims=True)
    o = ((o - mean) * lax.rsqrt(var + GN_EPS)).reshape(B, S, RWKV_WIDTH) * gn_w + gn_b
    bonus = jnp.sum(r_h * k_h * r_k, axis=-1, keepdims=True) * v_h
    o = (o + bonus.reshape(B, S, RWKV_WIDTH)) * g
    return o, v_first


def rotary_partial(t, cos, sin):
    half = ROPE_DIM // 2
    c = cos[:, :, None, None, :]
    s = sin[:, :, None, None, :]
    t1 = t[..., :half]
    t2 = t[..., half:ROPE_DIM]
    return jnp.concatenate([t1 * c - t2 * s, t2 * c + t1 * s, t[..., ROPE_DIM:]], axis=-1)


def diff_attention_group(q, k, v, cos, sin, lam_q1, lam_k1, lam_q2, lam_k2, subln_w, lambda_init):
    B, S, _ = q.shape
    f32 = jnp.float32
    q = rotary_partial(q.reshape(B, S, DIFF_HEADS, 2, DIFF_QK_DIM).astype(f32), cos, sin)
    k = rotary_partial(k.reshape(B, S, DIFF_HEADS, 2, DIFF_QK_DIM).astype(f32), cos, sin)
    v = jnp.transpose(v.reshape(B, S, DIFF_HEADS, DIFF_V_DIM).astype(f32), (0, 2, 1, 3))
    lam = (jnp.exp(jnp.sum(lam_q1.astype(f32) * lam_k1.astype(f32)))
           - jnp.exp(jnp.sum(lam_q2.astype(f32) * lam_k2.astype(f32))) + lambda_init)
    q = jnp.transpose(q, (0, 2, 3, 1, 4)) * (DIFF_QK_DIM ** -0.5)
    k = jnp.transpose(k, (0, 2, 3, 1, 4))
    key_pos = jnp.arange(S)

    def block(i):
        start = i * Q_BLOCK
        qb = lax.dynamic_slice_in_dim(q, start, Q_BLOCK, axis=3)
        s = jnp.einsum('bhmqd,bhmkd->bhmqk', qb, k)
        q_pos = start + jnp.arange(Q_BLOCK)
        mask = key_pos[None, :] <= q_pos[:, None]
        p = jax.nn.softmax(jnp.where(mask, s, -jnp.inf), axis=-1)
        attn = p[:, :, 0] - lam * p[:, :, 1]
        return jnp.einsum('bhqk,bhkd->bhqd', attn, v)

    o = lax.map(block, jnp.arange(S // Q_BLOCK))
    o = jnp.transpose(o, (1, 0, 3, 2, 4)).reshape(B, S, DIFF_HEADS, DIFF_V_DIM)
    o = o * lax.rsqrt(jnp.mean(o * o, axis=-1, keepdims=True) + SUBLN_EPS) * subln_w
    o = o * (1.0 - lambda_init)
    return o.reshape(B, S, DIFF_WIDTH)


def conv_glu_ffn(h, w_up, conv_w, conv_b, w_down):
    S = h.shape[1]
    gate, up = jnp.split(h @ w_up, 2, axis=-1)
    gp = jnp.pad(gate, ((0, 0), (CONV_WIDTH - 1, 0), (0, 0)))
    gate = sum(gp[:, j:j + S] * conv_w[j] for j in range(CONV_WIDTH)) + conv_b
    return (jax.nn.gelu(gate, approximate=True) * up) @ w_down


def setup_inputs(seed: int = 0) -> dict:
    key = jax.random.key(seed)
    keys = jax.random.split(key, 40)
    counter = [0]

    def nxt():
        counter[0] += 1
        return keys[counter[0] - 1]

    f32 = jnp.float32
    nrm = lambda shape, scale: jax.random.normal(nxt(), shape, f32) * scale
    unif = lambda shape, lo, hi: jax.random.uniform(nxt(), shape, f32, lo, hi)
    L, Lv = DEPTH, DEPTH - 1
    x = nrm((BATCH, SEQ, D_MODEL), 1.0)
    positions = (jax.random.randint(nxt(), (BATCH, 1), 0, 4096, jnp.int32)
                 + jnp.arange(SEQ, dtype=jnp.int32)[None, :])
    gain = lambda: 1.0 + nrm((L, D_MODEL), 0.05)
    conv_center = jnp.zeros((CONV_WIDTH,), f32).at[-1].set(1.0)
    return {
        'x': x,
        'positions': positions,
        'pre_mix_norm': gain(),
        'post_mix_norm': gain(),
        'pre_ffn_norm': gain(),
        'post_ffn_norm': gain(),
        'w_in': nrm((L, D_MODEL, IN_COLS), D_MODEL ** -0.5),
        'w_mv_down': nrm((Lv, D_MODEL, MV_LORA), D_MODEL ** -0.5),
        'shift_mu': unif((L, RWKV_COLS), 0.0, 1.0),
        'shift_mu_mv': unif((Lv, MV_LORA), 0.0, 1.0),
        'w0': unif((L, RWKV_WIDTH), -6.0, 1.0),
        'w2': nrm((L, DECAY_LORA, RWKV_WIDTH), 0.5 * DECAY_LORA ** -0.5),
        'a0': nrm((L, RWKV_WIDTH), 0.1),
        'a2': nrm((L, AAA_LORA, RWKV_WIDTH), AAA_LORA ** -0.5),
        'g2': nrm((L, GATE_LORA, RWKV_WIDTH), GATE_LORA ** -0.5),
        'k_k': 0.85 + nrm((L, RWKV_WIDTH), 0.05),
        'k_a': 1.0 + nrm((L, RWKV_WIDTH), 0.05),
        'r_k': nrm((L, RWKV_HEADS, RWKV_HEAD_DIM), 0.1),
        'gn_w': 1.0 + nrm((L, RWKV_WIDTH), 0.05),
        'gn_b': nrm((L, RWKV_WIDTH), 0.02),
        'v0': 1.0 + nrm((Lv, RWKV_WIDTH), 0.1),
        'v2': nrm((Lv, MV_LORA, RWKV_WIDTH), MV_LORA ** -0.5),
        'lam_q1': nrm((L, DIFF_QK_DIM), 0.1),
        'lam_k1': nrm((L, DIFF_QK_DIM), 0.1),
        'lam_q2': nrm((L, DIFF_QK_DIM), 0.1),
        'lam_k2': nrm((L, DIFF_QK_DIM), 0.1),
        'subln_w': 1.0 + nrm((L, DIFF_V_DIM), 0.05),
        'w_out': nrm((L, D_MODEL, D_MODEL), D_MODEL ** -0.5),
        'w_up': nrm((L, D_MODEL, 2 * D_FF), D_MODEL ** -0.5),
        'conv_w': nrm((L, CONV_WIDTH, D_FF), 0.2) + conv_center[None, :, None],
        'conv_b': nrm((L, D_FF), 0.02),
        'w_down': nrm((L, D_FF, D_MODEL), D_FF ** -0.5),
    }


def reference(x, positions, pre_mix_norm, post_mix_norm, pre_ffn_norm, post_ffn_norm,
              w_in, w_mv_down, shift_mu, shift_mu_mv, w0, w2, a0, a2, g2, k_k, k_a, r_k,
              gn_w, gn_b, v0, v2, lam_q1, lam_k1, lam_q2, lam_k2, subln_w, w_out,
              w_up, conv_w, conv_b, w_down):
    f32 = jnp.float32
    inv_freq = ROPE_THETA ** (-jnp.arange(0, ROPE_DIM, 2, dtype=f32) / ROPE_DIM)
    ang = positions.astype(f32)[..., None] * inv_freq
    cos, sin = jnp.cos(ang), jnp.sin(ang)
    v_first = None
    for l in range(DEPTH):
        h = rms_norm(x, pre_mix_norm[l])
        if l == 0:
            p = h @ w_in[0]
            p_mv, mu_mv, v0_l, v2_l = None, None, None, None
        else:
            p = h @ jnp.concatenate([w_in[l], w_mv_down[l - 1]], axis=-1)
            p_mv, mu_mv, v0_l, v2_l = p[..., IN_COLS:], shift_mu_mv[l - 1], v0[l - 1], v2[l - 1]
        p_rwkv = p[..., :RWKV_COLS]
        q_d, k_d, v_d = jnp.split(p[..., RWKV_COLS:IN_COLS],
                                  [DIFF_HEADS * 2 * DIFF_QK_DIM, 2 * DIFF_HEADS * 2 * DIFF_QK_DIM], axis=-1)
        o_rwkv, v_first = rwkv7_group(p_rwkv, p_mv, shift_mu[l], mu_mv, w0[l], w2[l], a0[l], a2[l],
                                      g2[l], k_k[l], k_a[l], r_k[l], gn_w[l], gn_b[l], v0_l, v2_l, v_first)
        lambda_init = 0.8 - 0.6 * math.exp(-0.3 * l)
        o_diff = diff_attention_group(q_d, k_d, v_d, cos, sin, lam_q1[l], lam_k1[l], lam_q2[l],
                                      lam_k2[l], subln_w[l], lambda_init)
        mix = jnp.concatenate([o_rwkv, o_diff], axis=-1).astype(x.dtype) @ w_out[l]
        x = x + rms_norm(mix, post_mix_norm[l])
        h = rms_norm(x, pre_ffn_norm[l])
        ff = conv_glu_ffn(h, w_up[l], conv_w[l], conv_b[l], w_down[l])
        x = x + rms_norm(ff, post_ffn_norm[l])
    return x
```

```python
import functools
import math

import jax
import jax.numpy as jnp
from jax import lax
from jax.experimental import pallas as pl
from jax.experimental.pallas import tpu as pltpu

F32 = jnp.float32
BF16 = jnp.bfloat16

RWKV_HEAD_DIM = 64
DIFF_QK_DIM = 64
DIFF_V_DIM = 128
DECAY_LORA = 64
AAA_LORA = 64
MV_LORA = 32
GATE_LORA = 160
CONV_WIDTH = 3
ROPE_THETA = 500000.0
ROPE_DIM = DIFF_QK_DIM // 4
NORM_EPS = 1e-6
GN_EPS = 64e-5
SUBLN_EPS = 1e-5

V7X_LANES = 128
V7X_MXU_DIM = 256
V7X_VMEM_LIMIT_BYTES = 56 * 1024 * 1024

SCAN_CHUNK = 64
SCAN_GROUP_HEADS = V7X_MXU_DIM // RWKV_HEAD_DIM
SCAN_GROUP_CH = SCAN_GROUP_HEADS * RWKV_HEAD_DIM


def _pick(n, candidates):
    for c in candidates:
        if n % c == 0:
            return c
    return n


def _params(semantics):
    return pltpu.CompilerParams(dimension_semantics=semantics,
                                vmem_limit_bytes=V7X_VMEM_LIMIT_BYTES)


def _norm_matmul_kernel(x_ref, g_ref, w_ref, o_ref, xn_ref):
    @pl.when(pl.program_id(1) == 0)
    def _():
        x = x_ref[...]
        ms = jnp.mean(x * x, axis=-1, keepdims=True)
        xn_ref[...] = (x * lax.rsqrt(ms + NORM_EPS) * g_ref[...]).astype(BF16)

    o_ref[...] = jnp.dot(xn_ref[...], w_ref[...],
                         preferred_element_type=F32).astype(o_ref.dtype)


def norm_matmul(x, g, w, out_dtype=F32):
    t, d = x.shape
    n = w.shape[1]
    tm = _pick(t, (1024, 512, 256, 128))
    tn = _pick(n, (512, 256, 128))
    return pl.pallas_call(
        _norm_matmul_kernel,
        grid=(t // tm, n // tn),
        in_specs=[
            pl.BlockSpec((tm, d), lambda i, j: (i, 0)),
            pl.BlockSpec((1, d), lambda i, j: (0, 0)),
            pl.BlockSpec((d, tn), lambda i, j: (0, j)),
        ],
        out_specs=pl.BlockSpec((tm, tn), lambda i, j: (i, j)),
        out_shape=jax.ShapeDtypeStruct((t, n), out_dtype),
        scratch_shapes=[pltpu.VMEM((tm, d), BF16)],
        compiler_params=_params(("parallel", "arbitrary")),
        name="norm_matmul",
    )(x, g.reshape(1, d).astype(F32), w)


def _matmul_norm_res_kernel(a_ref, w_ref, g_ref, res_ref, o_ref, acc_ref, *, nk):
    k = pl.program_id(1)
    part = jnp.dot(a_ref[...], w_ref[...], preferred_element_type=F32)

    @pl.when(k == 0)
    def _():
        acc_ref[...] = part

    @pl.when(k > 0)
    def _():
        acc_ref[...] += part

    @pl.when(k == nk - 1)
    def _():
        m = acc_ref[...]
        ms = jnp.mean(m * m, axis=-1, keepdims=True)
        o_ref[...] = res_ref[...] + m * lax.rsqrt(ms + NORM_EPS) * g_ref[...]


def matmul_norm_res(a, w, g, res):
    t, kdim = a.shape
    d = w.shape[1]
    tm = _pick(t, (512, 256, 128))
    tk = _pick(kdim, (1024, 512, 256, 128))
    nk = kdim // tk
    return pl.pallas_call(
        functools.partial(_matmul_norm_res_kernel, nk=nk),
        grid=(t // tm, nk),
        in_specs=[
            pl.BlockSpec((tm, tk), lambda i, k: (i, k)),
            pl.BlockSpec((tk, d), lambda i, k: (k, 0)),
            pl.BlockSpec((1, d), lambda i, k: (0, 0)),
            pl.BlockSpec((tm, d), lambda i, k: (i, 0)),
        ],
        out_specs=pl.BlockSpec((tm, d), lambda i, k: (i, 0)),
        out_shape=jax.ShapeDtypeStruct((t, d), F32),
        scratch_shapes=[pltpu.VMEM((tm, d), F32)],
        compiler_params=_params(("parallel", "arbitrary")),
        name="matmul_norm_res",
    )(a, w, g.reshape(1, d).astype(F32), res)


def _scan_masks():
    c, g, gc = SCAN_CHUNK, SCAN_GROUP_HEADS, SCAN_GROUP_CH
    row = lax.broadcasted_iota(jnp.int32, (g * c, gc), 0)
    col = lax.broadcasted_iota(jnp.int32, (g * c, gc), 1)
    same_block = (row // c) == (col // RWKV_HEAD_DIM)
    strict_bd = same_block & ((row % c) > (col % c))
    rowc = lax.broadcasted_iota(jnp.int32, (c, gc), 0)
    colc = lax.broadcasted_iota(jnp.int32, (c, gc), 1)
    strict_cat = rowc > (colc % c)
    incl_cat = rowc >= (colc % c)
    rt = lax.broadcasted_iota(jnp.int32, (c, c), 0)
    ct = lax.broadcasted_iota(jnp.int32, (c, c), 1)
    tril = jnp.where(rt >= ct, 1.0, 0.0).astype(BF16)
    return same_block, strict_bd, strict_cat, incl_cat, tril


def _scan_level_masks():
    idx = jnp.arange(SCAN_GROUP_HEADS * SCAN_CHUNK)
    t, s = idx[:, None], idx[None, :]
    levels = [t // 2 == s // 2]
    b = 2
    while b < SCAN_CHUNK:
        levels.append((t // (2 * b) == s // (2 * b)) & (t % (2 * b) >= b) & (s % (2 * b) < b))
        b *= 2
    return jnp.stack(levels).astype(BF16)


def _dot(a, b):
    return jnp.dot(a, b, preferred_element_type=F32)


def _dot_nt(a, b):
    return lax.dot_general(a, b, (((1,), (1,)), ((), ())), preferred_element_type=F32)


def _dot_tn(a, b):
    return lax.dot_general(a, b, (((0,), (0,)), ((), ())), preferred_element_type=F32)


def _scan_group_chunk(lw, r, k, v, kk, a, state, masks, level_masks):
    c, g = SCAN_CHUNK, SCAN_GROUP_HEADS
    same_block, strict_bd, strict_cat, incl_cat, tril = masks

    lw_hi = lw.astype(BF16)
    lw_lo = (lw - lw_hi.astype(F32)).astype(BF16)
    cum = _dot(tril, lw_hi) + _dot(tril, lw_lo)
    w_inc = jnp.exp(cum)
    w_inv = jnp.exp(-cum)
    w_prev = jnp.exp(cum - lw)
    w_end = w_inc[c - 1:c, :]

    a_b = (-(kk * w_prev)).astype(BF16)
    b_f = kk * a * w_inv
    k_f = k * w_inv
    b_b = b_f.astype(BF16)
    k_b = k_f.astype(BF16)
    r_b = (r * w_inc).astype(BF16)
    bend_b = (b_f * w_end).astype(BF16)
    kend_b = (k_f * w_end).astype(BF16)
    v_b = v.astype(BF16)

    def tile(x):
        return jnp.concatenate([x] * g, axis=0)

    def bdexp(x):
        return jnp.where(same_block, tile(x), jnp.zeros_like(tile(x)))

    l_bd = jnp.where(strict_bd, _dot_nt(bdexp(a_b), tile(b_b)), 0.0)
    l_bd_b = l_bd.astype(BF16)
    eye = jnp.where(lax.broadcasted_iota(jnp.int32, l_bd.shape, 0)
                    == lax.broadcasted_iota(jnp.int32, l_bd.shape, 1), 1.0, 0.0)
    t_bd = eye + (l_bd_b * level_masks[0]).astype(F32)
    for lvl in range(1, level_masks.shape[0]):
        t_b = t_bd.astype(BF16)
        t_bd = t_bd + _dot(_dot(t_b, l_bd_b * level_masks[lvl]).astype(BF16), t_b)
    t_cat = t_bd[0:c]
    for h in range(1, g):
        t_cat = t_cat + t_bd[h * c:(h + 1) * c]

    ar_b = jnp.concatenate([a_b, r_b], axis=0)
    l_cat = _dot_nt(ar_b, jnp.concatenate([bdexp(b_b), bdexp(k_b)], axis=0))
    gc = SCAN_GROUP_CH
    l_ak = jnp.where(strict_cat, l_cat[:c, gc:], 0.0).astype(BF16)
    l_rb = jnp.where(incl_cat, l_cat[c:, :gc], 0.0).astype(BF16)
    l_rk = jnp.where(incl_cat, l_cat[c:, gc:], 0.0).astype(BF16)

    state_b = state.astype(BF16)
    from_state = _dot_nt(ar_b, state_b)
    from_v = _dot(jnp.concatenate([l_ak, l_rk], axis=0), bdexp(v_b))
    x = from_state[:c] + from_v[:c]
    u = _dot(t_cat.astype(BF16), bdexp(x.astype(BF16)))
    u_b = u.astype(BF16)
    o = from_state[c:] + from_v[c:] + _dot(l_rb, bdexp(u_b))

    upd = _dot_tn(jnp.concatenate([u_b, v_b], axis=0),
                  jnp.concatenate([bend_b, kend_b], axis=0))
    eye_blk = ((lax.broadcasted_iota(jnp.int32, state.shape, 0) // RWKV_HEAD_DIM)
               == (lax.broadcasted_iota(jnp.int32, state.shape, 1) // RWKV_HEAD_DIM))
    new_state = jnp.where(eye_blk, state * w_end + upd, 0.0)
    return o, new_state


def _scan_kernel(lm_ref, lw_ref, r_ref, k_ref, v_ref, kk_ref, a_ref, o_ref, state_ref, *, n_chunks, n_groups):
    @pl.when(pl.program_id(1) == 0)
    def _():
        state_ref[...] = jnp.zeros_like(state_ref)

    masks = _scan_masks()
    c, gc = SCAN_CHUNK, SCAN_GROUP_CH

    def body(ci, carry):
        t0 = pl.multiple_of(ci * c, c)
        for gi in range(n_groups):
            sl = (pl.ds(t0, c), slice(gi * gc, (gi + 1) * gc))
            o, new_state = _scan_group_chunk(
                lw_ref[sl], r_ref[sl], k_ref[sl], v_ref[sl], kk_ref[sl], a_ref[sl],
                state_ref[gi], masks, lm_ref)
            o_ref[sl] = o
            state_ref[gi] = new_state
        return carry

    lax.fori_loop(0, n_chunks, body, 0)


def rwkv7_scan(logw, r, k, v, kk, a, batch, seq):
    t, width = r.shape
    tb = _pick(seq, (256, 128, 64))
    nb = seq // tb
    n_groups = width // SCAN_GROUP_CH
    spec = pl.BlockSpec((tb, width), lambda b, s: (b * nb + s, 0))
    level_masks = _scan_level_masks()
    return pl.pallas_call(
        functools.partial(_scan_kernel, n_chunks=tb // SCAN_CHUNK, n_groups=n_groups),
        grid=(batch, nb),
        in_specs=[pl.BlockSpec(level_masks.shape, lambda b, s: (0, 0, 0))] + [spec] * 6,
        out_specs=spec,
        out_shape=jax.ShapeDtypeStruct((t, width), F32),
        scratch_shapes=[pltpu.VMEM((n_groups, SCAN_GROUP_CH, SCAN_GROUP_CH), F32)],
        compiler_params=_params(("parallel", "arbitrary")),
        name="rwkv7_scan",
    )(level_masks, logw, r, k, v, kk, a)


def _diff_attn_kernel(lam_ref, q_ref, k_ref, vt_ref, w_ref, o_ref,
                      m_ref, l_ref, acc_ref, *, tq, scale_out):
    qi = pl.program_id(2)
    q = q_ref[...]
    lane = lax.broadcasted_iota(jnp.int32, q.shape, 1)
    zero = jnp.zeros_like(q)
    q_maps = (jnp.where(lane < DIFF_QK_DIM, q, zero), jnp.where(lane >= DIFF_QK_DIM, q, zero))

    m_ref[...] = jnp.full_like(m_ref, -jnp.inf)
    l_ref[...] = jnp.zeros_like(l_ref)
    acc_ref[...] = jnp.zeros_like(acc_ref)

    def block(j, masked):
        k0 = pl.multiple_of(j * tq, tq)
        kb = k_ref[pl.ds(k0, tq), :]
        vtb = vt_ref[:, pl.ds(k0, tq)]
        for mi in range(2):
            s = _dot_nt(kb, q_maps[mi])
            if masked:
                kpos = lax.broadcasted_iota(jnp.int32, s.shape, 0)
                qpos = lax.broadcasted_iota(jnp.int32, s.shape, 1)
                s = jnp.where(kpos <= qpos, s, -jnp.inf)
            m_old = m_ref[mi]
            m_new = jnp.maximum(m_old, jnp.max(s, axis=0, keepdims=True))
            alpha = jnp.exp(m_old - m_new)
            p = jnp.exp(s - m_new)
            l_ref[mi] = l_ref[mi] * alpha + jnp.sum(p, axis=0, keepdims=True)
            acc_ref[mi] = acc_ref[mi] * alpha + _dot(vtb, p.astype(BF16))
            m_ref[mi] = m_new

    def body(j, carry):
        block(j, False)
        return carry

    lax.fori_loop(0, qi, body, 0)
    block(qi, True)

    lam = lam_ref[0]
    o_t = acc_ref[0] / l_ref[0] - lam * (acc_ref[1] / l_ref[1])
    o = o_t.T
    ms = jnp.mean(o * o, axis=-1, keepdims=True)
    o_ref[...] = (o * lax.rsqrt(ms + SUBLN_EPS) * w_ref[...] * scale_out).astype(o_ref.dtype)


def diff_attention(q, k, vt, lam, subln_w, lambda_init, batch, seq):
    t, width = q.shape
    heads = width // DIFF_V_DIM
    tq = _pick(seq, (512, 256, 128))
    nq = seq // tq
    return pl.pallas_call(
        functools.partial(_diff_attn_kernel, tq=tq, scale_out=1.0 - lambda_init),
        grid=(batch, heads, nq),
        in_specs=[
            pl.BlockSpec(memory_space=pltpu.SMEM),
            pl.BlockSpec((tq, DIFF_V_DIM), lambda b, h, i: (b * nq + i, h)),
            pl.BlockSpec((seq, DIFF_V_DIM), lambda b, h, i: (b, h)),
            pl.BlockSpec((DIFF_V_DIM, seq), lambda b, h, i: (b * heads + h, 0)),
            pl.BlockSpec((1, DIFF_V_DIM), lambda b, h, i: (0, 0)),
        ],
        out_specs=pl.BlockSpec((tq, DIFF_V_DIM), lambda b, h, i: (b * nq + i, h)),
        out_shape=jax.ShapeDtypeStruct((t, width), BF16),
        scratch_shapes=[
            pltpu.VMEM((2, 1, tq), F32),
            pltpu.VMEM((2, 1, tq), F32),
            pltpu.VMEM((2, DIFF_V_DIM, tq), F32),
        ],
        compiler_params=_params(("parallel", "parallel", "arbitrary")),
        name="diff_attention",
    )(lam.reshape(1).astype(F32), q, k, vt, subln_w.reshape(1, DIFF_V_DIM).astype(F32))


def _token_shift(p, mu):
    prev = jnp.pad(p, ((0, 0), (1, 0), (0, 0)))[:, :-1]
    return p + mu * (prev - p)


def _rotary(t, cos, sin):
    half = ROPE_DIM // 2
    c = cos[:, :, None, None, :]
    s = sin[:, :, None, None, :]
    t1 = t[..., :half]
    t2 = t[..., half:ROPE_DIM]
    return jnp.concatenate([t1 * c - t2 * s, t2 * c + t1 * s, t[..., ROPE_DIM:]], axis=-1)


def kernel(x, positions, pre_mix_norm, post_mix_norm, pre_ffn_norm, post_ffn_norm, w_in, w_mv_down, shift_mu, shift_mu_mv, w0, w2, a0, a2, g2, k_k, k_a, r_k, gn_w, gn_b, v0, v2, lam_q1, lam_k1, lam_q2, lam_k2, subln_w, w_out, w_up, conv_w, conv_b, w_down):
    batch, seq, d_model = x.shape
    t = batch * seq
    depth = w_in.shape[0]
    rwkv_width = w0.shape[1]
    rwkv_heads = rwkv_width // RWKV_HEAD_DIM
    diff_width = d_model - rwkv_width
    diff_heads = diff_width // DIFF_V_DIM
    rwkv_cols = shift_mu.shape[1]
    in_cols = w_in.shape[2]
    d_ff = w_down.shape[1]
    qk_cols = diff_heads * 2 * DIFF_QK_DIM

    inv_freq = ROPE_THETA ** (-jnp.arange(0, ROPE_DIM, 2, dtype=F32) / ROPE_DIM)
    ang = positions.astype(F32)[..., None] * inv_freq
    cos, sin = jnp.cos(ang), jnp.sin(ang)

    xf = x.reshape(t, d_model)
    v_first = None
    for l in range(depth):
        if l == 0:
            w_cat = w_in[0]
        else:
            w_cat = jnp.concatenate([w_in[l], w_mv_down[l - 1]], axis=-1)
        n_pad = (-w_cat.shape[1]) % 512
        w_cat = jnp.pad(w_cat, ((0, 0), (0, n_pad))).astype(BF16)
        p = norm_matmul(xf, pre_mix_norm[l], w_cat).reshape(batch, seq, -1)

        pm = _token_shift(p[..., :rwkv_cols], shift_mu[l])
        sizes = (rwkv_width, DECAY_LORA, rwkv_width, rwkv_width, AAA_LORA, GATE_LORA)
        offs = [0]
        for s_ in sizes:
            offs.append(offs[-1] + s_)
        r_, wl, k_, v_, al, gl = [pm[..., offs[i]:offs[i + 1]] for i in range(6)]
        w_log = -jax.nn.softplus(-(w0[l] + jnp.tanh(wl) @ w2[l])) - 0.5
        logw = -jnp.exp(w_log)
        a_ = jax.nn.sigmoid(a0[l] + al @ a2[l])
        g_ = jax.nn.sigmoid(gl) @ g2[l]
        if l == 0:
            v_first = v_
        else:
            vl = _token_shift(p[..., in_cols:in_cols + MV_LORA], shift_mu_mv[l - 1])
            v_ = v_ + (v_first - v_) * jax.nn.sigmoid(v0[l - 1] + vl @ v2[l - 1])
        heads = lambda z: z.reshape(batch, seq, rwkv_heads, RWKV_HEAD_DIM)
        kk = heads(k_ * k_k[l])
        kk = kk / jnp.maximum(jnp.sqrt(jnp.sum(kk * kk, axis=-1, keepdims=True)), 1e-12)
        k2 = k_ * (1.0 + (a_ - 1.0) * k_a[l])
        flat = lambda z: z.reshape(t, rwkv_width)
        o = rwkv7_scan(flat(logw), flat(r_), flat(k2), flat(v_), flat(kk), flat(a_), batch, seq)
        o = heads(o)
        mean = jnp.mean(o, axis=-1, keepdims=True)
        var = jnp.mean(jnp.square(o - mean), axis=-1, keepdims=True)
        o = ((o - mean) * lax.rsqrt(var + GN_EPS)).reshape(batch, seq, rwkv_width) * gn_w[l] + gn_b[l]
        bonus = jnp.sum(heads(r_) * heads(k2) * r_k[l], axis=-1, keepdims=True) * heads(v_)
        o_rwkv = (o + bonus.reshape(batch, seq, rwkv_width)) * g_

        q_d = p[..., rwkv_cols:rwkv_cols + qk_cols]
        k_d = p[..., rwkv_cols + qk_cols:rwkv_cols + 2 * qk_cols]
        v_d = p[..., rwkv_cols + 2 * qk_cols:in_cols]
        q_d = _rotary(q_d.reshape(batch, seq, diff_heads, 2, DIFF_QK_DIM), cos, sin) * (DIFF_QK_DIM ** -0.5)
        k_d = _rotary(k_d.reshape(batch, seq, diff_heads, 2, DIFF_QK_DIM), cos, sin)
        vt = jnp.transpose(v_d.reshape(batch, seq, diff_width), (0, 2, 1)).reshape(batch * diff_width, seq)
        lambda_init = 0.8 - 0.6 * math.exp(-0.3 * l)
        lam = (jnp.exp(jnp.sum(lam_q1[l] * lam_k1[l])) - jnp.exp(jnp.sum(lam_q2[l] * lam_k2[l]))
               + lambda_init)
        o_diff = diff_attention(q_d.reshape(t, qk_cols).astype(BF16), k_d.reshape(t, qk_cols).astype(BF16),
                                vt.astype(BF16), lam, subln_w[l], lambda_init, batch, seq)

        mix = jnp.concatenate([o_rwkv.reshape(t, rwkv_width).astype(BF16), o_diff], axis=-1)
        xf = matmul_norm_res(mix, w_out[l].astype(BF16), post_mix_norm[l], xf)

        u = norm_matmul(xf, pre_ffn_norm[l], w_up[l].astype(BF16)).reshape(batch, seq, 2 * d_ff)
        gate, up = u[..., :d_ff], u[..., d_ff:]
        gp = jnp.pad(gate, ((0, 0), (CONV_WIDTH - 1, 0), (0, 0)))
        gate = sum(gp[:, j:j + seq] * conv_w[l][j] for j in range(CONV_WIDTH)) + conv_b[l]
        act = (jax.nn.gelu(gate, approximate=True) * up).astype(BF16).reshape(t, d_ff)
        xf = matmul_norm_res(act, w_down[l].astype(BF16), post_ffn_norm[l], xf)
    return xf.reshape(batch, seq, d_model)
```

```python
import functools
import math

import jax
import jax.numpy as jnp
from jax import lax
from jax.experimental import pallas as pl
from jax.experimental.pallas import tpu as pltpu

F32 = jnp.float32
BF16 = jnp.bfloat16

RWKV_HEAD_DIM = 64
DIFF_QK_DIM = 64
DIFF_V_DIM = 128
DECAY_LORA = 64
AAA_LORA = 64
MV_LORA = 32
GATE_LORA = 160
CONV_WIDTH = 3
ROPE_THETA = 500000.0
ROPE_DIM = DIFF_QK_DIM // 4
NORM_EPS = 1e-6
GN_EPS = 64e-5
SUBLN_EPS = 1e-5

V7X_LANES = 128
V7X_SUBLANES = 8
V7X_MXU_DIM = 256
V7X_VMEM_LIMIT_BYTES = 56 * 1024 * 1024

SCAN_CHUNK = 64
SCAN_GROUP_HEADS = V7X_MXU_DIM // RWKV_HEAD_DIM
SCAN_GROUP_CH = SCAN_GROUP_HEADS * RWKV_HEAD_DIM
assert SCAN_CHUNK == RWKV_HEAD_DIM

LORA_W, LORA_A, LORA_G, LORA_MV = 128, 128, 256, 128
LORA_BLOCK = 768
IN_RWKV_TN = 768
IN_ATTN_TN = 512


def _pick(n, candidates):
    for c in candidates:
        if n % c == 0:
            return c
    return n


def _params(semantics):
    return pltpu.CompilerParams(dimension_semantics=semantics,
                                vmem_limit_bytes=V7X_VMEM_LIMIT_BYTES)


def _dot(a, b):
    return jnp.dot(a, b, preferred_element_type=F32)


def _dot_nt(a, b):
    return lax.dot_general(a, b, (((1,), (1,)), ((), ())), preferred_element_type=F32)


def _dot_tn(a, b):
    return lax.dot_general(a, b, (((0,), (0,)), ((), ())), preferred_element_type=F32)


def _rms_normed(x_ref, g_ref):
    x = x_ref[...]
    ms = jnp.mean(x * x, axis=-1, keepdims=True)
    return (x * lax.rsqrt(ms + NORM_EPS) * g_ref[...]).astype(BF16)


def _rows_from_prev(x, n, carry):
    rolled = pltpu.roll(x, n, 0)
    row = lax.broadcasted_iota(jnp.int32, x.shape, 0)
    for r in range(n):
        src = V7X_SUBLANES - n + r
        rolled = jnp.where(row == r, carry[src:src + 1, :], rolled)
    return rolled


def _in_rwkv_kernel(x_ref, g_ref, w_ref, mu_ref, o_ref, xn_ref, carry_ref, *, tiles_per_seq):
    i, j = pl.program_id(0), pl.program_id(1)

    @pl.when(j == 0)
    def _():
        xn_ref[...] = _rms_normed(x_ref, g_ref)

    @pl.when(i % tiles_per_seq == 0)
    def _():
        carry_ref[j] = jnp.zeros(carry_ref.shape[1:], F32)

    p = _dot(xn_ref[...], w_ref[...])
    prev = _rows_from_prev(p, 1, carry_ref[j])
    o_ref[...] = p + mu_ref[...] * (prev - p)
    carry_ref[j] = p[p.shape[0] - V7X_SUBLANES:, :]


def in_proj_rwkv(x, g, w, mu, seq):
    t, d = x.shape
    n = w.shape[1]
    tm = _pick(seq, (1024, 512, 256, 128))
    tn = IN_RWKV_TN
    nj = n // tn
    return pl.pallas_call(
        functools.partial(_in_rwkv_kernel, tiles_per_seq=seq // tm),
        grid=(t // tm, nj),
        in_specs=[
            pl.BlockSpec((tm, d), lambda i, j: (i, 0)),
            pl.BlockSpec((1, d), lambda i, j: (0, 0)),
            pl.BlockSpec((d, tn), lambda i, j: (0, j)),
            pl.BlockSpec((1, tn), lambda i, j: (0, j)),
        ],
        out_specs=pl.BlockSpec((tm, tn), lambda i, j: (i, j)),
        out_shape=jax.ShapeDtypeStruct((t, n), F32),
        scratch_shapes=[pltpu.VMEM((tm, d), BF16), pltpu.VMEM((nj, V7X_SUBLANES, tn), F32)],
        compiler_params=_params(("arbitrary", "arbitrary")),
        name="in_proj_rwkv",
    )(x, g.reshape(1, d).astype(F32), w, mu.reshape(1, n).astype(F32))


def _in_attn_kernel(x_ref, g_ref, w_ref, rope_ref, o_ref, xn_ref, *, n_q_tiles, n_rot_tiles):
    j = pl.program_id(1)

    @pl.when(j == 0)
    def _():
        xn_ref[...] = _rms_normed(x_ref, g_ref)

    p = _dot(xn_ref[...], w_ref[...])

    @pl.when(j < n_rot_tiles)
    def _():
        scale = jnp.where(j < n_q_tiles, DIFF_QK_DIM ** -0.5, 1.0).astype(F32)
        cos = rope_ref[:, 0:V7X_LANES] * scale
        sin_lo = rope_ref[:, V7X_LANES:2 * V7X_LANES] * scale
        sin_hi = rope_ref[:, 2 * V7X_LANES:] * scale
        half = ROPE_DIM // 2
        for c0 in range(0, p.shape[1], V7X_LANES):
            xg = p[:, c0:c0 + V7X_LANES]
            rot = (xg * cos + pltpu.roll(xg, V7X_LANES - half, 1) * sin_lo
                   + pltpu.roll(xg, half, 1) * sin_hi)
            o_ref[:, c0:c0 + V7X_LANES] = rot.astype(o_ref.dtype)

    @pl.when(j >= n_rot_tiles)
    def _():
        o_ref[...] = p.astype(o_ref.dtype)


def in_proj_attn(x, g, w, rope, seq, qk_cols):
    t, d = x.shape
    n = w.shape[1]
    tm = _pick(seq, (1024, 512, 256, 128))
    tn = IN_ATTN_TN
    return pl.pallas_call(
        functools.partial(_in_attn_kernel, n_q_tiles=qk_cols // tn, n_rot_tiles=2 * qk_cols // tn),
        grid=(t // tm, n // tn),
        in_specs=[
            pl.BlockSpec((tm, d), lambda i, j: (i, 0)),
            pl.BlockSpec((1, d), lambda i, j: (0, 0)),
            pl.BlockSpec((d, tn), lambda i, j: (0, j)),
            pl.BlockSpec((tm, 3 * V7X_LANES), lambda i, j: (i, 0)),
        ],
        out_specs=pl.BlockSpec((tm, tn), lambda i, j: (i, j)),
        out_shape=jax.ShapeDtypeStruct((t, n), BF16),
        scratch_shapes=[pltpu.VMEM((tm, d), BF16)],
        compiler_params=_params(("parallel", "arbitrary")),
        name="in_proj_attn",
    )(x, g.reshape(1, d).astype(F32), w, rope)


def _rope_tables(positions):
    half = ROPE_DIM // 2
    inv_freq = ROPE_THETA ** (-jnp.arange(0, ROPE_DIM, 2, dtype=F32) / ROPE_DIM)
    ang = positions.astype(F32).reshape(-1, 1) * inv_freq
    cos, sin = jnp.cos(ang), jnp.sin(ang)
    t = ang.shape[0]
    pad = DIFF_QK_DIM - ROPE_DIM
    cos_map = jnp.concatenate([cos, cos, jnp.ones((t, pad), F32)], axis=-1)
    lo_map = jnp.concatenate([-sin, jnp.zeros((t, half + pad), F32)], axis=-1)
    hi_map = jnp.concatenate([jnp.zeros((t, half), F32), sin, jnp.zeros((t, pad), F32)], axis=-1)
    reps = V7X_LANES // DIFF_QK_DIM
    return jnp.concatenate([jnp.tile(m, (1, reps)) for m in (cos_map, lo_map, hi_map)], axis=-1)


def _scan_masks():
    c, g, gc = SCAN_CHUNK, SCAN_GROUP_HEADS, SCAN_GROUP_CH
    row = lax.broadcasted_iota(jnp.int32, (g * c, gc), 0)
    col = lax.broadcasted_iota(jnp.int32, (g * c, gc), 1)
    same_block = (row // c) == (col // c)
    strict_bd = same_block & ((row % c) > (col % c))
    rowc = lax.broadcasted_iota(jnp.int32, (c, gc), 0)
    colc = lax.broadcasted_iota(jnp.int32, (c, gc), 1)
    strict_cat = rowc > (colc % c)
    incl_cat = rowc >= (colc % c)
    rt = lax.broadcasted_iota(jnp.int32, (c, c), 0)
    ct = lax.broadcasted_iota(jnp.int32, (c, c), 1)
    tril = jnp.where(rt >= ct, 1.0, 0.0).astype(BF16)
    return same_block, strict_bd, strict_cat, incl_cat, tril


def _scan_constants():
    idx = jnp.arange(SCAN_GROUP_HEADS * SCAN_CHUNK)
    t, s = idx[:, None], idx[None, :]
    mats = [t // 2 == s // 2]
    b = 2
    while b < SCAN_CHUNK:
        mats.append((t // (2 * b) == s // (2 * b)) & (t % (2 * b) >= b) & (s % (2 * b) < b))
        b *= 2
    mats.append(t // RWKV_HEAD_DIM == s // RWKV_HEAD_DIM)
    return jnp.stack(mats).astype(BF16)


def _head_sum(x, ones_bd):
    hi = x.astype(BF16)
    lo = (x - hi.astype(F32)).astype(BF16)
    return _dot(hi, ones_bd) + _dot(lo, ones_bd)


def _scan_group_chunk(lw, r, k, v, kk, a, state, masks, const_ref):
    c, g = SCAN_CHUNK, SCAN_GROUP_HEADS
    same_block, strict_bd, strict_cat, incl_cat, tril = masks
    n_levels = const_ref.shape[0] - 1

    lw_hi = lw.astype(BF16)
    lw_lo = (lw - lw_hi.astype(F32)).astype(BF16)
    cum = _dot(tril, lw_hi) + _dot(tril, lw_lo)
    w_inc = jnp.exp(cum)
    w_inv = jnp.exp(-cum)
    w_prev = jnp.exp(cum - lw)
    w_end = w_inc[c - 1:c, :]

    a_b = (-(kk * w_prev)).astype(BF16)
    b_f = kk * a * w_inv
    k_f = k * w_inv
    b_b = b_f.astype(BF16)
    k_b = k_f.astype(BF16)
    r_b = (r * w_inc).astype(BF16)
    bend_b = (b_f * w_end).astype(BF16)
    kend_b = (k_f * w_end).astype(BF16)
    v_b = v.astype(BF16)

    def tile(x):
        return jnp.concatenate([x] * g, axis=0)

    def bdexp(x):
        return jnp.where(same_block, tile(x), jnp.zeros_like(tile(x)))

    l_bd = jnp.where(strict_bd, _dot_nt(bdexp(a_b), tile(b_b)), 0.0)
    l_bd_b = l_bd.astype(BF16)
    eye = jnp.where(lax.broadcasted_iota(jnp.int32, l_bd.shape, 0)
                    == lax.broadcasted_iota(jnp.int32, l_bd.shape, 1), 1.0, 0.0)
    t_bd = eye + (l_bd_b * const_ref[0]).astype(F32)
    for lvl in range(1, n_levels):
        t_b = t_bd.astype(BF16)
        t_bd = t_bd + _dot(_dot(t_b, l_bd_b * const_ref[lvl]).astype(BF16), t_b)
    t_cat = t_bd[0:c]
    for h in range(1, g):
        t_cat = t_cat + t_bd[h * c:(h + 1) * c]

    ar_b = jnp.concatenate([a_b, r_b], axis=0)
    l_cat = _dot_nt(ar_b, jnp.concatenate([bdexp(b_b), bdexp(k_b)], axis=0))
    gc = SCAN_GROUP_CH
    l_ak = jnp.where(strict_cat, l_cat[:c, gc:], 0.0).astype(BF16)
    l_rb = jnp.where(incl_cat, l_cat[c:, :gc], 0.0).astype(BF16)
    l_rk = jnp.where(incl_cat, l_cat[c:, gc:], 0.0).astype(BF16)

    state_b = state.astype(BF16)
    from_state = _dot_nt(ar_b, state_b)
    from_v = _dot(jnp.concatenate([l_ak, l_rk], axis=0), bdexp(v_b))
    x = from_state[:c] + from_v[:c]
    u = _dot(t_cat.astype(BF16), bdexp(x.astype(BF16)))
    u_b = u.astype(BF16)
    o = from_state[c:] + from_v[c:] + _dot(l_rb, bdexp(u_b))

    upd = _dot_tn(jnp.concatenate([u_b, v_b], axis=0),
                  jnp.concatenate([bend_b, kend_b], axis=0))
    new_state = jnp.where(same_block, state * w_end + upd, 0.0)
    return o, new_state


_VEC_W0, _VEC_A0, _VEC_KK, _VEC_KA, _VEC_RK, _VEC_GNW, _VEC_GNB, _VEC_V0 = range(8)


def _rwkv_kernel(*refs, n_chunks, n_groups, has_vfirst):
    if has_vfirst:
        (const_ref, vec_ref, w2_ref, a2_ref, g2_ref, v2_ref, r_ref, k_ref, v_ref, lora_ref, vf_ref,
         o_ref, state_ref, lw_s, a_s, kk_s, k2_s, v_s, o_s) = refs
    else:
        (const_ref, vec_ref, w2_ref, a2_ref, g2_ref, r_ref, k_ref, v_ref, lora_ref,
         o_ref, state_ref, lw_s, a_s, kk_s, k2_s, v_s, o_s) = refs
        v2_ref = vf_ref = None

    @pl.when(pl.program_id(1) == 0)
    def _():
        state_ref[...] = jnp.zeros_like(state_ref)

    c, gc = SCAN_CHUNK, SCAN_GROUP_CH
    ones_bd = const_ref[const_ref.shape[0] - 1]
    vec = lambda row, sl: vec_ref[row:row + 1, sl]

    o0 = LORA_W
    o1 = o0 + LORA_A
    o2 = o1 + LORA_G
    o3 = o2 + LORA_MV
    tanh_wl = jnp.tanh(lora_ref[:, 0:o0]).astype(BF16)
    al = lora_ref[:, o0:o1].astype(BF16)
    sig_gl = jax.nn.sigmoid(lora_ref[:, o1:o2]).astype(BF16)

    for gi in range(n_groups):
        sl = slice(gi * gc, (gi + 1) * gc)
        zw = vec(_VEC_W0, sl) + _dot(tanh_wl, w2_ref[:, sl])
        softplus_neg = jnp.maximum(-zw, 0.0) + jnp.log(1.0 + jnp.exp(-jnp.abs(zw)))
        lw_s[:, sl] = -jnp.exp(-softplus_neg - 0.5)
        a = jax.nn.sigmoid(vec(_VEC_A0, sl) + _dot(al, a2_ref[:, sl]))
        a_s[:, sl] = a
        v = v_ref[:, sl]
        if has_vfirst:
            mix = jax.nn.sigmoid(vec(_VEC_V0, sl)
                                 + _dot(lora_ref[:, o2:o3].astype(BF16), v2_ref[:, sl]))
            v = v + (vf_ref[:, sl] - v) * mix
        v_s[:, sl] = v
        k = k_ref[:, sl]
        kk = k * vec(_VEC_KK, sl)
        ss = _head_sum(kk * kk, ones_bd)
        kk_s[:, sl] = kk * lax.rsqrt(jnp.maximum(ss, 1e-24))
        k2_s[:, sl] = k * (1.0 + (a - 1.0) * vec(_VEC_KA, sl))

    masks = _scan_masks()

    def body(ci, carry):
        t0 = pl.multiple_of(ci * c, c)
        for gi in range(n_groups):
            sl = (pl.ds(t0, c), slice(gi * gc, (gi + 1) * gc))
            o, new_state = _scan_group_chunk(
                lw_s[sl], r_ref[sl], k2_s[sl], v_s[sl], kk_s[sl], a_s[sl],
                state_ref[gi], masks, const_ref)
            o_s[sl] = o
            state_ref[gi] = new_state
        return carry

    lax.fori_loop(0, n_chunks, body, 0)

    inv_n = 1.0 / RWKV_HEAD_DIM
    for gi in range(n_groups):
        sl = slice(gi * gc, (gi + 1) * gc)
        o = o_s[:, sl]
        dev = o - _head_sum(o, ones_bd) * inv_n
        var = _head_sum(dev * dev, ones_bd) * inv_n
        normed = dev * lax.rsqrt(var + GN_EPS) * vec(_VEC_GNW, sl) + vec(_VEC_GNB, sl)
        bonus = _head_sum(r_ref[:, sl] * k2_s[:, sl] * vec(_VEC_RK, sl), ones_bd) * v_s[:, sl]
        gate = _dot(sig_gl, g2_ref[:, sl])
        o_ref[:, sl] = ((normed + bonus) * gate).astype(o_ref.dtype)


def rwkv7_mixer(pr, pr_first, vecs, w2p, a2p, g2p, v2p, batch, seq, width):
    t = pr.shape[0]
    tb = _pick(seq, (256, 128, 64))
    nb = seq // tb
    n_groups = width // SCAN_GROUP_CH
    has_vfirst = pr_first is not None
    consts = _scan_constants()
    lora_col = 3 * width // LORA_BLOCK
    row_spec = lambda col: pl.BlockSpec((tb, width), lambda b, s: (b * nb + s, col))
    full = lambda arr: pl.BlockSpec(arr.shape, lambda b, s: (0,) * arr.ndim)
    weights = [w2p, a2p, g2p] + ([v2p] if has_vfirst else [])
    in_specs = ([full(consts), full(vecs)] + [full(w) for w in weights]
                + [row_spec(0), row_spec(1), row_spec(2),
                   pl.BlockSpec((tb, LORA_BLOCK), lambda b, s: (b * nb + s, lora_col))]
                + ([row_spec(2)] if has_vfirst else []))
    args = [consts, vecs] + weights + [pr, pr, pr, pr] + ([pr_first] if has_vfirst else [])
    return pl.pallas_call(
        functools.partial(_rwkv_kernel, n_chunks=tb // SCAN_CHUNK, n_groups=n_groups,
                          has_vfirst=has_vfirst),
        grid=(batch, nb),
        in_specs=in_specs,
        out_specs=pl.BlockSpec((tb, width), lambda b, s: (b * nb + s, 0)),
        out_shape=jax.ShapeDtypeStruct((t, width), BF16),
        scratch_shapes=([pltpu.VMEM((n_groups, SCAN_GROUP_CH, SCAN_GROUP_CH), F32)]
                        + [pltpu.VMEM((tb, width), F32)] * 6),
        compiler_params=_params(("parallel", "arbitrary")),
        name="rwkv7_mixer",
    )(*args)


def _diff_attn_kernel(lam_ref, q_ref, k_ref, v_ref, w_ref, o_ref,
                      vt_ref, m_ref, l_ref, acc_ref, *, tq, scale_out):
    qi = pl.program_id(2)

    @pl.when(qi == 0)
    def _():
        for c0 in range(0, v_ref.shape[0], tq):
            vt_ref[:, c0:c0 + tq] = v_ref[c0:c0 + tq, :].astype(F32).T.astype(BF16)

    q = q_ref[...]
    lane = lax.broadcasted_iota(jnp.int32, q.shape, 1)
    zero = jnp.zeros_like(q)
    q_maps = (jnp.where(lane < DIFF_QK_DIM, q, zero), jnp.where(lane >= DIFF_QK_DIM, q, zero))

    m_ref[...] = jnp.full_like(m_ref, -jnp.inf)
    l_ref[...] = jnp.zeros_like(l_ref)
    acc_ref[...] = jnp.zeros_like(acc_ref)

    def block(j, masked):
        k0 = pl.multiple_of(j * tq, tq)
        kb = k_ref[pl.ds(k0, tq), :]
        vtb = vt_ref[:, pl.ds(k0, tq)]
        for mi in range(2):
            s = _dot_nt(kb, q_maps[mi])
            if masked:
                kpos = lax.broadcasted_iota(jnp.int32, s.shape, 0)
                qpos = lax.broadcasted_iota(jnp.int32, s.shape, 1)
                s = jnp.where(kpos <= qpos, s, -jnp.inf)
            m_old = m_ref[mi]
            m_new = jnp.maximum(m_old, jnp.max(s, axis=0, keepdims=True))
            alpha = jnp.exp(m_old - m_new)
            p = jnp.exp(s - m_new)
            l_ref[mi] = l_ref[mi] * alpha + jnp.sum(p, axis=0, keepdims=True)
            acc_ref[mi] = acc_ref[mi] * alpha + _dot(vtb, p.astype(BF16))
            m_ref[mi] = m_new

    def body(j, carry):
        block(j, False)
        return carry

    lax.fori_loop(0, qi, body, 0)
    block(qi, True)

    lam = lam_ref[0]
    o_t = acc_ref[0] / l_ref[0] - lam * (acc_ref[1] / l_ref[1])
    o = o_t.T
    ms = jnp.mean(o * o, axis=-1, keepdims=True)
    o_ref[...] = (o * lax.rsqrt(ms + SUBLN_EPS) * w_ref[...] * scale_out).astype(o_ref.dtype)


def diff_attention(qkv, lam, subln_w, lambda_init, batch, seq, heads):
    t = qkv.shape[0]
    tq = _pick(seq, (512, 256, 128))
    nq = seq // tq
    return pl.pallas_call(
        functools.partial(_diff_attn_kernel, tq=tq, scale_out=1.0 - lambda_init),
        grid=(batch, heads, nq),
        in_specs=[
            pl.BlockSpec(memory_space=pltpu.SMEM),
            pl.BlockSpec((tq, DIFF_V_DIM), lambda b, h, i: (b * nq + i, h)),
            pl.BlockSpec((seq, DIFF_V_DIM), lambda b, h, i: (b, heads + h)),
            pl.BlockSpec((seq, DIFF_V_DIM), lambda b, h, i: (b, 2 * heads + h)),
            pl.BlockSpec((1, DIFF_V_DIM), lambda b, h, i: (0, 0)),
        ],
        out_specs=pl.BlockSpec((tq, DIFF_V_DIM), lambda b, h, i: (b * nq + i, h)),
        out_shape=jax.ShapeDtypeStruct((t, heads * DIFF_V_DIM), BF16),
        scratch_shapes=[
            pltpu.VMEM((DIFF_V_DIM, seq), BF16),
            pltpu.VMEM((2, 1, tq), F32),
            pltpu.VMEM((2, 1, tq), F32),
            pltpu.VMEM((2, DIFF_V_DIM, tq), F32),
        ],
        compiler_params=_params(("parallel", "parallel", "arbitrary")),
        name="diff_attention",
    )(lam.reshape(1).astype(F32), qkv, qkv, qkv, subln_w.reshape(1, DIFF_V_DIM).astype(F32))


def _out_proj_kernel(a1_ref, a2_ref, w_ref, g_ref, res_ref, o_ref):
    k1 = a1_ref.shape[1]
    m = _dot(a1_ref[...], w_ref[0:k1, :]) + _dot(a2_ref[...], w_ref[k1:, :])
    ms = jnp.mean(m * m, axis=-1, keepdims=True)
    o_ref[...] = res_ref[...] + m * lax.rsqrt(ms + NORM_EPS) * g_ref[...]


def out_proj(a1, a2, w, g, res):
    t, k1 = a1.shape
    k2 = a2.shape[1]
    d = w.shape[1]
    tm = _pick(t, (512, 256, 128))
    return pl.pallas_call(
        _out_proj_kernel,
        grid=(t // tm,),
        in_specs=[
            pl.BlockSpec((tm, k1), lambda i: (i, 0)),
            pl.BlockSpec((tm, k2), lambda i: (i, 0)),
            pl.BlockSpec((k1 + k2, d), lambda i: (0, 0)),
            pl.BlockSpec((1, d), lambda i: (0, 0)),
            pl.BlockSpec((tm, d), lambda i: (i, 0)),
        ],
        out_specs=pl.BlockSpec((tm, d), lambda i: (i, 0)),
        out_shape=jax.ShapeDtypeStruct((t, d), F32),
        compiler_params=_params(("parallel",)),
        name="out_proj",
    )(a1, a2, w, g.reshape(1, d).astype(F32), res)


def _ffn_up_kernel(x_ref, g_ref, wg_ref, wu_ref, cw_ref, o_ref, xn_ref, carry_ref, *, tiles_per_seq):
    i, j = pl.program_id(0), pl.program_id(1)

    @pl.when(j == 0)
    def _():
        xn_ref[...] = _rms_normed(x_ref, g_ref)

    @pl.when(i % tiles_per_seq == 0)
    def _():
        carry_ref[j] = jnp.zeros(carry_ref.shape[1:], F32)

    xn = xn_ref[...]
    gate = _dot(xn, wg_ref[...])
    up = _dot(xn, wu_ref[...])
    carry = carry_ref[j]
    conv = (_rows_from_prev(gate, 2, carry) * cw_ref[0:1, :]
            + _rows_from_prev(gate, 1, carry) * cw_ref[1:2, :]
            + gate * cw_ref[2:3, :] + cw_ref[3:4, :])
    carry_ref[j] = gate[gate.shape[0] - V7X_SUBLANES:, :]
    inner = math.sqrt(2.0 / math.pi) * (conv + 0.044715 * (conv * conv * conv))
    o_ref[...] = (0.5 * conv * (1.0 + jnp.tanh(inner)) * up).astype(o_ref.dtype)


def ffn_up(x, g, w_up, conv_wb, seq):
    t, d = x.shape
    f = w_up.shape[1] // 2
    tm = _pick(seq, (1024, 512, 256, 128))
    tn = _pick(f, (512, 256, 128))
    nj = f // tn
    return pl.pallas_call(
        functools.partial(_ffn_up_kernel, tiles_per_seq=seq // tm),
        grid=(t // tm, nj),
        in_specs=[
            pl.BlockSpec((tm, d), lambda i, j: (i, 0)),
            pl.BlockSpec((1, d), lambda i, j: (0, 0)),
            pl.BlockSpec((d, tn), lambda i, j: (0, j)),
            pl.BlockSpec((d, tn), lambda i, j: (0, j + nj)),
            pl.BlockSpec((CONV_WIDTH + 1, tn), lambda i, j: (0, j)),
        ],
        out_specs=pl.BlockSpec((tm, tn), lambda i, j: (i, j)),
        out_shape=jax.ShapeDtypeStruct((t, f), BF16),
        scratch_shapes=[pltpu.VMEM((tm, d), BF16), pltpu.VMEM((nj, V7X_SUBLANES, tn), F32)],
        compiler_params=_params(("arbitrary", "arbitrary")),
        name="ffn_up",
    )(x, g.reshape(1, d).astype(F32), w_up, w_up, conv_wb)


def _ffn_down_kernel(a_ref, w_ref, g_ref, res_ref, o_ref, *, nk):
    k = pl.program_id(1)
    part = _dot(a_ref[...], w_ref[...])

    @pl.when(k == 0)
    def _():
        o_ref[...] = part

    @pl.when(k > 0)
    def _():
        o_ref[...] += part

    @pl.when(k == nk - 1)
    def _():
        m = o_ref[...]
        ms = jnp.mean(m * m, axis=-1, keepdims=True)
        o_ref[...] = res_ref[...] + m * lax.rsqrt(ms + NORM_EPS) * g_ref[...]


def ffn_down(a, w, g, res):
    t, kdim = a.shape
    d = w.shape[1]
    tm = _pick(t, (1024, 512, 256, 128))
    tk = _pick(kdim, (512, 256, 128))
    nk = kdim // tk
    return pl.pallas_call(
        functools.partial(_ffn_down_kernel, nk=nk),
        grid=(t // tm, nk),
        in_specs=[
            pl.BlockSpec((tm, tk), lambda i, k: (i, k)),
            pl.BlockSpec((tk, d), lambda i, k: (k, 0)),
            pl.BlockSpec((1, d), lambda i, k: (0, 0)),
            pl.BlockSpec((tm, d), lambda i, k: (i, 0)),
        ],
        out_specs=pl.BlockSpec((tm, d), lambda i, k: (i, 0)),
        out_shape=jax.ShapeDtypeStruct((t, d), F32),
        compiler_params=_params(("parallel", "arbitrary")),
        name="ffn_down",
    )(a, w, g.reshape(1, d).astype(F32), res)


def _pad_cols(w, width):
    return jnp.pad(w, [(0, 0)] * (w.ndim - 1) + [(0, width - w.shape[-1])])


def _pad_rows(w, rows):
    return jnp.pad(w, ((0, rows - w.shape[0]), (0, 0)))


def kernel(x, positions, pre_mix_norm, post_mix_norm, pre_ffn_norm, post_ffn_norm, w_in, w_mv_down, shift_mu, shift_mu_mv, w0, w2, a0, a2, g2, k_k, k_a, r_k, gn_w, gn_b, v0, v2, lam_q1, lam_k1, lam_q2, lam_k2, subln_w, w_out, w_up, conv_w, conv_b, w_down):
    batch, seq, d_model = x.shape
    t = batch * seq
    depth = w_in.shape[0]
    width = w0.shape[1]
    diff_heads = (d_model - width) // DIFF_V_DIM
    rwkv_cols = shift_mu.shape[1]
    qk_cols = diff_heads * 2 * DIFF_QK_DIM
    rwkv_n = 3 * width + LORA_BLOCK
    assert rwkv_n % IN_RWKV_TN == 0 and (3 * width) % LORA_BLOCK == 0

    rope = _rope_tables(positions)
    xf = x.reshape(t, d_model)
    pr_first = None
    for l in range(depth):
        o_r, o_wl = 0, width
        o_k = o_wl + DECAY_LORA
        o_v = o_k + width
        o_al = o_v + width
        o_gl = o_al + AAA_LORA
        wl_cols = lambda m: _pad_cols(m[..., o_wl:o_k], LORA_W)
        al_cols = lambda m: _pad_cols(m[..., o_al:o_gl], LORA_A)
        gl_cols = lambda m: _pad_cols(m[..., o_gl:rwkv_cols], LORA_G)
        if l == 0:
            mv_w = jnp.zeros((d_model, LORA_MV), F32)
            mv_mu = jnp.zeros((LORA_MV,), F32)
        else:
            mv_w = _pad_cols(w_mv_down[l - 1], LORA_MV)
            mv_mu = _pad_cols(shift_mu_mv[l - 1], LORA_MV)
        arrange = lambda m, mv: jnp.concatenate(
            [m[..., o_r:o_wl], m[..., o_k:o_v], m[..., o_v:o_al], wl_cols(m), al_cols(m), gl_cols(m), mv],
            axis=-1)
        w_rwkv = _pad_cols(arrange(w_in[l], mv_w), rwkv_n).astype(BF16)
        mu_rwkv = _pad_cols(arrange(shift_mu[l], mv_mu), rwkv_n)
        pr = in_proj_rwkv(xf, pre_mix_norm[l], w_rwkv, mu_rwkv, seq)

        vecs = jnp.stack([w0[l], a0[l], k_k[l], k_a[l], r_k[l].reshape(width), gn_w[l], gn_b[l],
                          v0[l - 1] if l > 0 else jnp.zeros((width,), F32)]).astype(F32)
        w2p = _pad_rows(w2[l], LORA_W).astype(BF16)
        a2p = _pad_rows(a2[l], LORA_A).astype(BF16)
        g2p = _pad_rows(g2[l], LORA_G).astype(BF16)
        v2p = _pad_rows(v2[l - 1], LORA_MV).astype(BF16) if l > 0 else None
        o_rwkv = rwkv7_mixer(pr, pr_first, vecs, w2p, a2p, g2p, v2p, batch, seq, width)
        if l == 0:
            pr_first = pr

        qkv = in_proj_attn(xf, pre_mix_norm[l], w_in[l][:, rwkv_cols:].astype(BF16), rope, seq, qk_cols)
        lambda_init = 0.8 - 0.6 * math.exp(-0.3 * l)
        lam = (jnp.exp(jnp.sum(lam_q1[l] * lam_k1[l])) - jnp.exp(jnp.sum(lam_q2[l] * lam_k2[l]))
               + lambda_init)
        o_diff = diff_attention(qkv, lam, subln_w[l], lambda_init, batch, seq, diff_heads)

        xf = out_proj(o_rwkv, o_diff, w_out[l].astype(BF16), post_mix_norm[l], xf)

        conv_wb = jnp.concatenate([conv_w[l], conv_b[l][None, :]], axis=0).astype(F32)
        act = ffn_up(xf, pre_ffn_norm[l], w_up[l].astype(BF16), conv_wb, seq)
        xf = ffn_down(act, w_down[l].astype(BF16), post_ffn_norm[l], xf)
    return xf.reshape(batch, seq, d_model)
```

```python
import functools
import math

import jax
import jax.numpy as jnp
from jax import lax
from jax.experimental import pallas as pl
from jax.experimental.pallas import tpu as pltpu

F32 = jnp.float32
BF16 = jnp.bfloat16

RWKV_HEAD_DIM = 64
DIFF_QK_DIM = 64
DIFF_V_DIM = 128
DECAY_LORA = 64
AAA_LORA = 64
MV_LORA = 32
GATE_LORA = 160
CONV_WIDTH = 3
ROPE_THETA = 500000.0
ROPE_DIM = DIFF_QK_DIM // 4
NORM_EPS = 1e-6
GN_EPS = 64e-5
SUBLN_EPS = 1e-5

V7X_LANES = 128
V7X_SUBLANES = 8
V7X_MXU_DIM = 256
V7X_VMEM_LIMIT_BYTES = 56 * 1024 * 1024

SCAN_CHUNK = 64
SCAN_GROUP_HEADS = V7X_MXU_DIM // RWKV_HEAD_DIM
SCAN_GROUP_CH = SCAN_GROUP_HEADS * RWKV_HEAD_DIM
assert SCAN_CHUNK == RWKV_HEAD_DIM

LORA_W, LORA_A, LORA_G, LORA_MV = 128, 128, 256, 128
LORA_BLOCK = 768
IN_RWKV_TN = 768
IN_ATTN_TN = 512
ATTN_VT_ROWS = DIFF_V_DIM + 16


def _pick(n, candidates):
    for c in candidates:
        if n % c == 0:
            return c
    return n


def _params(semantics):
    return pltpu.CompilerParams(dimension_semantics=semantics,
                                vmem_limit_bytes=V7X_VMEM_LIMIT_BYTES)


def _dot(a, b):
    return jnp.dot(a, b, preferred_element_type=F32)


def _dot_nt(a, b):
    return lax.dot_general(a, b, (((1,), (1,)), ((), ())), preferred_element_type=F32)


def _dot_tn(a, b):
    return lax.dot_general(a, b, (((0,), (0,)), ((), ())), preferred_element_type=F32)


def _rms_normed(x_ref, g_ref):
    x = x_ref[...]
    ms = jnp.mean(x * x, axis=-1, keepdims=True)
    return (x * lax.rsqrt(ms + NORM_EPS) * g_ref[...]).astype(BF16)


def _rows_from_prev(x, n, carry):
    rolled = pltpu.roll(x, n, 0)
    row = lax.broadcasted_iota(jnp.int32, x.shape, 0)
    for r in range(n):
        src = V7X_SUBLANES - n + r
        rolled = jnp.where(row == r, carry[src:src + 1, :], rolled)
    return rolled


def _in_rwkv_kernel(x_ref, g_ref, w_ref, mu_ref, o_ref, xn_ref, carry_ref, *, tiles_per_seq):
    i, j = pl.program_id(0), pl.program_id(1)

    @pl.when(j == 0)
    def _():
        xn_ref[...] = _rms_normed(x_ref, g_ref)

    @pl.when(i % tiles_per_seq == 0)
    def _():
        carry_ref[j] = jnp.zeros(carry_ref.shape[1:], F32)

    p = _dot(xn_ref[...], w_ref[...])
    prev = _rows_from_prev(p, 1, carry_ref[j])
    o_ref[...] = p + mu_ref[...] * (prev - p)
    carry_ref[j] = p[p.shape[0] - V7X_SUBLANES:, :]


def in_proj_rwkv(x, g, w, mu, seq):
    t, d = x.shape
    n = w.shape[1]
    tm = _pick(seq, (1024, 512, 256, 128))
    tn = IN_RWKV_TN
    nj = n // tn
    return pl.pallas_call(
        functools.partial(_in_rwkv_kernel, tiles_per_seq=seq // tm),
        grid=(t // tm, nj),
        in_specs=[
            pl.BlockSpec((tm, d), lambda i, j: (i, 0)),
            pl.BlockSpec((1, d), lambda i, j: (0, 0)),
            pl.BlockSpec((d, tn), lambda i, j: (0, j)),
            pl.BlockSpec((1, tn), lambda i, j: (0, j)),
        ],
        out_specs=pl.BlockSpec((tm, tn), lambda i, j: (i, j)),
        out_shape=jax.ShapeDtypeStruct((t, n), F32),
        scratch_shapes=[pltpu.VMEM((tm, d), BF16), pltpu.VMEM((nj, V7X_SUBLANES, tn), F32)],
        compiler_params=_params(("arbitrary", "arbitrary")),
        name="in_proj_rwkv",
    )(x, g.reshape(1, d).astype(F32), w, mu.reshape(1, n).astype(F32))


def _in_attn_kernel(x_ref, g_ref, w_ref, rope_ref, o_ref, xn_ref, *, n_q_tiles, n_rot_tiles):
    j = pl.program_id(1)

    @pl.when(j == 0)
    def _():
        xn_ref[...] = _rms_normed(x_ref, g_ref)

    p = _dot(xn_ref[...], w_ref[...])

    @pl.when(j < n_rot_tiles)
    def _():
        scale = jnp.where(j < n_q_tiles, DIFF_QK_DIM ** -0.5 * math.log2(math.e), 1.0).astype(F32)
        cos = rope_ref[:, 0:V7X_LANES] * scale
        sin_lo = rope_ref[:, V7X_LANES:2 * V7X_LANES] * scale
        sin_hi = rope_ref[:, 2 * V7X_LANES:] * scale
        half = ROPE_DIM // 2
        for c0 in range(0, p.shape[1], V7X_LANES):
            xg = p[:, c0:c0 + V7X_LANES]
            rot = (xg * cos + pltpu.roll(xg, V7X_LANES - half, 1) * sin_lo
                   + pltpu.roll(xg, half, 1) * sin_hi)
            o_ref[:, c0:c0 + V7X_LANES] = rot.astype(o_ref.dtype)

    @pl.when(j >= n_rot_tiles)
    def _():
        o_ref[...] = p.astype(o_ref.dtype)


def in_proj_attn(x, g, w, rope, seq, qk_cols):
    t, d = x.shape
    n = w.shape[1]
    tm = _pick(seq, (1024, 512, 256, 128))
    tn = IN_ATTN_TN
    return pl.pallas_call(
        functools.partial(_in_attn_kernel, n_q_tiles=qk_cols // tn, n_rot_tiles=2 * qk_cols // tn),
        grid=(t // tm, n // tn),
        in_specs=[
            pl.BlockSpec((tm, d), lambda i, j: (i, 0)),
            pl.BlockSpec((1, d), lambda i, j: (0, 0)),
            pl.BlockSpec((d, tn), lambda i, j: (0, j)),
            pl.BlockSpec((tm, 3 * V7X_LANES), lambda i, j: (i, 0)),
        ],
        out_specs=pl.BlockSpec((tm, tn), lambda i, j: (i, j)),
        out_shape=jax.ShapeDtypeStruct((t, n), BF16),
        scratch_shapes=[pltpu.VMEM((tm, d), BF16)],
        compiler_params=_params(("parallel", "arbitrary")),
        name="in_proj_attn",
    )(x, g.reshape(1, d).astype(F32), w, rope)


def _rope_tables(positions):
    half = ROPE_DIM // 2
    inv_freq = ROPE_THETA ** (-jnp.arange(0, ROPE_DIM, 2, dtype=F32) / ROPE_DIM)
    ang = positions.astype(F32).reshape(-1, 1) * inv_freq
    cos, sin = jnp.cos(ang), jnp.sin(ang)
    t = ang.shape[0]
    pad = DIFF_QK_DIM - ROPE_DIM
    cos_map = jnp.concatenate([cos, cos, jnp.ones((t, pad), F32)], axis=-1)
    lo_map = jnp.concatenate([-sin, jnp.zeros((t, half + pad), F32)], axis=-1)
    hi_map = jnp.concatenate([jnp.zeros((t, half), F32), sin, jnp.zeros((t, pad), F32)], axis=-1)
    reps = V7X_LANES // DIFF_QK_DIM
    return jnp.concatenate([jnp.tile(m, (1, reps)) for m in (cos_map, lo_map, hi_map)], axis=-1)


def _scan_masks():
    c, g, gc = SCAN_CHUNK, SCAN_GROUP_HEADS, SCAN_GROUP_CH
    row = lax.broadcasted_iota(jnp.int32, (g * c, gc), 0)
    col = lax.broadcasted_iota(jnp.int32, (g * c, gc), 1)
    same_block = (row // c) == (col // c)
    rowc = lax.broadcasted_iota(jnp.int32, (c, gc), 0)
    colc = lax.broadcasted_iota(jnp.int32, (c, gc), 1) % c
    strict_cat = rowc > colc
    incl_cat = rowc >= colc
    level0_cat = strict_cat & ((rowc // 2) == (colc // 2))
    eye_cat = jnp.where(rowc == colc, 1.0, 0.0).astype(F32)
    rt = lax.broadcasted_iota(jnp.int32, (c, c), 0)
    ct = lax.broadcasted_iota(jnp.int32, (c, c), 1)
    tril = jnp.where(rt >= ct, 1.0, 0.0).astype(BF16)
    return same_block, strict_cat, incl_cat, level0_cat, eye_cat, tril


def _scan_constants():
    idx = jnp.arange(SCAN_GROUP_HEADS * SCAN_CHUNK)
    t, s = idx[:, None], idx[None, :]
    mats = [t // 2 == s // 2]
    b = 2
    while b < SCAN_CHUNK:
        mats.append((t // (2 * b) == s // (2 * b)) & (t % (2 * b) >= b) & (s % (2 * b) < b))
        b *= 2
    mats.append(t // RWKV_HEAD_DIM == s // RWKV_HEAD_DIM)
    return jnp.stack(mats).astype(BF16)


def _head_sum(x, ones_bd, split=False):
    hi = x.astype(BF16)
    total = _dot(hi, ones_bd)
    if split:
        total = total + _dot((x - hi.astype(F32)).astype(BF16), ones_bd)
    return total


def _scan_chunk(ops, states, masks, const_ref):
    c, g, gc = SCAN_CHUNK, SCAN_GROUP_HEADS, SCAN_GROUP_CH
    same_block, strict_cat, incl_cat, level0_cat, eye_cat, tril = masks
    n_levels = const_ref.shape[0] - 1
    groups = range(len(ops))

    def tile(x):
        return jnp.concatenate([x] * g, axis=0)

    def bdexp(x):
        return jnp.where(same_block, tile(x), jnp.zeros_like(tile(x)))

    cums = []
    for lw, *_ in ops:
        lw_hi = lw.astype(BF16)
        lw_lo = (lw - lw_hi.astype(F32)).astype(BF16)
        cums.append(_dot(tril, lw_hi) + _dot(tril, lw_lo))

    a_b, b_b, k_b, r_b, end_b, v_b, w_end = [], [], [], [], [], [], []
    for (lw, r, k, v, kk, a), cum in zip(ops, cums):
        w_inc = jnp.exp(cum)
        w_inv = jnp.exp(-cum)
        w_prev = jnp.exp(cum - lw)
        w_last = w_inc[c - 1:c, :]
        b_f = kk * a * w_inv
        k_f = k * w_inv
        a_b.append((-(kk * w_prev)).astype(BF16))
        b_b.append(b_f.astype(BF16))
        k_b.append(k_f.astype(BF16))
        r_b.append((r * w_inc).astype(BF16))
        end_b.append(jnp.concatenate([(b_f * w_last).astype(BF16), (k_f * w_last).astype(BF16)], axis=0))
        v_b.append(v.astype(BF16))
        w_end.append(w_last)

    ar_b = [jnp.concatenate([a_b[i], r_b[i]], axis=0) for i in groups]
    l_cat = [_dot_nt(ar_b[i], jnp.concatenate([bdexp(b_b[i]), bdexp(k_b[i])], axis=0))
             for i in groups]
    l_ab = [jnp.where(strict_cat, lc[:c, :gc], 0.0).astype(BF16) for lc in l_cat]

    l_bd = [bdexp(l) for l in l_ab]
    t_cat = [eye_cat + jnp.where(level0_cat, l, jnp.zeros_like(l)).astype(F32) for l in l_ab]
    for lvl in range(1, n_levels):
        t_b = [t.astype(BF16) for t in t_cat]
        half = [_dot(t_b[i], l_bd[i] * const_ref[lvl]).astype(BF16) for i in groups]
        t_cat = [t_cat[i] + _dot(half[i], bdexp(t_b[i])) for i in groups]
    t_cat = [t.astype(BF16) for t in t_cat]

    l_rb = [jnp.where(incl_cat, lc[c:, :gc], 0.0).astype(BF16) for lc in l_cat]
    l_k = [jnp.concatenate([jnp.where(strict_cat, lc[:c, gc:], 0.0).astype(BF16),
                            jnp.where(incl_cat, lc[c:, gc:], 0.0).astype(BF16)], axis=0)
           for lc in l_cat]
    from_v = [_dot(l_k[i], bdexp(v_b[i])) for i in groups]

    from_state = [_dot_nt(ar_b[i], states[i].astype(BF16)) for i in groups]
    x_b = [(from_state[i][:c] + from_v[i][:c]).astype(BF16) for i in groups]
    u_b = [_dot(t_cat[i], bdexp(x_b[i])).astype(BF16) for i in groups]
    upd = [_dot_tn(jnp.concatenate([u_b[i], v_b[i]], axis=0), end_b[i]) for i in groups]
    new_states = [jnp.where(same_block, states[i] * w_end[i] + upd[i], 0.0) for i in groups]
    outs = [from_state[i][c:] + from_v[i][c:] + _dot(l_rb[i], bdexp(u_b[i])) for i in groups]
    return outs, new_states


_VEC_W0, _VEC_A0, _VEC_KK, _VEC_KA, _VEC_RK, _VEC_GNW, _VEC_GNB, _VEC_V0 = range(8)


def _rwkv_kernel(*refs, n_chunks, n_groups, has_vfirst):
    if has_vfirst:
        (const_ref, vec_ref, w2_ref, a2_ref, g2_ref, v2_ref, r_ref, k_ref, v_ref, lora_ref, vf_ref,
         o_ref, state_ref, lw_s, a_s, kk_s, k2_s, v_s, o_s) = refs
    else:
        (const_ref, vec_ref, w2_ref, a2_ref, g2_ref, r_ref, k_ref, v_ref, lora_ref,
         o_ref, state_ref, lw_s, a_s, kk_s, k2_s, v_s, o_s) = refs
        v2_ref = vf_ref = None

    @pl.when(pl.program_id(1) == 0)
    def _():
        state_ref[...] = jnp.zeros_like(state_ref)

    c, gc = SCAN_CHUNK, SCAN_GROUP_CH
    n_seqs = r_ref.shape[0]
    ones_bd = const_ref[const_ref.shape[0] - 1]
    vec = lambda row, sl: vec_ref[row:row + 1, sl]

    o0 = LORA_W
    o1 = o0 + LORA_A
    o2 = o1 + LORA_G
    o3 = o2 + LORA_MV

    for bi in range(n_seqs):
        tanh_wl = jnp.tanh(lora_ref[bi, :, 0:o0]).astype(BF16)
        al = lora_ref[bi, :, o0:o1].astype(BF16)
        for gi in range(n_groups):
            sl = slice(gi * gc, (gi + 1) * gc)
            zw = vec(_VEC_W0, sl) + _dot(tanh_wl, w2_ref[:, sl])
            lw_s[bi, :, sl] = -math.exp(-0.5) * jax.nn.sigmoid(zw)
            a = jax.nn.sigmoid(vec(_VEC_A0, sl) + _dot(al, a2_ref[:, sl]))
            a_s[bi, :, sl] = a
            v = v_ref[bi, :, sl]
            if has_vfirst:
                mix = jax.nn.sigmoid(vec(_VEC_V0, sl)
                                     + _dot(lora_ref[bi, :, o2:o3].astype(BF16), v2_ref[:, sl]))
                v = v + (vf_ref[bi, :, sl] - v) * mix
            v_s[bi, :, sl] = v
            k = k_ref[bi, :, sl]
            kk = k * vec(_VEC_KK, sl)
            ss = _head_sum(kk * kk, ones_bd)
            kk_s[bi, :, sl] = kk * lax.rsqrt(jnp.maximum(ss, 1e-24))
            k2_s[bi, :, sl] = k * (1.0 + (a - 1.0) * vec(_VEC_KA, sl))

    masks = _scan_masks()
    problems = [(bi, gi) for bi in range(n_seqs) for gi in range(n_groups)]

    def body(ci, carry):
        t0 = pl.multiple_of(ci * c, c)
        sls = [(bi, pl.ds(t0, c), slice(gi * gc, (gi + 1) * gc)) for bi, gi in problems]
        ops = [(lw_s[sl], r_ref[sl], k2_s[sl], v_s[sl], kk_s[sl], a_s[sl]) for sl in sls]
        outs, new_states = _scan_chunk(ops, [state_ref[pi] for pi in range(len(problems))],
                                       masks, const_ref)
        for pi in range(len(problems)):
            o_s[sls[pi]] = outs[pi]
            state_ref[pi] = new_states[pi]
        return carry

    lax.fori_loop(0, n_chunks, body, 0)

    inv_n = 1.0 / RWKV_HEAD_DIM
    for bi in range(n_seqs):
        sig_gl = jax.nn.sigmoid(lora_ref[bi, :, o1:o2]).astype(BF16)
        for gi in range(n_groups):
            sl = slice(gi * gc, (gi + 1) * gc)
            o = o_s[bi, :, sl]
            dev = o - _head_sum(o, ones_bd) * inv_n
            var = _head_sum(dev * dev, ones_bd) * inv_n
            normed = dev * lax.rsqrt(var + GN_EPS) * vec(_VEC_GNW, sl) + vec(_VEC_GNB, sl)
            bonus = _head_sum(r_ref[bi, :, sl] * k2_s[bi, :, sl] * vec(_VEC_RK, sl), ones_bd,
                              split=True) * v_s[bi, :, sl]
            gate = _dot(sig_gl, g2_ref[:, sl])
            o_ref[bi, :, sl] = ((normed + bonus) * gate).astype(o_ref.dtype)


def rwkv7_mixer(pr, pr_first, vecs, w2p, a2p, g2p, v2p, batch, seq, width):
    t, n = pr.shape
    tb = _pick(seq, (256, 128, 64))
    nb = seq // tb
    n_seqs = 2 if batch % 2 == 0 else 1
    n_groups = width // SCAN_GROUP_CH
    has_vfirst = pr_first is not None
    consts = _scan_constants()
    lora_col = 3 * width // LORA_BLOCK
    row_spec = lambda col: pl.BlockSpec((n_seqs, tb, width), lambda b, s: (b, s, col))
    full = lambda arr: pl.BlockSpec(arr.shape, lambda b, s: (0,) * arr.ndim)
    weights = [w2p, a2p, g2p] + ([v2p] if has_vfirst else [])
    in_specs = ([full(consts), full(vecs)] + [full(w) for w in weights]
                + [row_spec(0), row_spec(1), row_spec(2),
                   pl.BlockSpec((n_seqs, tb, LORA_BLOCK), lambda b, s: (b, s, lora_col))]
                + ([row_spec(2)] if has_vfirst else []))
    pr3 = pr.reshape(batch, seq, n)
    args = ([consts, vecs] + weights + [pr3, pr3, pr3, pr3]
            + ([pr_first.reshape(batch, seq, n)] if has_vfirst else []))
    out = pl.pallas_call(
        functools.partial(_rwkv_kernel, n_chunks=tb // SCAN_CHUNK, n_groups=n_groups,
                          has_vfirst=has_vfirst),
        grid=(batch // n_seqs, nb),
        in_specs=in_specs,
        out_specs=pl.BlockSpec((n_seqs, tb, width), lambda b, s: (b, s, 0)),
        out_shape=jax.ShapeDtypeStruct((batch, seq, width), BF16),
        scratch_shapes=([pltpu.VMEM((n_seqs * n_groups, SCAN_GROUP_CH, SCAN_GROUP_CH), F32)]
                        + [pltpu.VMEM((n_seqs, tb, width), F32)] * 6),
        compiler_params=_params(("parallel", "arbitrary")),
        name="rwkv7_mixer",
    )(*args)
    return out.reshape(t, width)


def _diff_attn_kernel(lam_ref, q_ref, k_ref, v_ref, w_ref, o_ref,
                      vt_ref, m_ref, acc_ref, s0_ref, s1_ref, *, tq, scale_out):
    qi = pl.program_id(2)
    n_heads = q_ref.shape[1] // DIFF_V_DIM

    @pl.when(qi == 0)
    def _():
        for hh in range(n_heads):
            r0 = hh * ATTN_VT_ROWS
            for c0 in range(0, v_ref.shape[0], tq):
                vt_ref[r0:r0 + DIFF_V_DIM, c0:c0 + tq] = (
                    v_ref[c0:c0 + tq, hh * DIFF_V_DIM:(hh + 1) * DIFF_V_DIM].astype(F32).T.astype(BF16))
            vt_ref[r0 + DIFF_V_DIM:r0 + ATTN_VT_ROWS, :] = jnp.ones(
                (ATTN_VT_ROWS - DIFF_V_DIM, vt_ref.shape[1]), BF16)

    lane = lax.broadcasted_iota(jnp.int32, (tq, DIFF_V_DIM), 1)
    q_streams = []
    for hh in range(n_heads):
        q = q_ref[:, hh * DIFF_V_DIM:(hh + 1) * DIFF_V_DIM]
        zero = jnp.zeros_like(q)
        q_streams.append((hh, jnp.where(lane < DIFF_QK_DIM, q, zero)))
        q_streams.append((hh, jnp.where(lane >= DIFF_QK_DIM, q, zero)))
    n_streams = len(q_streams)

    m_ref[...] = jnp.full_like(m_ref, -jnp.inf)
    acc_ref[...] = jnp.zeros_like(acc_ref)

    tk = tq // 2
    s_bufs = (s0_ref, s1_ref)

    def scores(j, slot):
        k0 = pl.multiple_of(j * tk, tk)
        for si, (hh, qm) in enumerate(q_streams):
            kb = k_ref[pl.ds(k0, tk), hh * DIFF_V_DIM:(hh + 1) * DIFF_V_DIM]
            s_bufs[slot][si] = _dot_nt(kb, qm)

    def softmax_pv(j, slot, diag_offset=None):
        k0 = pl.multiple_of(j * tk, tk)
        for si, (hh, _) in enumerate(q_streams):
            s = s_bufs[slot][si]
            if diag_offset is not None:
                kpos = lax.broadcasted_iota(jnp.int32, s.shape, 0) + diag_offset
                qpos = lax.broadcasted_iota(jnp.int32, s.shape, 1)
                s = jnp.where(kpos <= qpos, s, -jnp.inf)
            m_old = m_ref[si]
            m_new = jnp.maximum(m_old, jnp.max(s, axis=0, keepdims=True))
            alpha = jnp.exp2(m_old - m_new)
            p = jnp.exp2(s - m_new)
            m_ref[si] = m_new
            vtb = vt_ref[hh * ATTN_VT_ROWS:(hh + 1) * ATTN_VT_ROWS, pl.ds(k0, tk)]
            acc_ref[si] = acc_ref[si] * alpha + _dot(vtb, p.astype(BF16))

    scores(0, 0)

    def body(jj, carry):
        scores(2 * jj + 1, 1)
        softmax_pv(2 * jj, 0)
        scores(2 * jj + 2, 0)
        softmax_pv(2 * jj + 1, 1)
        return carry

    lax.fori_loop(0, qi, body, 0)
    scores(2 * qi + 1, 1)
    softmax_pv(2 * qi, 0, diag_offset=0)
    softmax_pv(2 * qi + 1, 1, diag_offset=tk)

    lam = lam_ref[0]
    for hh in range(n_heads):
        a1, a2 = acc_ref[2 * hh], acc_ref[2 * hh + 1]
        nv = DIFF_V_DIM
        o_t = a1[:nv] / a1[nv:nv + 1] - lam * (a2[:nv] / a2[nv:nv + 1])
        o = o_t.T
        ms = jnp.mean(o * o, axis=-1, keepdims=True)
        o_ref[:, hh * DIFF_V_DIM:(hh + 1) * DIFF_V_DIM] = (
            o * lax.rsqrt(ms + SUBLN_EPS) * w_ref[...] * scale_out).astype(o_ref.dtype)


def diff_attention(qkv, lam, subln_w, lambda_init, batch, seq, heads):
    t = qkv.shape[0]
    tq = _pick(seq, (512, 256, 128))
    nq = seq // tq
    hb = 2 if heads % 2 == 0 else 1
    wb = hb * DIFF_V_DIM
    n_hb = heads // hb
    return pl.pallas_call(
        functools.partial(_diff_attn_kernel, tq=tq, scale_out=1.0 - lambda_init),
        grid=(batch, n_hb, nq),
        in_specs=[
            pl.BlockSpec(memory_space=pltpu.SMEM),
            pl.BlockSpec((tq, wb), lambda b, h, i: (b * nq + i, h)),
            pl.BlockSpec((seq, wb), lambda b, h, i: (b, n_hb + h)),
            pl.BlockSpec((seq, wb), lambda b, h, i: (b, 2 * n_hb + h)),
            pl.BlockSpec((1, DIFF_V_DIM), lambda b, h, i: (0, 0)),
        ],
        out_specs=pl.BlockSpec((tq, wb), lambda b, h, i: (b * nq + i, h)),
        out_shape=jax.ShapeDtypeStruct((t, heads * DIFF_V_DIM), BF16),
        scratch_shapes=[
            pltpu.VMEM((hb * ATTN_VT_ROWS, seq), BF16),
            pltpu.VMEM((2 * hb, 1, tq), F32),
            pltpu.VMEM((2 * hb, ATTN_VT_ROWS, tq), F32),
            pltpu.VMEM((2 * hb, tq // 2, tq), F32),
            pltpu.VMEM((2 * hb, tq // 2, tq), F32),
        ],
        compiler_params=_params(("parallel", "parallel", "arbitrary")),
        name="diff_attention",
    )(lam.reshape(1).astype(F32), qkv, qkv, qkv, subln_w.reshape(1, DIFF_V_DIM).astype(F32))


def _out_proj_kernel(a1_ref, a2_ref, w_ref, g_ref, res_ref, o_ref):
    k1 = a1_ref.shape[1]
    m = _dot(a1_ref[...], w_ref[0:k1, :]) + _dot(a2_ref[...], w_ref[k1:, :])
    ms = jnp.mean(m * m, axis=-1, keepdims=True)
    o_ref[...] = res_ref[...] + m * lax.rsqrt(ms + NORM_EPS) * g_ref[...]


def out_proj(a1, a2, w, g, res):
    t, k1 = a1.shape
    k2 = a2.shape[1]
    d = w.shape[1]
    tm = _pick(t, (512, 256, 128))
    return pl.pallas_call(
        _out_proj_kernel,
        grid=(t // tm,),
        in_specs=[
            pl.BlockSpec((tm, k1), lambda i: (i, 0)),
            pl.BlockSpec((tm, k2), lambda i: (i, 0)),
            pl.BlockSpec((k1 + k2, d), lambda i: (0, 0)),
            pl.BlockSpec((1, d), lambda i: (0, 0)),
            pl.BlockSpec((tm, d), lambda i: (i, 0)),
        ],
        out_specs=pl.BlockSpec((tm, d), lambda i: (i, 0)),
        out_shape=jax.ShapeDtypeStruct((t, d), F32),
        compiler_params=_params(("parallel",)),
        name="out_proj",
    )(a1, a2, w, g.reshape(1, d).astype(F32), res)


def _ffn_up_kernel(x_ref, g_ref, wg_ref, wu_ref, cw_ref, o_ref, xn_ref, carry_ref, *, tiles_per_seq):
    i, j = pl.program_id(0), pl.program_id(1)

    @pl.when(j == 0)
    def _():
        xn_ref[...] = _rms_normed(x_ref, g_ref)

    @pl.when(i % tiles_per_seq == 0)
    def _():
        carry_ref[j] = jnp.zeros(carry_ref.shape[1:], F32)

    xn = xn_ref[...]
    gate = _dot(xn, wg_ref[...])
    up = _dot(xn, wu_ref[...])
    carry = carry_ref[j]
    conv = (_rows_from_prev(gate, 2, carry) * cw_ref[0:1, :]
            + _rows_from_prev(gate, 1, carry) * cw_ref[1:2, :]
            + gate * cw_ref[2:3, :] + cw_ref[3:4, :])
    carry_ref[j] = gate[gate.shape[0] - V7X_SUBLANES:, :]
    inner = math.sqrt(2.0 / math.pi) * (conv + 0.044715 * (conv * conv * conv))
    o_ref[...] = (0.5 * conv * (1.0 + jnp.tanh(inner)) * up).astype(o_ref.dtype)


def ffn_up(x, g, w_up, conv_wb, seq):
    t, d = x.shape
    f = w_up.shape[1] // 2
    tm = _pick(seq, (1024, 512, 256, 128))
    tn = _pick(f, (512, 256, 128))
    nj = f // tn
    return pl.pallas_call(
        functools.partial(_ffn_up_kernel, tiles_per_seq=seq // tm),
        grid=(t // tm, nj),
        in_specs=[
            pl.BlockSpec((tm, d), lambda i, j: (i, 0)),
            pl.BlockSpec((1, d), lambda i, j: (0, 0)),
            pl.BlockSpec((d, tn), lambda i, j: (0, j)),
            pl.BlockSpec((d, tn), lambda i, j: (0, j + nj)),
            pl.BlockSpec((CONV_WIDTH + 1, tn), lambda i, j: (0, j)),
        ],
        out_specs=pl.BlockSpec((tm, tn), lambda i, j: (i, j)),
        out_shape=jax.ShapeDtypeStruct((t, f), BF16),
        scratch_shapes=[pltpu.VMEM((tm, d), BF16), pltpu.VMEM((nj, V7X_SUBLANES, tn), F32)],
        compiler_params=_params(("arbitrary", "arbitrary")),
        name="ffn_up",
    )(x, g.reshape(1, d).astype(F32), w_up, w_up, conv_wb)


def _ffn_down_kernel(a_ref, w_ref, g_ref, res_ref, o_ref, *, nk):
    k = pl.program_id(1)

    @pl.when(k == 0)
    def _():
        o_ref[...] = jnp.zeros_like(o_ref)

    o_ref[...] += _dot(a_ref[...], w_ref[...])

    @pl.when(k == nk - 1)
    def _():
        m = o_ref[...]
        ms = jnp.mean(m * m, axis=-1, keepdims=True)
        o_ref[...] = res_ref[...] + m * lax.rsqrt(ms + NORM_EPS) * g_ref[...]


def ffn_down(a, w, g, res):
    t, kdim = a.shape
    d = w.shape[1]
    tm = _pick(t, (1024, 512, 256, 128))
    tk = _pick(kdim, (512, 256, 128))
    nk = kdim // tk
    return pl.pallas_call(
        functools.partial(_ffn_down_kernel, nk=nk),
        grid=(t // tm, nk),
        in_specs=[
            pl.BlockSpec((tm, tk), lambda i, k: (i, k)),
            pl.BlockSpec((tk, d), lambda i, k: (k, 0)),
            pl.BlockSpec((1, d), lambda i, k: (0, 0)),
            pl.BlockSpec((tm, d), lambda i, k: (i, 0)),
        ],
        out_specs=pl.BlockSpec((tm, d), lambda i, k: (i, 0)),
        out_shape=jax.ShapeDtypeStruct((t, d), F32),
        compiler_params=_params(("parallel", "arbitrary")),
        name="ffn_down",
    )(a, w, g.reshape(1, d).astype(F32), res)


def _pad_cols(w, width):
    return jnp.pad(w, [(0, 0)] * (w.ndim - 1) + [(0, width - w.shape[-1])])


def _pad_rows(w, rows):
    return jnp.pad(w, ((0, rows - w.shape[0]), (0, 0)))


def kernel(x, positions, pre_mix_norm, post_mix_norm, pre_ffn_norm, post_ffn_norm, w_in, w_mv_down, shift_mu, shift_mu_mv, w0, w2, a0, a2, g2, k_k, k_a, r_k, gn_w, gn_b, v0, v2, lam_q1, lam_k1, lam_q2, lam_k2, subln_w, w_out, w_up, conv_w, conv_b, w_down):
    batch, seq, d_model = x.shape
    t = batch * seq
    depth = w_in.shape[0]
    width = w0.shape[1]
    diff_heads = (d_model - width) // DIFF_V_DIM
    rwkv_cols = shift_mu.shape[1]
    qk_cols = diff_heads * 2 * DIFF_QK_DIM
    rwkv_n = 3 * width + LORA_BLOCK
    assert rwkv_n % IN_RWKV_TN == 0 and (3 * width) % LORA_BLOCK == 0

    rope = _rope_tables(positions)
    xf = x.reshape(t, d_model)
    pr_first = None
    for l in range(depth):
        o_r, o_wl = 0, width
        o_k = o_wl + DECAY_LORA
        o_v = o_k + width
        o_al = o_v + width
        o_gl = o_al + AAA_LORA
        wl_cols = lambda m: _pad_cols(m[..., o_wl:o_k], LORA_W)
        al_cols = lambda m: _pad_cols(m[..., o_al:o_gl], LORA_A)
        gl_cols = lambda m: _pad_cols(m[..., o_gl:rwkv_cols], LORA_G)
        if l == 0:
            mv_w = jnp.zeros((d_model, LORA_MV), F32)
            mv_mu = jnp.zeros((LORA_MV,), F32)
        else:
            mv_w = _pad_cols(w_mv_down[l - 1], LORA_MV)
            mv_mu = _pad_cols(shift_mu_mv[l - 1], LORA_MV)
        arrange = lambda m, mv: jnp.concatenate(
            [m[..., o_r:o_wl], m[..., o_k:o_v], m[..., o_v:o_al], wl_cols(m), al_cols(m), gl_cols(m), mv],
            axis=-1)
        w_rwkv = _pad_cols(arrange(w_in[l], mv_w), rwkv_n).astype(BF16)
        mu_rwkv = _pad_cols(arrange(shift_mu[l], mv_mu), rwkv_n)
        pr = in_proj_rwkv(xf, pre_mix_norm[l], w_rwkv, mu_rwkv, seq)

        vecs = jnp.stack([w0[l], a0[l], k_k[l], k_a[l], r_k[l].reshape(width), gn_w[l], gn_b[l],
                          v0[l - 1] if l > 0 else jnp.zeros((width,), F32)]).astype(F32)
        w2p = _pad_rows(w2[l], LORA_W).astype(BF16)
        a2p = _pad_rows(a2[l], LORA_A).astype(BF16)
        g2p = _pad_rows(g2[l], LORA_G).astype(BF16)
        v2p = _pad_rows(v2[l - 1], LORA_MV).astype(BF16) if l > 0 else None
        o_rwkv = rwkv7_mixer(pr, pr_first, vecs, w2p, a2p, g2p, v2p, batch, seq, width)
        if l == 0:
            pr_first = pr

        qkv = in_proj_attn(xf, pre_mix_norm[l], w_in[l][:, rwkv_cols:].astype(BF16), rope, seq, qk_cols)
        lambda_init = 0.8 - 0.6 * math.exp(-0.3 * l)
        lam = (jnp.exp(jnp.sum(lam_q1[l] * lam_k1[l])) - jnp.exp(jnp.sum(lam_q2[l] * lam_k2[l]))
               + lambda_init)
        o_diff = diff_attention(qkv, lam, subln_w[l], lambda_init, batch, seq, diff_heads)

        xf = out_proj(o_rwkv, o_diff, w_out[l].astype(BF16), post_mix_norm[l], xf)

        conv_wb = jnp.concatenate([conv_w[l], conv_b[l][None, :]], axis=0).astype(F32)
        act = ffn_up(xf, pre_ffn_norm[l], w_up[l].astype(BF16), conv_wb, seq)
        xf = ffn_down(act, w_down[l].astype(BF16), post_ffn_norm[l], xf)
    return xf.reshape(batch, seq, d_model)
```

```python
import functools
import math

import jax
import jax.numpy as jnp
from jax import lax
from jax.experimental import pallas as pl
from jax.experimental.pallas import tpu as pltpu

F32 = jnp.float32
BF16 = jnp.bfloat16

RWKV_HEAD_DIM = 64
DIFF_QK_DIM = 64
DIFF_V_DIM = 128
DECAY_LORA = 64
AAA_LORA = 64
MV_LORA = 32
GATE_LORA = 160
CONV_WIDTH = 3
ROPE_THETA = 500000.0
ROPE_DIM = DIFF_QK_DIM // 4
NORM_EPS = 1e-6
GN_EPS = 64e-5
SUBLN_EPS = 1e-5

V7X_LANES = 128
V7X_SUBLANES = 8
V7X_MXU_DIM = 256
V7X_VMEM_LIMIT_BYTES = 56 * 1024 * 1024

SCAN_CHUNK = 64
SCAN_GROUP_HEADS = V7X_MXU_DIM // RWKV_HEAD_DIM
SCAN_GROUP_CH = SCAN_GROUP_HEADS * RWKV_HEAD_DIM
assert SCAN_CHUNK == RWKV_HEAD_DIM

LORA_W, LORA_A, LORA_G, LORA_MV = 128, 128, 256, 128
LORA_BLOCK = 768
IN_RWKV_TN = 768
IN_ATTN_TN = 1024
ATTN_VT_ROWS = DIFF_V_DIM + 16


def _pick(n, candidates):
    for c in candidates:
        if n % c == 0:
            return c
    return n


def _params(semantics):
    return pltpu.CompilerParams(dimension_semantics=semantics,
                                vmem_limit_bytes=V7X_VMEM_LIMIT_BYTES)


def _dot(a, b):
    return jnp.dot(a, b, preferred_element_type=F32)


def _dot_nt(a, b):
    return lax.dot_general(a, b, (((1,), (1,)), ((), ())), preferred_element_type=F32)


def _dot_tn(a, b):
    return lax.dot_general(a, b, (((0,), (0,)), ((), ())), preferred_element_type=F32)


def _rms_normed(x_ref, g_ref):
    x = x_ref[...]
    ms = jnp.mean(x * x, axis=-1, keepdims=True)
    return (x * lax.rsqrt(ms + NORM_EPS) * g_ref[...]).astype(BF16)


def _rows_from_prev(x, n, carry):
    rolled = pltpu.roll(x, n, 0)
    row = lax.broadcasted_iota(jnp.int32, x.shape, 0)
    for r in range(n):
        src = V7X_SUBLANES - n + r
        rolled = jnp.where(row == r, carry[src:src + 1, :], rolled)
    return rolled


def _rms_norm_kernel(x_ref, g_ref, o_ref):
    o_ref[...] = _rms_normed(x_ref, g_ref)


def rms_norm_bf16(x, g):
    t, d = x.shape
    tm = _pick(t, (1024, 512, 256, 128))
    return pl.pallas_call(
        _rms_norm_kernel,
        grid=(t // tm,),
        in_specs=[pl.BlockSpec((tm, d), lambda i: (i, 0)), pl.BlockSpec((1, d), lambda i: (0, 0))],
        out_specs=pl.BlockSpec((tm, d), lambda i: (i, 0)),
        out_shape=jax.ShapeDtypeStruct((t, d), BF16),
        compiler_params=_params(("parallel",)),
        name="rms_norm",
    )(x, g.reshape(1, d).astype(F32))


def _in_rwkv_kernel(h_ref, w_ref, mu_ref, o_ref, carry_ref, *, tiles_per_seq):
    i, j = pl.program_id(0), pl.program_id(1)

    @pl.when(i % tiles_per_seq == 0)
    def _():
        carry_ref[j] = jnp.zeros(carry_ref.shape[1:], F32)

    h = h_ref[...]
    tm = h.shape[0]
    for c0 in range(0, o_ref.shape[1], V7X_MXU_DIM):
        cs = slice(c0, c0 + V7X_MXU_DIM)
        p = _dot(h, w_ref[:, cs])
        prev = _rows_from_prev(p, 1, carry_ref[j, :, cs])
        o_ref[:, cs] = p + mu_ref[:, cs] * (prev - p)
        carry_ref[j, :, cs] = p[tm - V7X_SUBLANES:, :]


def in_proj_rwkv(h, w, mu, seq):
    t, d = h.shape
    n = w.shape[1]
    tm = _pick(seq, (1024, 512, 256, 128))
    tn = IN_RWKV_TN
    nj = n // tn
    return pl.pallas_call(
        functools.partial(_in_rwkv_kernel, tiles_per_seq=seq // tm),
        grid=(t // tm, nj),
        in_specs=[
            pl.BlockSpec((tm, d), lambda i, j: (i, 0)),
            pl.BlockSpec((d, tn), lambda i, j: (0, j)),
            pl.BlockSpec((1, tn), lambda i, j: (0, j)),
        ],
        out_specs=pl.BlockSpec((tm, tn), lambda i, j: (i, j)),
        out_shape=jax.ShapeDtypeStruct((t, n), F32),
        scratch_shapes=[pltpu.VMEM((nj, V7X_SUBLANES, tn), F32)],
        compiler_params=_params(("arbitrary", "arbitrary")),
        name="in_proj_rwkv",
    )(h, w, mu.reshape(1, n).astype(F32))


def _in_attn_kernel(h_ref, w_ref, rope_ref, o_ref, *, n_q_tiles, n_rot_tiles):
    j = pl.program_id(1)
    h = h_ref[...]
    chunks = [slice(c0, c0 + V7X_MXU_DIM) for c0 in range(0, o_ref.shape[1], V7X_MXU_DIM)]

    @pl.when(j < n_rot_tiles)
    def _():
        scale = jnp.where(j < n_q_tiles, DIFF_QK_DIM ** -0.5 * math.log2(math.e), 1.0).astype(F32)
        cos = rope_ref[:, 0:V7X_LANES] * scale
        sin_lo = rope_ref[:, V7X_LANES:2 * V7X_LANES] * scale
        sin_hi = rope_ref[:, 2 * V7X_LANES:] * scale
        half = ROPE_DIM // 2
        for cs in chunks:
            p = _dot(h, w_ref[:, cs])
            for g0 in range(0, V7X_MXU_DIM, V7X_LANES):
                xg = p[:, g0:g0 + V7X_LANES]
                rot = (xg * cos + pltpu.roll(xg, V7X_LANES - half, 1) * sin_lo
                       + pltpu.roll(xg, half, 1) * sin_hi)
                o_ref[:, cs.start + g0:cs.start + g0 + V7X_LANES] = rot.astype(o_ref.dtype)

    @pl.when(j >= n_rot_tiles)
    def _():
        for cs in chunks:
            o_ref[:, cs] = _dot(h, w_ref[:, cs]).astype(o_ref.dtype)


def in_proj_attn(h, w, rope, seq, qk_cols):
    t, d = h.shape
    n = w.shape[1]
    tm = _pick(seq, (1024, 512, 256, 128))
    tn = IN_ATTN_TN
    assert qk_cols % tn == 0 and n % tn == 0
    return pl.pallas_call(
        functools.partial(_in_attn_kernel, n_q_tiles=qk_cols // tn, n_rot_tiles=2 * qk_cols // tn),
        grid=(t // tm, n // tn),
        in_specs=[
            pl.BlockSpec((tm, d), lambda i, j: (i, 0)),
            pl.BlockSpec((d, tn), lambda i, j: (0, j)),
            pl.BlockSpec((tm, 3 * V7X_LANES), lambda i, j: (i, 0)),
        ],
        out_specs=pl.BlockSpec((tm, tn), lambda i, j: (i, j)),
        out_shape=jax.ShapeDtypeStruct((t, n), BF16),
        compiler_params=_params(("parallel", "arbitrary")),
        name="in_proj_attn",
    )(h, w, rope)


def _rope_tables(positions):
    half = ROPE_DIM // 2
    inv_freq = ROPE_THETA ** (-jnp.arange(0, ROPE_DIM, 2, dtype=F32) / ROPE_DIM)
    ang = positions.astype(F32).reshape(-1, 1) * inv_freq
    cos, sin = jnp.cos(ang), jnp.sin(ang)
    t = ang.shape[0]
    pad = DIFF_QK_DIM - ROPE_DIM
    cos_map = jnp.concatenate([cos, cos, jnp.ones((t, pad), F32)], axis=-1)
    lo_map = jnp.concatenate([-sin, jnp.zeros((t, half + pad), F32)], axis=-1)
    hi_map = jnp.concatenate([jnp.zeros((t, half), F32), sin, jnp.zeros((t, pad), F32)], axis=-1)
    reps = V7X_LANES // DIFF_QK_DIM
    return jnp.concatenate([jnp.tile(m, (1, reps)) for m in (cos_map, lo_map, hi_map)], axis=-1)


def _scan_masks():
    c, g, gc = SCAN_CHUNK, SCAN_GROUP_HEADS, SCAN_GROUP_CH
    row = lax.broadcasted_iota(jnp.int32, (g * c, gc), 0)
    col = lax.broadcasted_iota(jnp.int32, (g * c, gc), 1)
    same_block = (row // c) == (col // c)
    rowc = lax.broadcasted_iota(jnp.int32, (c, gc), 0)
    colc = lax.broadcasted_iota(jnp.int32, (c, gc), 1) % c
    strict_cat = rowc > colc
    incl_cat = rowc >= colc
    level0_cat = strict_cat & ((rowc // 2) == (colc // 2))
    eye_cat = jnp.where(rowc == colc, 1.0, 0.0).astype(F32)
    rt = lax.broadcasted_iota(jnp.int32, (c, c), 0)
    ct = lax.broadcasted_iota(jnp.int32, (c, c), 1)
    tril = jnp.where(rt >= ct, 1.0, 0.0).astype(BF16)
    return same_block, strict_cat, incl_cat, level0_cat, eye_cat, tril


def _scan_constants():
    idx = jnp.arange(SCAN_GROUP_HEADS * SCAN_CHUNK)
    t, s = idx[:, None], idx[None, :]
    mats = [t // 2 == s // 2]
    b = 2
    while b < SCAN_CHUNK:
        mats.append((t // (2 * b) == s // (2 * b)) & (t % (2 * b) >= b) & (s % (2 * b) < b))
        b *= 2
    mats.append(t // RWKV_HEAD_DIM == s // RWKV_HEAD_DIM)
    return jnp.stack(mats).astype(BF16)


def _head_sum(x, ones_bd, split=False):
    hi = x.astype(BF16)
    total = _dot(hi, ones_bd)
    if split:
        total = total + _dot((x - hi.astype(F32)).astype(BF16), ones_bd)
    return total


def _scan_chunk(ops, states, masks, const_ref):
    c, g, gc = SCAN_CHUNK, SCAN_GROUP_HEADS, SCAN_GROUP_CH
    same_block, strict_cat, incl_cat, level0_cat, eye_cat, tril = masks
    n_levels = const_ref.shape[0] - 1
    groups = range(len(ops))

    def tile(x):
        return jnp.concatenate([x] * g, axis=0)

    def bdexp(x):
        return jnp.where(same_block, tile(x), jnp.zeros_like(tile(x)))

    cums = []
    for lw, *_ in ops:
        lw_hi = lw.astype(BF16)
        lw_lo = (lw - lw_hi.astype(F32)).astype(BF16)
        cums.append(_dot(tril, lw_hi) + _dot(tril, lw_lo))

    a_b, b_b, k_b, r_b, end_b, v_b, w_end = [], [], [], [], [], [], []
    for (lw, r, k, v, kk, a), cum in zip(ops, cums):
        w_inc = jnp.exp(cum)
        w_inv = jnp.exp(-cum)
        w_prev = jnp.exp(cum - lw)
        w_last = w_inc[c - 1:c, :]
        b_f = kk * a * w_inv
        k_f = k * w_inv
        a_b.append((-(kk * w_prev)).astype(BF16))
        b_b.append(b_f.astype(BF16))
        k_b.append(k_f.astype(BF16))
        r_b.append((r * w_inc).astype(BF16))
        end_b.append(jnp.concatenate([(b_f * w_last).astype(BF16), (k_f * w_last).astype(BF16)], axis=0))
        v_b.append(v.astype(BF16))
        w_end.append(w_last)

    ar_b = [jnp.concatenate([a_b[i], r_b[i]], axis=0) for i in groups]
    l_cat = [_dot_nt(ar_b[i], jnp.concatenate([bdexp(b_b[i]), bdexp(k_b[i])], axis=0))
             for i in groups]
    l_ab = [jnp.where(strict_cat, lc[:c, :gc], 0.0).astype(BF16) for lc in l_cat]

    l_bd = [bdexp(l) for l in l_ab]
    t_cat = [eye_cat + jnp.where(level0_cat, l, jnp.zeros_like(l)).astype(F32) for l in l_ab]
    for lvl in range(1, n_levels):
        t_b = [t.astype(BF16) for t in t_cat]
        half = [_dot(t_b[i], l_bd[i] * const_ref[lvl]).astype(BF16) for i in groups]
        t_cat = [t_cat[i] + _dot(half[i], bdexp(t_b[i])) for i in groups]
    t_cat = [t.astype(BF16) for t in t_cat]

    l_rb = [jnp.where(incl_cat, lc[c:, :gc], 0.0).astype(BF16) for lc in l_cat]
    l_k = [jnp.concatenate([jnp.where(strict_cat, lc[:c, gc:], 0.0).astype(BF16),
                            jnp.where(incl_cat, lc[c:, gc:], 0.0).astype(BF16)], axis=0)
           for lc in l_cat]
    from_v = [_dot(l_k[i], bdexp(v_b[i])) for i in groups]

    from_state = [_dot_nt(ar_b[i], states[i].astype(BF16)) for i in groups]
    x_b = [(from_state[i][:c] + from_v[i][:c]).astype(BF16) for i in groups]
    u_b = [_dot(t_cat[i], bdexp(x_b[i])).astype(BF16) for i in groups]
    upd = [_dot_tn(jnp.concatenate([u_b[i], v_b[i]], axis=0), end_b[i]) for i in groups]
    new_states = [jnp.where(same_block, states[i] * w_end[i] + upd[i], 0.0) for i in groups]
    outs = [from_state[i][c:] + from_v[i][c:] + _dot(l_rb[i], bdexp(u_b[i])) for i in groups]
    return outs, new_states


_VEC_W0, _VEC_A0, _VEC_KK, _VEC_KA, _VEC_RK, _VEC_GNW, _VEC_GNB, _VEC_V0 = range(8)


def _rwkv_kernel(*refs, n_chunks, n_groups, has_vfirst):
    if has_vfirst:
        (const_ref, vec_ref, w2_ref, a2_ref, g2_ref, v2_ref, r_ref, k_ref, v_ref, lora_ref, vf_ref,
         o_ref, state_ref, lw_s, a_s, kk_s, k2_s, v_s, o_s) = refs
    else:
        (const_ref, vec_ref, w2_ref, a2_ref, g2_ref, r_ref, k_ref, v_ref, lora_ref,
         o_ref, state_ref, lw_s, a_s, kk_s, k2_s, v_s, o_s) = refs
        v2_ref = vf_ref = None

    @pl.when(pl.program_id(1) == 0)
    def _():
        state_ref[...] = jnp.zeros_like(state_ref)

    c, gc = SCAN_CHUNK, SCAN_GROUP_CH
    n_seqs = r_ref.shape[0]
    ones_bd = const_ref[const_ref.shape[0] - 1]
    vec = lambda row, sl: vec_ref[row:row + 1, sl]

    o0 = LORA_W
    o1 = o0 + LORA_A
    o2 = o1 + LORA_G
    o3 = o2 + LORA_MV

    for bi in range(n_seqs):
        tanh_wl = jnp.tanh(lora_ref[bi, :, 0:o0]).astype(BF16)
        al = lora_ref[bi, :, o0:o1].astype(BF16)
        for gi in range(n_groups):
            sl = slice(gi * gc, (gi + 1) * gc)
            zw = vec(_VEC_W0, sl) + _dot(tanh_wl, w2_ref[:, sl])
            lw_s[bi, :, sl] = -math.exp(-0.5) * jax.nn.sigmoid(zw)
            a = jax.nn.sigmoid(vec(_VEC_A0, sl) + _dot(al, a2_ref[:, sl]))
            a_s[bi, :, sl] = a
            v = v_ref[bi, :, sl]
            if has_vfirst:
                mix = jax.nn.sigmoid(vec(_VEC_V0, sl)
                                     + _dot(lora_ref[bi, :, o2:o3].astype(BF16), v2_ref[:, sl]))
                v = v + (vf_ref[bi, :, sl] - v) * mix
            v_s[bi, :, sl] = v
            k = k_ref[bi, :, sl]
            kk = k * vec(_VEC_KK, sl)
            ss = _head_sum(kk * kk, ones_bd)
            kk_s[bi, :, sl] = kk * lax.rsqrt(jnp.maximum(ss, 1e-24))
            k2_s[bi, :, sl] = k * (1.0 + (a - 1.0) * vec(_VEC_KA, sl))

    masks = _scan_masks()
    problems = [(bi, gi) for bi in range(n_seqs) for gi in range(n_groups)]

    def body(ci, carry):
        t0 = pl.multiple_of(ci * c, c)
        sls = [(bi, pl.ds(t0, c), slice(gi * gc, (gi + 1) * gc)) for bi, gi in problems]
        ops = [(lw_s[sl], r_ref[sl], k2_s[sl], v_s[sl], kk_s[sl], a_s[sl]) for sl in sls]
        outs, new_states = _scan_chunk(ops, [state_ref[pi] for pi in range(len(problems))],
                                       masks, const_ref)
        for pi in range(len(problems)):
            o_s[sls[pi]] = outs[pi]
            state_ref[pi] = new_states[pi]
        return carry

    lax.fori_loop(0, n_chunks, body, 0)

    inv_n = 1.0 / RWKV_HEAD_DIM
    for bi in range(n_seqs):
        sig_gl = jax.nn.sigmoid(lora_ref[bi, :, o1:o2]).astype(BF16)
        for gi in range(n_groups):
            sl = slice(gi * gc, (gi + 1) * gc)
            o = o_s[bi, :, sl]
            dev = o - _head_sum(o, ones_bd) * inv_n
            var = _head_sum(dev * dev, ones_bd) * inv_n
            normed = dev * lax.rsqrt(var + GN_EPS) * vec(_VEC_GNW, sl) + vec(_VEC_GNB, sl)
            bonus = _head_sum(r_ref[bi, :, sl] * k2_s[bi, :, sl] * vec(_VEC_RK, sl), ones_bd,
                              split=True) * v_s[bi, :, sl]
            gate = _dot(sig_gl, g2_ref[:, sl])
            o_ref[bi, :, sl] = ((normed + bonus) * gate).astype(o_ref.dtype)


def rwkv7_mixer(pr, pr_first, vecs, w2p, a2p, g2p, v2p, batch, seq, width):
    t, n = pr.shape
    tb = _pick(seq, (256, 128, 64))
    nb = seq // tb
    n_seqs = 2 if batch % 2 == 0 else 1
    n_groups = width // SCAN_GROUP_CH
    has_vfirst = pr_first is not None
    consts = _scan_constants()
    lora_col = 3 * width // LORA_BLOCK
    row_spec = lambda col: pl.BlockSpec((n_seqs, tb, width), lambda b, s: (b, s, col))
    full = lambda arr: pl.BlockSpec(arr.shape, lambda b, s: (0,) * arr.ndim)
    weights = [w2p, a2p, g2p] + ([v2p] if has_vfirst else [])
    in_specs = ([full(consts), full(vecs)] + [full(w) for w in weights]
                + [row_spec(0), row_spec(1), row_spec(2),
                   pl.BlockSpec((n_seqs, tb, LORA_BLOCK), lambda b, s: (b, s, lora_col))]
                + ([row_spec(2)] if has_vfirst else []))
    pr3 = pr.reshape(batch, seq, n)
    args = ([consts, vecs] + weights + [pr3, pr3, pr3, pr3]
            + ([pr_first.reshape(batch, seq, n)] if has_vfirst else []))
    out = pl.pallas_call(
        functools.partial(_rwkv_kernel, n_chunks=tb // SCAN_CHUNK, n_groups=n_groups,
                          has_vfirst=has_vfirst),
        grid=(batch // n_seqs, nb),
        in_specs=in_specs,
        out_specs=pl.BlockSpec((n_seqs, tb, width), lambda b, s: (b, s, 0)),
        out_shape=jax.ShapeDtypeStruct((batch, seq, width), BF16),
        scratch_shapes=([pltpu.VMEM((n_seqs * n_groups, SCAN_GROUP_CH, SCAN_GROUP_CH), F32)]
                        + [pltpu.VMEM((n_seqs, tb, width), F32)] * 6),
        compiler_params=_params(("parallel", "arbitrary")),
        name="rwkv7_mixer",
    )(*args)
    return out.reshape(t, width)


def _diff_attn_kernel(lam_ref, q_ref, k_ref, v_ref, w_ref, o_ref,
                      vt_ref, m_ref, acc_ref, s0_ref, s1_ref, *, tq, scale_out):
    qi = pl.program_id(2)
    n_heads = q_ref.shape[1] // DIFF_V_DIM

    @pl.when(qi == 0)
    def _():
        for hh in range(n_heads):
            r0 = hh * ATTN_VT_ROWS
            for c0 in range(0, v_ref.shape[0], tq):
                vt_ref[r0:r0 + DIFF_V_DIM, c0:c0 + tq] = (
                    v_ref[c0:c0 + tq, hh * DIFF_V_DIM:(hh + 1) * DIFF_V_DIM].astype(F32).T.astype(BF16))
            vt_ref[r0 + DIFF_V_DIM:r0 + ATTN_VT_ROWS, :] = jnp.ones(
                (ATTN_VT_ROWS - DIFF_V_DIM, vt_ref.shape[1]), BF16)

    lane = lax.broadcasted_iota(jnp.int32, (tq, DIFF_V_DIM), 1)
    q_streams = []
    for hh in range(n_heads):
        q = q_ref[:, hh * DIFF_V_DIM:(hh + 1) * DIFF_V_DIM]
        zero = jnp.zeros_like(q)
        q_streams.append((hh, jnp.where(lane < DIFF_QK_DIM, q, zero)))
        q_streams.append((hh, jnp.where(lane >= DIFF_QK_DIM, q, zero)))
    n_streams = len(q_streams)

    m_ref[...] = jnp.full_like(m_ref, -jnp.inf)
    acc_ref[...] = jnp.zeros_like(acc_ref)

    tk = tq // 2
    s_bufs = (s0_ref, s1_ref)

    def scores(j, slot):
        k0 = pl.multiple_of(j * tk, tk)
        for si, (hh, qm) in enumerate(q_streams):
            kb = k_ref[pl.ds(k0, tk), hh * DIFF_V_DIM:(hh + 1) * DIFF_V_DIM]
            s_bufs[slot][si] = _dot_nt(kb, qm)

    def softmax_pv(j, slot, diag_offset=None):
        k0 = pl.multiple_of(j * tk, tk)
        for si, (hh, _) in enumerate(q_streams):
            s = s_bufs[slot][si]
            if diag_offset is not None:
                kpos = lax.broadcasted_iota(jnp.int32, s.shape, 0) + diag_offset
                qpos = lax.broadcasted_iota(jnp.int32, s.shape, 1)
                s = jnp.where(kpos <= qpos, s, -jnp.inf)
            m_old = m_ref[si]
            m_new = jnp.maximum(m_old, jnp.max(s, axis=0, keepdims=True))
            alpha = jnp.exp2(m_old - m_new)
            p = jnp.exp2(s - m_new)
            m_ref[si] = m_new
            vtb = vt_ref[hh * ATTN_VT_ROWS:(hh + 1) * ATTN_VT_ROWS, pl.ds(k0, tk)]
            acc_ref[si] = acc_ref[si] * alpha + _dot(vtb, p.astype(BF16))

    scores(0, 0)

    def body(jj, carry):
        scores(2 * jj + 1, 1)
        softmax_pv(2 * jj, 0)
        scores(2 * jj + 2, 0)
        softmax_pv(2 * jj + 1, 1)
        return carry

    lax.fori_loop(0, qi, body, 0)
    scores(2 * qi + 1, 1)
    softmax_pv(2 * qi, 0, diag_offset=0)
    softmax_pv(2 * qi + 1, 1, diag_offset=tk)

    lam = lam_ref[0]
    for hh in range(n_heads):
        a1, a2 = acc_ref[2 * hh], acc_ref[2 * hh + 1]
        nv = DIFF_V_DIM
        o_t = a1[:nv] / a1[nv:nv + 1] - lam * (a2[:nv] / a2[nv:nv + 1])
        o = o_t.T
        ms = jnp.mean(o * o, axis=-1, keepdims=True)
        o_ref[:, hh * DIFF_V_DIM:(hh + 1) * DIFF_V_DIM] = (
            o * lax.rsqrt(ms + SUBLN_EPS) * w_ref[...] * scale_out).astype(o_ref.dtype)


def diff_attention(qkv, lam, subln_w, lambda_init, batch, seq, heads):
    t = qkv.shape[0]
    tq = _pick(seq, (512, 256, 128))
    nq = seq // tq
    hb = 2 if heads % 2 == 0 else 1
    wb = hb * DIFF_V_DIM
    n_hb = heads // hb
    return pl.pallas_call(
        functools.partial(_diff_attn_kernel, tq=tq, scale_out=1.0 - lambda_init),
        grid=(batch, n_hb, nq),
        in_specs=[
            pl.BlockSpec(memory_space=pltpu.SMEM),
            pl.BlockSpec((tq, wb), lambda b, h, i: (b * nq + i, h)),
            pl.BlockSpec((seq, wb), lambda b, h, i: (b, n_hb + h)),
            pl.BlockSpec((seq, wb), lambda b, h, i: (b, 2 * n_hb + h)),
            pl.BlockSpec((1, DIFF_V_DIM), lambda b, h, i: (0, 0)),
        ],
        out_specs=pl.BlockSpec((tq, wb), lambda b, h, i: (b * nq + i, h)),
        out_shape=jax.ShapeDtypeStruct((t, heads * DIFF_V_DIM), BF16),
        scratch_shapes=[
            pltpu.VMEM((hb * ATTN_VT_ROWS, seq), BF16),
            pltpu.VMEM((2 * hb, 1, tq), F32),
            pltpu.VMEM((2 * hb, ATTN_VT_ROWS, tq), F32),
            pltpu.VMEM((2 * hb, tq // 2, tq), F32),
            pltpu.VMEM((2 * hb, tq // 2, tq), F32),
        ],
        compiler_params=_params(("parallel", "parallel", "arbitrary")),
        name="diff_attention",
    )(lam.reshape(1).astype(F32), qkv, qkv, qkv, subln_w.reshape(1, DIFF_V_DIM).astype(F32))


def _residual_norm_epilogue(m, g_ref, res_ref, gn_ref, x_ref, h_ref):
    ms = jnp.mean(m * m, axis=-1, keepdims=True)
    x = res_ref[...] + m * lax.rsqrt(ms + NORM_EPS) * g_ref[...]
    x_ref[...] = x
    if h_ref is not None:
        ms_x = jnp.mean(x * x, axis=-1, keepdims=True)
        h_ref[...] = (x * lax.rsqrt(ms_x + NORM_EPS) * gn_ref[...]).astype(h_ref.dtype)


def _out_proj_kernel(a1_ref, a2_ref, w_ref, g_ref, res_ref, gn_ref, x_ref, h_ref):
    k1 = a1_ref.shape[1]
    m = _dot(a1_ref[...], w_ref[0:k1, :]) + _dot(a2_ref[...], w_ref[k1:, :])
    _residual_norm_epilogue(m, g_ref, res_ref, gn_ref, x_ref, h_ref)


def out_proj(a1, a2, w, g, res, g_next):
    t, k1 = a1.shape
    k2 = a2.shape[1]
    d = w.shape[1]
    tm = _pick(t, (512, 256, 128))
    row = pl.BlockSpec((tm, d), lambda i: (i, 0))
    vec = pl.BlockSpec((1, d), lambda i: (0, 0))
    return pl.pallas_call(
        _out_proj_kernel,
        grid=(t // tm,),
        in_specs=[
            pl.BlockSpec((tm, k1), lambda i: (i, 0)),
            pl.BlockSpec((tm, k2), lambda i: (i, 0)),
            pl.BlockSpec((k1 + k2, d), lambda i: (0, 0)),
            vec, row, vec,
        ],
        out_specs=[row, row],
        out_shape=[jax.ShapeDtypeStruct((t, d), F32), jax.ShapeDtypeStruct((t, d), BF16)],
        compiler_params=_params(("parallel",)),
        name="out_proj",
    )(a1, a2, w, g.reshape(1, d).astype(F32), res, g_next.reshape(1, d).astype(F32))


def _ffn_up_kernel(h_ref, wg_ref, wu_ref, cw_ref, o_ref, carry_ref, *, tiles_per_seq):
    i, j = pl.program_id(0), pl.program_id(1)

    @pl.when(i % tiles_per_seq == 0)
    def _():
        carry_ref[j] = jnp.zeros(carry_ref.shape[1:], F32)

    h = h_ref[...]
    tm = h.shape[0]
    chunks = [slice(c0, c0 + V7X_MXU_DIM) for c0 in range(0, o_ref.shape[1], V7X_MXU_DIM)]
    gates = [_dot(h, wg_ref[:, cs]) for cs in chunks]
    for cs, gate in zip(chunks, gates):
        carry = carry_ref[j, :, cs]
        conv = (_rows_from_prev(gate, 2, carry) * cw_ref[0:1, cs]
                + _rows_from_prev(gate, 1, carry) * cw_ref[1:2, cs]
                + gate * cw_ref[2:3, cs] + cw_ref[3:4, cs])
        carry_ref[j, :, cs] = gate[tm - V7X_SUBLANES:, :]
        inner = math.sqrt(2.0 / math.pi) * (conv + 0.044715 * (conv * conv * conv))
        act = 0.5 * conv * (1.0 + jnp.tanh(inner))
        o_ref[:, cs] = (act * _dot(h, wu_ref[:, cs])).astype(o_ref.dtype)


def ffn_up(h, w_up, conv_wb, seq):
    t, d = h.shape
    f = w_up.shape[1] // 2
    tm = _pick(seq, (1024, 512, 256, 128))
    tn = _pick(f, (512, 256))
    nj = f // tn
    return pl.pallas_call(
        functools.partial(_ffn_up_kernel, tiles_per_seq=seq // tm),
        grid=(t // tm, nj),
        in_specs=[
            pl.BlockSpec((tm, d), lambda i, j: (i, 0)),
            pl.BlockSpec((d, tn), lambda i, j: (0, j)),
            pl.BlockSpec((d, tn), lambda i, j: (0, j + nj)),
            pl.BlockSpec((CONV_WIDTH + 1, tn), lambda i, j: (0, j)),
        ],
        out_specs=pl.BlockSpec((tm, tn), lambda i, j: (i, j)),
        out_shape=jax.ShapeDtypeStruct((t, f), BF16),
        scratch_shapes=[pltpu.VMEM((nj, V7X_SUBLANES, tn), F32)],
        compiler_params=_params(("arbitrary", "arbitrary")),
        name="ffn_up",
    )(h, w_up, w_up, conv_wb)


def _ffn_down_kernel(a_ref, w_ref, g_ref, res_ref, gn_ref, x_ref, *maybe_h_ref):
    m = _dot(a_ref[...], w_ref[...])
    _residual_norm_epilogue(m, g_ref, res_ref, gn_ref, x_ref, maybe_h_ref[0] if maybe_h_ref else None)


def ffn_down(a, w, g, res, g_next):
    t, kdim = a.shape
    d = w.shape[1]
    tm = _pick(t, (256, 128))
    row = pl.BlockSpec((tm, d), lambda i: (i, 0))
    vec = pl.BlockSpec((1, d), lambda i: (0, 0))
    has_next = g_next is not None
    gn = (g_next if has_next else g).reshape(1, d).astype(F32)
    outs = pl.pallas_call(
        _ffn_down_kernel,
        grid=(t // tm,),
        in_specs=[
            pl.BlockSpec((tm, kdim), lambda i: (i, 0)),
            pl.BlockSpec((kdim, d), lambda i: (0, 0), pipeline_mode=pl.Buffered(1)),
            vec, row, vec,
        ],
        out_specs=[row, row] if has_next else [row],
        out_shape=([jax.ShapeDtypeStruct((t, d), F32)]
                   + ([jax.ShapeDtypeStruct((t, d), BF16)] if has_next else [])),
        compiler_params=_params(("parallel",)),
        name="ffn_down",
    )(a, w, g.reshape(1, d).astype(F32), res, gn)
    return (outs[0], outs[1]) if has_next else (outs[0], None)


def _pad_cols(w, width):
    return jnp.pad(w, [(0, 0)] * (w.ndim - 1) + [(0, width - w.shape[-1])])


def _pad_rows(w, rows):
    return jnp.pad(w, ((0, rows - w.shape[0]), (0, 0)))


def kernel(x, positions, pre_mix_norm, post_mix_norm, pre_ffn_norm, post_ffn_norm, w_in, w_mv_down, shift_mu, shift_mu_mv, w0, w2, a0, a2, g2, k_k, k_a, r_k, gn_w, gn_b, v0, v2, lam_q1, lam_k1, lam_q2, lam_k2, subln_w, w_out, w_up, conv_w, conv_b, w_down):
    batch, seq, d_model = x.shape
    t = batch * seq
    depth = w_in.shape[0]
    width = w0.shape[1]
    diff_heads = (d_model - width) // DIFF_V_DIM
    rwkv_cols = shift_mu.shape[1]
    qk_cols = diff_heads * 2 * DIFF_QK_DIM
    rwkv_n = 3 * width + LORA_BLOCK
    assert rwkv_n % IN_RWKV_TN == 0 and (3 * width) % LORA_BLOCK == 0

    rope = _rope_tables(positions)
    xf = x.reshape(t, d_model)
    h = rms_norm_bf16(xf, pre_mix_norm[0])
    pr_first = None
    for l in range(depth):
        o_r, o_wl = 0, width
        o_k = o_wl + DECAY_LORA
        o_v = o_k + width
        o_al = o_v + width
        o_gl = o_al + AAA_LORA
        wl_cols = lambda m: _pad_cols(m[..., o_wl:o_k], LORA_W)
        al_cols = lambda m: _pad_cols(m[..., o_al:o_gl], LORA_A)
        gl_cols = lambda m: _pad_cols(m[..., o_gl:rwkv_cols], LORA_G)
        if l == 0:
            mv_w = jnp.zeros((d_model, LORA_MV), F32)
            mv_mu = jnp.zeros((LORA_MV,), F32)
        else:
            mv_w = _pad_cols(w_mv_down[l - 1], LORA_MV)
            mv_mu = _pad_cols(shift_mu_mv[l - 1], LORA_MV)
        arrange = lambda m, mv: jnp.concatenate(
            [m[..., o_r:o_wl], m[..., o_k:o_v], m[..., o_v:o_al], wl_cols(m), al_cols(m), gl_cols(m), mv],
            axis=-1)
        w_rwkv = _pad_cols(arrange(w_in[l], mv_w), rwkv_n).astype(BF16)
        mu_rwkv = _pad_cols(arrange(shift_mu[l], mv_mu), rwkv_n)
        pr = in_proj_rwkv(h, w_rwkv, mu_rwkv, seq)

        vecs = jnp.stack([w0[l], a0[l], k_k[l], k_a[l], r_k[l].reshape(width), gn_w[l], gn_b[l],
                          v0[l - 1] if l > 0 else jnp.zeros((width,), F32)]).astype(F32)
        w2p = _pad_rows(w2[l], LORA_W).astype(BF16)
        a2p = _pad_rows(a2[l], LORA_A).astype(BF16)
        g2p = _pad_rows(g2[l], LORA_G).astype(BF16)
        v2p = _pad_rows(v2[l - 1], LORA_MV).astype(BF16) if l > 0 else None
        o_rwkv = rwkv7_mixer(pr, pr_first, vecs, w2p, a2p, g2p, v2p, batch, seq, width)
        if l == 0:
            pr_first = pr

        qkv = in_proj_attn(h, w_in[l][:, rwkv_cols:].astype(BF16), rope, seq, qk_cols)
        lambda_init = 0.8 - 0.6 * math.exp(-0.3 * l)
        lam = (jnp.exp(jnp.sum(lam_q1[l] * lam_k1[l])) - jnp.exp(jnp.sum(lam_q2[l] * lam_k2[l]))
               + lambda_init)
        o_diff = diff_attention(qkv, lam, subln_w[l], lambda_init, batch, seq, diff_heads)

        xf, h = out_proj(o_rwkv, o_diff, w_out[l].astype(BF16), post_mix_norm[l], xf, pre_ffn_norm[l])

        conv_wb = jnp.concatenate([conv_w[l], conv_b[l][None, :]], axis=0).astype(F32)
        act = ffn_up(h, w_up[l].astype(BF16), conv_wb, seq)
        g_next = pre_mix_norm[l + 1] if l + 1 < depth else None
        xf, h = ffn_down(act, w_down[l].astype(BF16), post_ffn_norm[l], xf, g_next)
    return xf.reshape(batch, seq, d_model)
```

```python
import functools
import math

import jax
import jax.numpy as jnp
from jax import lax
from jax.experimental import pallas as pl
from jax.experimental.pallas import tpu as pltpu

F32 = jnp.float32
BF16 = jnp.bfloat16

RWKV_HEAD_DIM = 64
DIFF_QK_DIM = 64
DIFF_V_DIM = 128
DECAY_LORA = 64
AAA_LORA = 64
MV_LORA = 32
GATE_LORA = 160
CONV_WIDTH = 3
ROPE_THETA = 500000.0
ROPE_DIM = DIFF_QK_DIM // 4
NORM_EPS = 1e-6
GN_EPS = 64e-5
SUBLN_EPS = 1e-5

V7X_LANES = 128
V7X_SUBLANES = 8
V7X_MXU_DIM = 256
V7X_VMEM_LIMIT_BYTES = 56 * 1024 * 1024

SCAN_CHUNK = 64
SCAN_GROUP_HEADS = V7X_MXU_DIM // RWKV_HEAD_DIM
SCAN_GROUP_CH = SCAN_GROUP_HEADS * RWKV_HEAD_DIM
assert SCAN_CHUNK == RWKV_HEAD_DIM

LORA_W, LORA_A, LORA_G, LORA_MV = 128, 128, 256, 128
LORA_BLOCK = 768
IN_RWKV_TN = 768
IN_ATTN_TN = 1024
ATTN_VT_ROWS = DIFF_V_DIM + 16


def _pick(n, candidates):
    for c in candidates:
        if n % c == 0:
            return c
    return n


def _params(semantics):
    return pltpu.CompilerParams(dimension_semantics=semantics,
                                vmem_limit_bytes=V7X_VMEM_LIMIT_BYTES)


def _dot(a, b):
    return jnp.dot(a, b, preferred_element_type=F32)


def _dot_nt(a, b):
    return lax.dot_general(a, b, (((1,), (1,)), ((), ())), preferred_element_type=F32)


def _dot_tn(a, b):
    return lax.dot_general(a, b, (((0,), (0,)), ((), ())), preferred_element_type=F32)


def _rms_normed(x_ref, g_ref):
    x = x_ref[...]
    ms = jnp.mean(x * x, axis=-1, keepdims=True)
    return (x * lax.rsqrt(ms + NORM_EPS) * g_ref[...]).astype(BF16)


def _rows_from_prev(x, n, carry):
    rolled = pltpu.roll(x, n, 0)
    row = lax.broadcasted_iota(jnp.int32, x.shape, 0)
    for r in range(n):
        src = V7X_SUBLANES - n + r
        rolled = jnp.where(row == r, carry[src:src + 1, :], rolled)
    return rolled


def _rms_norm_kernel(x_ref, g_ref, o_ref):
    o_ref[...] = _rms_normed(x_ref, g_ref)


def rms_norm_bf16(x, g):
    t, d = x.shape
    tm = _pick(t, (1024, 512, 256, 128))
    return pl.pallas_call(
        _rms_norm_kernel,
        grid=(t // tm,),
        in_specs=[pl.BlockSpec((tm, d), lambda i: (i, 0)), pl.BlockSpec((1, d), lambda i: (0, 0))],
        out_specs=pl.BlockSpec((tm, d), lambda i: (i, 0)),
        out_shape=jax.ShapeDtypeStruct((t, d), BF16),
        compiler_params=_params(("parallel",)),
        name="rms_norm",
    )(x, g.reshape(1, d).astype(F32))


def _in_rwkv_kernel(h_ref, w_ref, mu_ref, o_ref, carry_ref, *, tiles_per_seq):
    i, j = pl.program_id(0), pl.program_id(1)

    @pl.when(i % tiles_per_seq == 0)
    def _():
        carry_ref[j] = jnp.zeros(carry_ref.shape[1:], F32)

    h = h_ref[...]
    tm = h.shape[0]
    for c0 in range(0, o_ref.shape[1], V7X_MXU_DIM):
        cs = slice(c0, c0 + V7X_MXU_DIM)
        p = _dot(h, w_ref[:, cs])
        prev = _rows_from_prev(p, 1, carry_ref[j, :, cs])
        o_ref[:, cs] = p + mu_ref[:, cs] * (prev - p)
        carry_ref[j, :, cs] = p[tm - V7X_SUBLANES:, :]


def in_proj_rwkv(h, w, mu, seq):
    t, d = h.shape
    n = w.shape[1]
    tm = _pick(seq, (1024, 512, 256, 128))
    tn = IN_RWKV_TN
    nj = n // tn
    return pl.pallas_call(
        functools.partial(_in_rwkv_kernel, tiles_per_seq=seq // tm),
        grid=(t // tm, nj),
        in_specs=[
            pl.BlockSpec((tm, d), lambda i, j: (i, 0)),
            pl.BlockSpec((d, tn), lambda i, j: (0, j)),
            pl.BlockSpec((1, tn), lambda i, j: (0, j)),
        ],
        out_specs=pl.BlockSpec((tm, tn), lambda i, j: (i, j)),
        out_shape=jax.ShapeDtypeStruct((t, n), F32),
        scratch_shapes=[pltpu.VMEM((nj, V7X_SUBLANES, tn), F32)],
        compiler_params=_params(("arbitrary", "arbitrary")),
        name="in_proj_rwkv",
    )(h, w, mu.reshape(1, n).astype(F32))


def _in_attn_kernel(h_ref, w_ref, rope_ref, o_ref, *, n_q_tiles, n_rot_tiles):
    j = pl.program_id(1)
    h = h_ref[...]
    chunks = [slice(c0, c0 + V7X_MXU_DIM) for c0 in range(0, o_ref.shape[1], V7X_MXU_DIM)]

    @pl.when(j < n_rot_tiles)
    def _():
        scale = jnp.where(j < n_q_tiles, DIFF_QK_DIM ** -0.5 * math.log2(math.e), 1.0).astype(F32)
        cos = rope_ref[:, 0:V7X_LANES] * scale
        sin_lo = rope_ref[:, V7X_LANES:2 * V7X_LANES] * scale
        sin_hi = rope_ref[:, 2 * V7X_LANES:] * scale
        half = ROPE_DIM // 2
        for cs in chunks:
            p = _dot(h, w_ref[:, cs])
            for g0 in range(0, V7X_MXU_DIM, V7X_LANES):
                xg = p[:, g0:g0 + V7X_LANES]
                rot = (xg * cos + pltpu.roll(xg, V7X_LANES - half, 1) * sin_lo
                       + pltpu.roll(xg, half, 1) * sin_hi)
                o_ref[:, cs.start + g0:cs.start + g0 + V7X_LANES] = rot.astype(o_ref.dtype)

    @pl.when(j >= n_rot_tiles)
    def _():
        for cs in chunks:
            o_ref[:, cs] = _dot(h, w_ref[:, cs]).astype(o_ref.dtype)


def in_proj_attn(h, w, rope, seq, qk_cols):
    t, d = h.shape
    n = w.shape[1]
    tm = _pick(seq, (1024, 512, 256, 128))
    tn = IN_ATTN_TN
    assert qk_cols % tn == 0 and n % tn == 0
    return pl.pallas_call(
        functools.partial(_in_attn_kernel, n_q_tiles=qk_cols // tn, n_rot_tiles=2 * qk_cols // tn),
        grid=(t // tm, n // tn),
        in_specs=[
            pl.BlockSpec((tm, d), lambda i, j: (i, 0)),
            pl.BlockSpec((d, tn), lambda i, j: (0, j)),
            pl.BlockSpec((tm, 3 * V7X_LANES), lambda i, j: (i, 0)),
        ],
        out_specs=pl.BlockSpec((tm, tn), lambda i, j: (i, j)),
        out_shape=jax.ShapeDtypeStruct((t, n), BF16),
        compiler_params=_params(("parallel", "arbitrary")),
        name="in_proj_attn",
    )(h, w, rope)


def _rope_tables(positions):
    half = ROPE_DIM // 2
    inv_freq = ROPE_THETA ** (-jnp.arange(0, ROPE_DIM, 2, dtype=F32) / ROPE_DIM)
    ang = positions.astype(F32).reshape(-1, 1) * inv_freq
    cos, sin = jnp.cos(ang), jnp.sin(ang)
    t = ang.shape[0]
    pad = DIFF_QK_DIM - ROPE_DIM
    cos_map = jnp.concatenate([cos, cos, jnp.ones((t, pad), F32)], axis=-1)
    lo_map = jnp.concatenate([-sin, jnp.zeros((t, half + pad), F32)], axis=-1)
    hi_map = jnp.concatenate([jnp.zeros((t, half), F32), sin, jnp.zeros((t, pad), F32)], axis=-1)
    reps = V7X_LANES // DIFF_QK_DIM
    return jnp.concatenate([jnp.tile(m, (1, reps)) for m in (cos_map, lo_map, hi_map)], axis=-1)


def _scan_masks():
    c, g, gc = SCAN_CHUNK, SCAN_GROUP_HEADS, SCAN_GROUP_CH
    row = lax.broadcasted_iota(jnp.int32, (g * c, gc), 0)
    col = lax.broadcasted_iota(jnp.int32, (g * c, gc), 1)
    same_block = (row // c) == (col // c)
    rowc = lax.broadcasted_iota(jnp.int32, (c, gc), 0)
    colc = lax.broadcasted_iota(jnp.int32, (c, gc), 1) % c
    strict_cat = rowc > colc
    incl_cat = rowc >= colc
    level0_cat = strict_cat & ((rowc // 2) == (colc // 2))
    eye_cat = jnp.where(rowc == colc, 1.0, 0.0).astype(F32)
    rt = lax.broadcasted_iota(jnp.int32, (c, c), 0)
    ct = lax.broadcasted_iota(jnp.int32, (c, c), 1)
    tril = jnp.where(rt >= ct, 1.0, 0.0).astype(BF16)
    return same_block, strict_cat, incl_cat, level0_cat, eye_cat, tril


def _scan_constants():
    idx = jnp.arange(SCAN_GROUP_HEADS * SCAN_CHUNK)
    t, s = idx[:, None], idx[None, :]
    mats = [t // 2 == s // 2]
    b = 2
    while b < SCAN_CHUNK:
        mats.append((t // (2 * b) == s // (2 * b)) & (t % (2 * b) >= b) & (s % (2 * b) < b))
        b *= 2
    mats.append(t // RWKV_HEAD_DIM == s // RWKV_HEAD_DIM)
    return jnp.stack(mats).astype(BF16)


def _sigmoid(x):
    return 0.5 * jnp.tanh(0.5 * x) + 0.5


def _head_sum(x, ones_bd, split=False):
    hi = x.astype(BF16)
    total = _dot(hi, ones_bd)
    if split:
        total = total + _dot((x - hi.astype(F32)).astype(BF16), ones_bd)
    return total


def _scan_chunk(ops, states, masks, const_ref):
    c, g, gc = SCAN_CHUNK, SCAN_GROUP_HEADS, SCAN_GROUP_CH
    same_block, strict_cat, incl_cat, level0_cat, eye_cat, tril = masks
    n_levels = const_ref.shape[0] - 1
    groups = range(len(ops))

    def tile(x):
        return jnp.concatenate([x] * g, axis=0)

    def bdexp(x):
        return jnp.where(same_block, tile(x), jnp.zeros_like(tile(x)))

    cums = []
    for lw, *_ in ops:
        lw_hi = lw.astype(BF16)
        lw_lo = (lw - lw_hi.astype(F32)).astype(BF16)
        cums.append(_dot(tril, lw_hi) + _dot(tril, lw_lo))

    a_b, b_b, k_b, r_b, end_b, v_b, w_end = [], [], [], [], [], [], []
    for (lw, r, k, v, kk, a), cum in zip(ops, cums):
        w_inc = jnp.exp(cum)
        w_inv = jnp.exp(-cum)
        w_prev = jnp.exp(cum - lw)
        w_last = w_inc[c - 1:c, :]
        b_f = kk * a * w_inv
        k_f = k * w_inv
        a_b.append((-(kk * w_prev)).astype(BF16))
        b_b.append(b_f.astype(BF16))
        k_b.append(k_f.astype(BF16))
        r_b.append((r * w_inc).astype(BF16))
        end_b.append(jnp.concatenate([(b_f * w_last).astype(BF16), (k_f * w_last).astype(BF16)], axis=0))
        v_b.append(v.astype(BF16))
        w_end.append(w_last)

    ar_b = [jnp.concatenate([a_b[i], r_b[i]], axis=0) for i in groups]
    l_cat = [_dot_nt(ar_b[i], jnp.concatenate([bdexp(b_b[i]), bdexp(k_b[i])], axis=0))
             for i in groups]
    l_ab = [jnp.where(strict_cat, lc[:c, :gc], 0.0).astype(BF16) for lc in l_cat]

    l_bd = [bdexp(l) for l in l_ab]
    t_cat = [eye_cat + jnp.where(level0_cat, l, jnp.zeros_like(l)).astype(F32) for l in l_ab]
    for lvl in range(1, n_levels):
        t_b = [t.astype(BF16) for t in t_cat]
        half = [_dot(t_b[i], l_bd[i] * const_ref[lvl]).astype(BF16) for i in groups]
        t_cat = [t_cat[i] + _dot(half[i], bdexp(t_b[i])) for i in groups]
    t_cat = [t.astype(BF16) for t in t_cat]

    l_rb = [jnp.where(incl_cat, lc[c:, :gc], 0.0).astype(BF16) for lc in l_cat]
    l_k = [jnp.concatenate([jnp.where(strict_cat, lc[:c, gc:], 0.0).astype(BF16),
                            jnp.where(incl_cat, lc[c:, gc:], 0.0).astype(BF16)], axis=0)
           for lc in l_cat]
    from_v = [_dot(l_k[i], bdexp(v_b[i])) for i in groups]

    from_state = [_dot_nt(ar_b[i], states[i].astype(BF16)) for i in groups]
    x_b = [(from_state[i][:c] + from_v[i][:c]).astype(BF16) for i in groups]
    u_b = [_dot(t_cat[i], bdexp(x_b[i])).astype(BF16) for i in groups]
    upd = [_dot_tn(jnp.concatenate([u_b[i], v_b[i]], axis=0), end_b[i]) for i in groups]
    new_states = [jnp.where(same_block, states[i] * w_end[i] + upd[i], 0.0) for i in groups]
    outs = [from_state[i][c:] + from_v[i][c:] + _dot(l_rb[i], bdexp(u_b[i])) for i in groups]
    return outs, new_states


_VEC_W0, _VEC_A0, _VEC_KK, _VEC_KA, _VEC_RK, _VEC_GNW, _VEC_GNB, _VEC_V0 = range(8)


def _rwkv_kernel(*refs, n_chunks, n_groups, has_vfirst):
    if has_vfirst:
        (const_ref, vec_ref, w2_ref, a2_ref, g2_ref, v2_ref, r_ref, k_ref, v_ref, lora_ref, vf_ref,
         o_ref, state_ref, lw_s, a_s, kk_s, k2_s, v_s, o_s) = refs
    else:
        (const_ref, vec_ref, w2_ref, a2_ref, g2_ref, r_ref, k_ref, v_ref, lora_ref,
         o_ref, state_ref, lw_s, a_s, kk_s, k2_s, v_s, o_s) = refs
        v2_ref = vf_ref = None

    @pl.when(pl.program_id(1) == 0)
    def _():
        state_ref[...] = jnp.zeros_like(state_ref)

    c, gc = SCAN_CHUNK, SCAN_GROUP_CH
    n_seqs = r_ref.shape[0]
    ones_bd = const_ref[const_ref.shape[0] - 1]
    vec = lambda row, sl: vec_ref[row:row + 1, sl]

    o0 = LORA_W
    o1 = o0 + LORA_A
    o2 = o1 + LORA_G
    o3 = o2 + LORA_MV

    for bi in range(n_seqs):
        tanh_wl = jnp.tanh(lora_ref[bi, :, 0:o0]).astype(BF16)
        al = lora_ref[bi, :, o0:o1].astype(BF16)
        for gi in range(n_groups):
            sl = slice(gi * gc, (gi + 1) * gc)
            zw = vec(_VEC_W0, sl) + _dot(tanh_wl, w2_ref[:, sl])
            lw_s[bi, :, sl] = -math.exp(-0.5) * _sigmoid(zw)
            a = _sigmoid(vec(_VEC_A0, sl) + _dot(al, a2_ref[:, sl]))
            a_s[bi, :, sl] = a
            v = v_ref[bi, :, sl]
            if has_vfirst:
                mix = _sigmoid(vec(_VEC_V0, sl)
                               + _dot(lora_ref[bi, :, o2:o3].astype(BF16), v2_ref[:, sl]))
                v = v + (vf_ref[bi, :, sl] - v) * mix
            v_s[bi, :, sl] = v
            k = k_ref[bi, :, sl]
            kk = k * vec(_VEC_KK, sl)
            ss = _head_sum(kk * kk, ones_bd)
            kk_s[bi, :, sl] = kk * lax.rsqrt(jnp.maximum(ss, 1e-24))
            k2_s[bi, :, sl] = k * (1.0 + (a - 1.0) * vec(_VEC_KA, sl))

    masks = _scan_masks()
    problems = [(bi, gi) for bi in range(n_seqs) for gi in range(n_groups)]

    def body(ci, carry):
        t0 = pl.multiple_of(ci * c, c)
        sls = [(bi, pl.ds(t0, c), slice(gi * gc, (gi + 1) * gc)) for bi, gi in problems]
        ops = [(lw_s[sl], r_ref[sl], k2_s[sl], v_s[sl], kk_s[sl], a_s[sl]) for sl in sls]
        outs, new_states = _scan_chunk(ops, [state_ref[pi] for pi in range(len(problems))],
                                       masks, const_ref)
        for pi in range(len(problems)):
            o_s[sls[pi]] = outs[pi]
            state_ref[pi] = new_states[pi]
        return carry

    lax.fori_loop(0, n_chunks, body, 0)

    inv_n = 1.0 / RWKV_HEAD_DIM
    for bi in range(n_seqs):
        sig_gl = _sigmoid(lora_ref[bi, :, o1:o2]).astype(BF16)
        for gi in range(n_groups):
            sl = slice(gi * gc, (gi + 1) * gc)
            o = o_s[bi, :, sl]
            dev = o - _head_sum(o, ones_bd) * inv_n
            var = _head_sum(dev * dev, ones_bd) * inv_n
            normed = dev * lax.rsqrt(var + GN_EPS) * vec(_VEC_GNW, sl) + vec(_VEC_GNB, sl)
            bonus = _head_sum(r_ref[bi, :, sl] * k2_s[bi, :, sl] * vec(_VEC_RK, sl), ones_bd,
                              split=True) * v_s[bi, :, sl]
            gate = _dot(sig_gl, g2_ref[:, sl])
            o_ref[bi, :, sl] = ((normed + bonus) * gate).astype(o_ref.dtype)


def rwkv7_mixer(pr, pr_first, vecs, w2p, a2p, g2p, v2p, batch, seq, width):
    t, n = pr.shape
    tb = _pick(seq, (256, 128, 64))
    nb = seq // tb
    n_seqs = 2 if batch % 2 == 0 else 1
    n_groups = width // SCAN_GROUP_CH
    has_vfirst = pr_first is not None
    consts = _scan_constants()
    lora_col = 3 * width // LORA_BLOCK
    row_spec = lambda col: pl.BlockSpec((n_seqs, tb, width), lambda b, s: (b, s, col))
    full = lambda arr: pl.BlockSpec(arr.shape, lambda b, s: (0,) * arr.ndim)
    weights = [w2p, a2p, g2p] + ([v2p] if has_vfirst else [])
    in_specs = ([full(consts), full(vecs)] + [full(w) for w in weights]
                + [row_spec(0), row_spec(1), row_spec(2),
                   pl.BlockSpec((n_seqs, tb, LORA_BLOCK), lambda b, s: (b, s, lora_col))]
                + ([row_spec(2)] if has_vfirst else []))
    pr3 = pr.reshape(batch, seq, n)
    args = ([consts, vecs] + weights + [pr3, pr3, pr3, pr3]
            + ([pr_first.reshape(batch, seq, n)] if has_vfirst else []))
    out = pl.pallas_call(
        functools.partial(_rwkv_kernel, n_chunks=tb // SCAN_CHUNK, n_groups=n_groups,
                          has_vfirst=has_vfirst),
        grid=(batch // n_seqs, nb),
        in_specs=in_specs,
        out_specs=pl.BlockSpec((n_seqs, tb, width), lambda b, s: (b, s, 0)),
        out_shape=jax.ShapeDtypeStruct((batch, seq, width), BF16),
        scratch_shapes=([pltpu.VMEM((n_seqs * n_groups, SCAN_GROUP_CH, SCAN_GROUP_CH), F32)]
                        + [pltpu.VMEM((n_seqs, tb, width), F32)] * 6),
        compiler_params=_params(("parallel", "arbitrary")),
        name="rwkv7_mixer",
    )(*args)
    return out.reshape(t, width)


def _diff_attn_kernel(lam_ref, q_ref, k_ref, v_ref, w_ref, o_ref,
                      vt_ref, m_ref, acc_ref, s0_ref, s1_ref, *, tq, scale_out):
    qi = pl.program_id(2)
    n_heads = q_ref.shape[1] // DIFF_V_DIM

    @pl.when(qi == 0)
    def _():
        for hh in range(n_heads):
            r0 = hh * ATTN_VT_ROWS
            for c0 in range(0, v_ref.shape[0], tq):
                vt_ref[r0:r0 + DIFF_V_DIM, c0:c0 + tq] = (
                    v_ref[c0:c0 + tq, hh * DIFF_V_DIM:(hh + 1) * DIFF_V_DIM].astype(F32).T.astype(BF16))
            vt_ref[r0 + DIFF_V_DIM:r0 + ATTN_VT_ROWS, :] = jnp.ones(
                (ATTN_VT_ROWS - DIFF_V_DIM, vt_ref.shape[1]), BF16)

    lane = lax.broadcasted_iota(jnp.int32, (tq, DIFF_V_DIM), 1)
    q_streams = []
    for hh in range(n_heads):
        q = q_ref[:, hh * DIFF_V_DIM:(hh + 1) * DIFF_V_DIM]
        zero = jnp.zeros_like(q)
        q_streams.append((hh, jnp.where(lane < DIFF_QK_DIM, q, zero)))
        q_streams.append((hh, jnp.where(lane >= DIFF_QK_DIM, q, zero)))
    n_streams = len(q_streams)

    m_ref[...] = jnp.full_like(m_ref, -jnp.inf)
    acc_ref[...] = jnp.zeros_like(acc_ref)

    tk = tq // 2
    s_bufs = (s0_ref, s1_ref)

    every_query = slice(0, tq)
    late_queries = slice(tk, tq)

    def scores(j, slot, cols=every_query):
        k0 = pl.multiple_of(j * tk, tk)
        for si, (hh, qm) in enumerate(q_streams):
            kb = k_ref[pl.ds(k0, tk), hh * DIFF_V_DIM:(hh + 1) * DIFF_V_DIM]
            s_bufs[slot][si, :, cols] = _dot_nt(kb, qm[cols])

    def softmax_pv(j, slot, diag=False, cols=every_query):
        k0 = pl.multiple_of(j * tk, tk)
        for si, (hh, _) in enumerate(q_streams):
            s = s_bufs[slot][si, :, cols]
            if diag:
                kpos = lax.broadcasted_iota(jnp.int32, s.shape, 0)
                qpos = lax.broadcasted_iota(jnp.int32, s.shape, 1)
                s = jnp.where(kpos <= qpos, s, -jnp.inf)
            m_old = m_ref[si, :, cols]
            m_new = jnp.maximum(m_old, jnp.max(s, axis=0, keepdims=True))
            alpha = jnp.exp2(m_old - m_new)
            p = jnp.exp2(s - m_new)
            m_ref[si, :, cols] = m_new
            vtb = vt_ref[hh * ATTN_VT_ROWS:(hh + 1) * ATTN_VT_ROWS, pl.ds(k0, tk)]
            acc_ref[si, :, cols] = acc_ref[si, :, cols] * alpha + _dot(vtb, p.astype(BF16))

    scores(0, 0)

    def body(jj, carry):
        scores(2 * jj + 1, 1)
        softmax_pv(2 * jj, 0)
        scores(2 * jj + 2, 0)
        softmax_pv(2 * jj + 1, 1)
        return carry

    lax.fori_loop(0, qi, body, 0)
    scores(2 * qi + 1, 1, cols=late_queries)
    softmax_pv(2 * qi, 0, diag=True)
    softmax_pv(2 * qi + 1, 1, diag=True, cols=late_queries)

    lam = lam_ref[0]
    for hh in range(n_heads):
        a1, a2 = acc_ref[2 * hh], acc_ref[2 * hh + 1]
        nv = DIFF_V_DIM
        o_t = a1[:nv] * (1.0 / a1[nv:nv + 1]) - a2[:nv] * (lam / a2[nv:nv + 1])
        o = o_t.T
        ms = jnp.mean(o * o, axis=-1, keepdims=True)
        o_ref[:, hh * DIFF_V_DIM:(hh + 1) * DIFF_V_DIM] = (
            o * lax.rsqrt(ms + SUBLN_EPS) * w_ref[...] * scale_out).astype(o_ref.dtype)


def diff_attention(qkv, lam, subln_w, lambda_init, batch, seq, heads):
    t = qkv.shape[0]
    tq = _pick(seq, (1024, 512, 256, 128))
    nq = seq // tq
    hb = 2 if heads % 2 == 0 else 1
    wb = hb * DIFF_V_DIM
    n_hb = heads // hb
    return pl.pallas_call(
        functools.partial(_diff_attn_kernel, tq=tq, scale_out=1.0 - lambda_init),
        grid=(batch, n_hb, nq),
        in_specs=[
            pl.BlockSpec(memory_space=pltpu.SMEM),
            pl.BlockSpec((tq, wb), lambda b, h, i: (b * nq + i, h)),
            pl.BlockSpec((seq, wb), lambda b, h, i: (b, n_hb + h)),
            pl.BlockSpec((seq, wb), lambda b, h, i: (b, 2 * n_hb + h)),
            pl.BlockSpec((1, DIFF_V_DIM), lambda b, h, i: (0, 0)),
        ],
        out_specs=pl.BlockSpec((tq, wb), lambda b, h, i: (b * nq + i, h)),
        out_shape=jax.ShapeDtypeStruct((t, heads * DIFF_V_DIM), BF16),
        scratch_shapes=[
            pltpu.VMEM((hb * ATTN_VT_ROWS, seq), BF16),
            pltpu.VMEM((2 * hb, 1, tq), F32),
            pltpu.VMEM((2 * hb, ATTN_VT_ROWS, tq), F32),
            pltpu.VMEM((2 * hb, tq // 2, tq), F32),
            pltpu.VMEM((2 * hb, tq // 2, tq), F32),
        ],
        compiler_params=_params(("parallel", "parallel", "arbitrary")),
        name="diff_attention",
    )(lam.reshape(1).astype(F32), qkv, qkv, qkv, subln_w.reshape(1, DIFF_V_DIM).astype(F32))


def _residual_norm_epilogue(m, g_ref, res_ref, gn_ref, x_ref, h_ref, rows=slice(None)):
    ms = jnp.mean(m * m, axis=-1, keepdims=True)
    x = res_ref[rows, :] + m * lax.rsqrt(ms + NORM_EPS) * g_ref[...]
    x_ref[rows, :] = x
    if h_ref is not None:
        ms_x = jnp.mean(x * x, axis=-1, keepdims=True)
        h_ref[rows, :] = (x * lax.rsqrt(ms_x + NORM_EPS) * gn_ref[...]).astype(h_ref.dtype)


def _out_proj_kernel(a1_ref, a2_ref, w_ref, g_ref, res_ref, gn_ref, x_ref, h_ref):
    k1 = a1_ref.shape[1]
    tm = a1_ref.shape[0]
    for r0 in range(0, tm, V7X_MXU_DIM):
        rows = slice(r0, min(r0 + V7X_MXU_DIM, tm))
        m = _dot(a1_ref[rows, :], w_ref[0:k1, :]) + _dot(a2_ref[rows, :], w_ref[k1:, :])
        _residual_norm_epilogue(m, g_ref, res_ref, gn_ref, x_ref, h_ref, rows)


def out_proj(a1, a2, w, layer, g, res, g_next):
    t, k1 = a1.shape
    k2 = a2.shape[1]
    d = w.shape[2]
    tm = _pick(t, (512, 256, 128))
    row = pl.BlockSpec((tm, d), lambda i: (i, 0))
    vec = pl.BlockSpec((1, d), lambda i: (0, 0))
    return pl.pallas_call(
        _out_proj_kernel,
        grid=(t // tm,),
        in_specs=[
            pl.BlockSpec((tm, k1), lambda i: (i, 0)),
            pl.BlockSpec((tm, k2), lambda i: (i, 0)),
            pl.BlockSpec((None, k1 + k2, d), lambda i: (layer, 0, 0)),
            vec, row, vec,
        ],
        out_specs=[row, row],
        out_shape=[jax.ShapeDtypeStruct((t, d), F32), jax.ShapeDtypeStruct((t, d), BF16)],
        compiler_params=_params(("parallel",)),
        name="out_proj",
    )(a1, a2, w, g.reshape(1, d).astype(F32), res, g_next.reshape(1, d).astype(F32))


def _ffn_up_kernel(h_ref, wg_ref, wu_ref, cw_ref, o_ref, carry_ref, *, tiles_per_seq):
    i, j = pl.program_id(0), pl.program_id(1)

    @pl.when(i % tiles_per_seq == 0)
    def _():
        carry_ref[j] = jnp.zeros(carry_ref.shape[1:], F32)

    h = h_ref[...]
    tm = h.shape[0]
    chunks = [slice(c0, c0 + V7X_MXU_DIM) for c0 in range(0, o_ref.shape[1], V7X_MXU_DIM)]
    gates = [_dot(h, wg_ref[:, cs].astype(BF16)) for cs in chunks]
    for cs, gate in zip(chunks, gates):
        carry = carry_ref[j, :, cs]
        conv = (_rows_from_prev(gate, 2, carry) * cw_ref[0:1, cs]
                + _rows_from_prev(gate, 1, carry) * cw_ref[1:2, cs]
                + gate * cw_ref[2:3, cs] + cw_ref[3:4, cs])
        carry_ref[j, :, cs] = gate[tm - V7X_SUBLANES:, :]
        inner = math.sqrt(2.0 / math.pi) * (conv + 0.044715 * (conv * conv * conv))
        act = 0.5 * conv * (1.0 + jnp.tanh(inner))
        o_ref[:, cs] = (act * _dot(h, wu_ref[:, cs].astype(BF16))).astype(o_ref.dtype)


def ffn_up(h, w_up, layer, conv_wb, seq):
    t, d = h.shape
    f = w_up.shape[2] // 2
    tm = _pick(seq, (1024, 512, 256, 128))
    tn = _pick(f, (512, 256))
    nj = f // tn
    return pl.pallas_call(
        functools.partial(_ffn_up_kernel, tiles_per_seq=seq // tm),
        grid=(t // tm, nj),
        in_specs=[
            pl.BlockSpec((tm, d), lambda i, j: (i, 0)),
            pl.BlockSpec((None, d, tn), lambda i, j: (layer, 0, j)),
            pl.BlockSpec((None, d, tn), lambda i, j: (layer, 0, j + nj)),
            pl.BlockSpec((CONV_WIDTH + 1, tn), lambda i, j: (0, j)),
        ],
        out_specs=pl.BlockSpec((tm, tn), lambda i, j: (i, j)),
        out_shape=jax.ShapeDtypeStruct((t, f), BF16),
        scratch_shapes=[pltpu.VMEM((nj, V7X_SUBLANES, tn), F32)],
        compiler_params=_params(("arbitrary", "arbitrary")),
        name="ffn_up",
    )(h, w_up, w_up, conv_wb)


def _ffn_down_kernel(a_ref, w_ref, g_ref, res_ref, gn_ref, x_ref, *maybe_h_ref):
    m = _dot(a_ref[...], w_ref[...])
    _residual_norm_epilogue(m, g_ref, res_ref, gn_ref, x_ref, maybe_h_ref[0] if maybe_h_ref else None)


def ffn_down(a, w, layer, g, res, g_next):
    t, kdim = a.shape
    d = w.shape[2]
    tm = _pick(t, (256, 128))
    row = pl.BlockSpec((tm, d), lambda i: (i, 0))
    vec = pl.BlockSpec((1, d), lambda i: (0, 0))
    has_next = g_next is not None
    gn = (g_next if has_next else g).reshape(1, d).astype(F32)
    outs = pl.pallas_call(
        _ffn_down_kernel,
        grid=(t // tm,),
        in_specs=[
            pl.BlockSpec((tm, kdim), lambda i: (i, 0)),
            pl.BlockSpec((None, kdim, d), lambda i: (layer, 0, 0), pipeline_mode=pl.Buffered(1)),
            vec, row, vec,
        ],
        out_specs=[row, row] if has_next else [row],
        out_shape=([jax.ShapeDtypeStruct((t, d), F32)]
                   + ([jax.ShapeDtypeStruct((t, d), BF16)] if has_next else [])),
        compiler_params=_params(("parallel",)),
        name="ffn_down",
    )(a, w, g.reshape(1, d).astype(F32), res, gn)
    return (outs[0], outs[1]) if has_next else (outs[0], None)


def _pad_cols(w, width):
    return jnp.pad(w, [(0, 0)] * (w.ndim - 1) + [(0, width - w.shape[-1])])


def _pad_rows(w, rows):
    return jnp.pad(w, ((0, rows - w.shape[0]), (0, 0)))


def kernel(x, positions, pre_mix_norm, post_mix_norm, pre_ffn_norm, post_ffn_norm, w_in, w_mv_down, shift_mu, shift_mu_mv, w0, w2, a0, a2, g2, k_k, k_a, r_k, gn_w, gn_b, v0, v2, lam_q1, lam_k1, lam_q2, lam_k2, subln_w, w_out, w_up, conv_w, conv_b, w_down):
    batch, seq, d_model = x.shape
    t = batch * seq
    depth = w_in.shape[0]
    width = w0.shape[1]
    diff_heads = (d_model - width) // DIFF_V_DIM
    rwkv_cols = shift_mu.shape[1]
    qk_cols = diff_heads * 2 * DIFF_QK_DIM
    rwkv_n = 3 * width + LORA_BLOCK
    assert rwkv_n % IN_RWKV_TN == 0 and (3 * width) % LORA_BLOCK == 0

    rope = _rope_tables(positions)
    w_out_b = w_out.astype(BF16)
    w_down_b = w_down.astype(BF16)
    xf = x.reshape(t, d_model)
    h = rms_norm_bf16(xf, pre_mix_norm[0])
    pr_first = None
    for l in range(depth):
        o_r, o_wl = 0, width
        o_k = o_wl + DECAY_LORA
        o_v = o_k + width
        o_al = o_v + width
        o_gl = o_al + AAA_LORA
        wl_cols = lambda m: _pad_cols(m[..., o_wl:o_k], LORA_W)
        al_cols = lambda m: _pad_cols(m[..., o_al:o_gl], LORA_A)
        gl_cols = lambda m: _pad_cols(m[..., o_gl:rwkv_cols], LORA_G)
        if l == 0:
            mv_w = jnp.zeros((d_model, LORA_MV), F32)
            mv_mu = jnp.zeros((LORA_MV,), F32)
        else:
            mv_w = _pad_cols(w_mv_down[l - 1], LORA_MV)
            mv_mu = _pad_cols(shift_mu_mv[l - 1], LORA_MV)
        arrange = lambda m, mv: jnp.concatenate(
            [m[..., o_r:o_wl], m[..., o_k:o_v], m[..., o_v:o_al], wl_cols(m), al_cols(m), gl_cols(m), mv],
            axis=-1)
        w_rwkv = _pad_cols(arrange(w_in[l], mv_w), rwkv_n).astype(BF16)
        mu_rwkv = _pad_cols(arrange(shift_mu[l], mv_mu), rwkv_n)
        pr = in_proj_rwkv(h, w_rwkv, mu_rwkv, seq)

        vecs = jnp.stack([w0[l], a0[l], k_k[l], k_a[l], r_k[l].reshape(width), gn_w[l], gn_b[l],
                          v0[l - 1] if l > 0 else jnp.zeros((width,), F32)]).astype(F32)
        w2p = _pad_rows(w2[l], LORA_W).astype(BF16)
        a2p = _pad_rows(a2[l], LORA_A).astype(BF16)
        g2p = _pad_rows(g2[l], LORA_G).astype(BF16)
        v2p = _pad_rows(v2[l - 1], LORA_MV).astype(BF16) if l > 0 else None
        o_rwkv = rwkv7_mixer(pr, pr_first, vecs, w2p, a2p, g2p, v2p, batch, seq, width)
        if l == 0:
            pr_first = pr

        qkv = in_proj_attn(h, w_in[l][:, rwkv_cols:].astype(BF16), rope, seq, qk_cols)
        lambda_init = 0.8 - 0.6 * math.exp(-0.3 * l)
        lam = (jnp.exp(jnp.sum(lam_q1[l] * lam_k1[l])) - jnp.exp(jnp.sum(lam_q2[l] * lam_k2[l]))
               + lambda_init)
        o_diff = diff_attention(qkv, lam, subln_w[l], lambda_init, batch, seq, diff_heads)

        xf, h = out_proj(o_rwkv, o_diff, w_out_b, l, post_mix_norm[l], xf, pre_ffn_norm[l])

        conv_wb = jnp.concatenate([conv_w[l], conv_b[l][None, :]], axis=0).astype(F32)
        act = ffn_up(h, w_up, l, conv_wb, seq)
        g_next = pre_mix_norm[l + 1] if l + 1 < depth else None
        xf, h = ffn_down(act, w_down_b, l, post_ffn_norm[l], xf, g_next)
    return xf.reshape(batch, seq, d_model)
```

```python
import functools
import math

import jax
import jax.numpy as jnp
from jax import lax
from jax.experimental import pallas as pl
from jax.experimental.pallas import tpu as pltpu

F32 = jnp.float32
BF16 = jnp.bfloat16

RWKV_HEAD_DIM = 64
DIFF_QK_DIM = 64
DIFF_V_DIM = 128
DECAY_LORA = 64
AAA_LORA = 64
MV_LORA = 32
GATE_LORA = 160
CONV_WIDTH = 3
ROPE_THETA = 500000.0
ROPE_DIM = DIFF_QK_DIM // 4
NORM_EPS = 1e-6
GN_EPS = 64e-5
SUBLN_EPS = 1e-5

V7X_LANES = 128
V7X_SUBLANES = 8
V7X_MXU_DIM = 256
V7X_VMEM_LIMIT_BYTES = 56 * 1024 * 1024

SCAN_CHUNK = 64
SCAN_GROUP_HEADS = V7X_MXU_DIM // RWKV_HEAD_DIM
SCAN_GROUP_CH = SCAN_GROUP_HEADS * RWKV_HEAD_DIM
assert SCAN_CHUNK == RWKV_HEAD_DIM

LORA_WA = 128
LORA_GM = 256
LORA_BLOCK = 512
IN_RWKV_TN = 1792
IN_ATTN_TN = 1536
ATTN_VT_ROWS = DIFF_V_DIM + 16


def _pick(n, candidates):
    for c in candidates:
        if n % c == 0:
            return c
    return n


def _params(semantics):
    return pltpu.CompilerParams(dimension_semantics=semantics,
                                vmem_limit_bytes=V7X_VMEM_LIMIT_BYTES)


def _dot(a, b):
    return jnp.dot(a, b, preferred_element_type=F32)


def _dot_nt(a, b):
    return lax.dot_general(a, b, (((1,), (1,)), ((), ())), preferred_element_type=F32)


def _dot_tn(a, b):
    return lax.dot_general(a, b, (((0,), (0,)), ((), ())), preferred_element_type=F32)


def _rms_normed(x_ref, g_ref):
    x = x_ref[...]
    ms = jnp.mean(x * x, axis=-1, keepdims=True)
    return (x * lax.rsqrt(ms + NORM_EPS) * g_ref[...]).astype(BF16)


def _rows_from_prev(x, n, carry):
    rolled = pltpu.roll(x, n, 0)
    row = lax.broadcasted_iota(jnp.int32, x.shape, 0)
    for r in range(n):
        src = V7X_SUBLANES - n + r
        rolled = jnp.where(row == r, carry[src:src + 1, :], rolled)
    return rolled


def _rms_norm_kernel(x_ref, g_ref, o_ref):
    o_ref[...] = _rms_normed(x_ref, g_ref)


def rms_norm_bf16(x, g):
    t, d = x.shape
    tm = _pick(t, (1024, 512, 256, 128))
    return pl.pallas_call(
        _rms_norm_kernel,
        grid=(t // tm,),
        in_specs=[pl.BlockSpec((tm, d), lambda i: (i, 0)), pl.BlockSpec((1, d), lambda i: (0, 0))],
        out_specs=pl.BlockSpec((tm, d), lambda i: (i, 0)),
        out_shape=jax.ShapeDtypeStruct((t, d), BF16),
        compiler_params=_params(("parallel",)),
        name="rms_norm",
    )(x, g.reshape(1, d).astype(F32))


def _in_rwkv_kernel(h_ref, w_ref, mu_ref, o_ref, carry_ref, *, tiles_per_seq):
    i, j = pl.program_id(0), pl.program_id(1)

    @pl.when(i % tiles_per_seq == 0)
    def _():
        carry_ref[j] = jnp.zeros(carry_ref.shape[1:], F32)

    h = h_ref[...]
    tm = h.shape[0]
    for c0 in range(0, o_ref.shape[1], V7X_MXU_DIM):
        cs = slice(c0, c0 + V7X_MXU_DIM)
        p = _dot(h, w_ref[:, cs])
        prev = _rows_from_prev(p, 1, carry_ref[j, :, cs])
        o_ref[:, cs] = p + mu_ref[:, cs] * (prev - p)
        carry_ref[j, :, cs] = p[tm - V7X_SUBLANES:, :]


def in_proj_rwkv(h, w, mu, seq):
    t, d = h.shape
    n = w.shape[1]
    tm = _pick(seq, (1024, 512, 256, 128))
    tn = IN_RWKV_TN
    nj = n // tn
    return pl.pallas_call(
        functools.partial(_in_rwkv_kernel, tiles_per_seq=seq // tm),
        grid=(t // tm, nj),
        in_specs=[
            pl.BlockSpec((tm, d), lambda i, j: (i, 0)),
            pl.BlockSpec((d, tn), lambda i, j: (0, j)),
            pl.BlockSpec((1, tn), lambda i, j: (0, j)),
        ],
        out_specs=pl.BlockSpec((tm, tn), lambda i, j: (i, j)),
        out_shape=jax.ShapeDtypeStruct((t, n), F32),
        scratch_shapes=[pltpu.VMEM((nj, V7X_SUBLANES, tn), F32)],
        compiler_params=_params(("arbitrary", "arbitrary")),
        name="in_proj_rwkv",
    )(h, w, mu.reshape(1, n).astype(F32))


def _in_attn_kernel(h_ref, w_ref, rope_ref, o_ref, *, qk_cols, n_tiles):
    j = pl.program_id(1)
    tn = o_ref.shape[1]
    half = ROPE_DIM // 2
    q_scale = DIFF_QK_DIM ** -0.5 * math.log2(math.e)

    def tile(jt):
        h = h_ref[...]
        tables = {}
        for c0 in range(0, tn, V7X_MXU_DIM):
            col = jt * tn + c0
            p = _dot(h, w_ref[:, c0:c0 + V7X_MXU_DIM])
            if col >= 2 * qk_cols:
                o_ref[:, c0:c0 + V7X_MXU_DIM] = p.astype(o_ref.dtype)
                continue
            scale = q_scale if col < qk_cols else 1.0
            if scale not in tables:
                tables[scale] = [rope_ref[:, k * V7X_LANES:(k + 1) * V7X_LANES] * scale
                                 for k in range(3)]
            cos, sin_lo, sin_hi = tables[scale]
            for g0 in range(0, V7X_MXU_DIM, V7X_LANES):
                xg = p[:, g0:g0 + V7X_LANES]
                rot = (xg * cos + pltpu.roll(xg, V7X_LANES - half, 1) * sin_lo
                       + pltpu.roll(xg, half, 1) * sin_hi)
                o_ref[:, c0 + g0:c0 + g0 + V7X_LANES] = rot.astype(o_ref.dtype)

    for jt in range(n_tiles):
        pl.when(j == jt)(functools.partial(tile, jt))


def in_proj_attn(h, w, rope, seq, qk_cols):
    t, d = h.shape
    n = w.shape[1]
    tm = _pick(seq, (1024, 512, 256, 128))
    tn = IN_ATTN_TN
    assert qk_cols % V7X_MXU_DIM == 0 and n % tn == 0 and tn % V7X_MXU_DIM == 0
    return pl.pallas_call(
        functools.partial(_in_attn_kernel, qk_cols=qk_cols, n_tiles=n // tn),
        grid=(t // tm, n // tn),
        in_specs=[
            pl.BlockSpec((tm, d), lambda i, j: (i, 0)),
            pl.BlockSpec((d, tn), lambda i, j: (0, j)),
            pl.BlockSpec((tm, 3 * V7X_LANES), lambda i, j: (i, 0)),
        ],
        out_specs=pl.BlockSpec((tm, tn), lambda i, j: (i, j)),
        out_shape=jax.ShapeDtypeStruct((t, n), BF16),
        compiler_params=_params(("parallel", "arbitrary")),
        name="in_proj_attn",
    )(h, w, rope)


def _rope_tables(positions):
    half = ROPE_DIM // 2
    inv_freq = ROPE_THETA ** (-jnp.arange(0, ROPE_DIM, 2, dtype=F32) / ROPE_DIM)
    ang = positions.astype(F32).reshape(-1, 1) * inv_freq
    cos, sin = jnp.cos(ang), jnp.sin(ang)
    t = ang.shape[0]
    pad = DIFF_QK_DIM - ROPE_DIM
    cos_map = jnp.concatenate([cos, cos, jnp.ones((t, pad), F32)], axis=-1)
    lo_map = jnp.concatenate([-sin, jnp.zeros((t, half + pad), F32)], axis=-1)
    hi_map = jnp.concatenate([jnp.zeros((t, half), F32), sin, jnp.zeros((t, pad), F32)], axis=-1)
    reps = V7X_LANES // DIFF_QK_DIM
    return jnp.concatenate([jnp.tile(m, (1, reps)) for m in (cos_map, lo_map, hi_map)], axis=-1)


def _scan_masks():
    c, g, gc = SCAN_CHUNK, SCAN_GROUP_HEADS, SCAN_GROUP_CH
    row = lax.broadcasted_iota(jnp.int32, (g * c, gc), 0)
    col = lax.broadcasted_iota(jnp.int32, (g * c, gc), 1)
    same_block = (row // c) == (col // c)
    rowc = lax.broadcasted_iota(jnp.int32, (c, gc), 0)
    colc = lax.broadcasted_iota(jnp.int32, (c, gc), 1) % c
    strict_cat = rowc > colc
    incl_cat = rowc >= colc
    level0_cat = strict_cat & ((rowc // 2) == (colc // 2))
    eye_cat = jnp.where(rowc == colc, 1.0, 0.0).astype(F32)
    rt = lax.broadcasted_iota(jnp.int32, (c, c), 0)
    ct = lax.broadcasted_iota(jnp.int32, (c, c), 1)
    tril = jnp.where(rt >= ct, 1.0, 0.0).astype(BF16)
    return same_block, strict_cat, incl_cat, level0_cat, eye_cat, tril


def _scan_constants():
    idx = jnp.arange(SCAN_GROUP_HEADS * SCAN_CHUNK)
    t, s = idx[:, None], idx[None, :]
    mats = [t // 2 == s // 2]
    b = 2
    while b < SCAN_CHUNK:
        mats.append((t // (2 * b) == s // (2 * b)) & (t % (2 * b) >= b) & (s % (2 * b) < b))
        b *= 2
    mats.append(t // RWKV_HEAD_DIM == s // RWKV_HEAD_DIM)
    return jnp.stack(mats).astype(BF16)


def _sigmoid(x):
    return 0.5 * jnp.tanh(0.5 * x) + 0.5


def _head_sum(x, ones_bd, split=False):
    hi = x.astype(BF16)
    total = _dot(hi, ones_bd)
    if split:
        total = total + _dot((x - hi.astype(F32)).astype(BF16), ones_bd)
    return total


def _scan_chunk(ops, states, masks, const_ref):
    c, g, gc = SCAN_CHUNK, SCAN_GROUP_HEADS, SCAN_GROUP_CH
    same_block, strict_cat, incl_cat, level0_cat, eye_cat, tril = masks
    n_levels = const_ref.shape[0] - 1
    groups = range(len(ops))

    def tile(x):
        return jnp.concatenate([x] * g, axis=0)

    def bdexp(x):
        return jnp.where(same_block, tile(x), jnp.zeros_like(tile(x)))

    cums = []
    for lw, *_ in ops:
        lw_hi = lw.astype(BF16)
        lw_lo = (lw - lw_hi.astype(F32)).astype(BF16)
        cums.append(_dot(tril, lw_hi) + _dot(tril, lw_lo))

    a_b, b_b, k_b, r_b, end_b, v_b, w_end = [], [], [], [], [], [], []
    for (lw, r, k, v, kk, a), cum in zip(ops, cums):
        w_inc = jnp.exp(cum)
        w_inv = jnp.exp(-cum)
        w_prev = jnp.exp(cum - lw)
        w_last = w_inc[c - 1:c, :]
        b_f = kk * a * w_inv
        k_f = k * w_inv
        a_b.append((-(kk * w_prev)).astype(BF16))
        b_b.append(b_f.astype(BF16))
        k_b.append(k_f.astype(BF16))
        r_b.append((r * w_inc).astype(BF16))
        end_b.append(jnp.concatenate([(b_f * w_last).astype(BF16), (k_f * w_last).astype(BF16)], axis=0))
        v_b.append(v.astype(BF16))
        w_end.append(w_last)

    ar_b = [jnp.concatenate([a_b[i], r_b[i]], axis=0) for i in groups]
    l_cat = [_dot_nt(ar_b[i], jnp.concatenate([bdexp(b_b[i]), bdexp(k_b[i])], axis=0))
             for i in groups]
    l_ab = [jnp.where(strict_cat, lc[:c, :gc], 0.0).astype(BF16) for lc in l_cat]

    l_bd = [bdexp(l) for l in l_ab]
    t_cat = [eye_cat + jnp.where(level0_cat, l, jnp.zeros_like(l)).astype(F32) for l in l_ab]
    for lvl in range(1, n_levels):
        t_b = [t.astype(BF16) for t in t_cat]
        half = [_dot(t_b[i], l_bd[i] * const_ref[lvl]).astype(BF16) for i in groups]
        t_cat = [t_cat[i] + _dot(half[i], bdexp(t_b[i])) for i in groups]
    t_cat = [t.astype(BF16) for t in t_cat]

    l_rb = [jnp.where(incl_cat, lc[c:, :gc], 0.0).astype(BF16) for lc in l_cat]
    l_k = [jnp.concatenate([jnp.where(strict_cat, lc[:c, gc:], 0.0).astype(BF16),
                            jnp.where(incl_cat, lc[c:, gc:], 0.0).astype(BF16)], axis=0)
           for lc in l_cat]
    from_v = [_dot(l_k[i], bdexp(v_b[i])) for i in groups]

    from_state = [_dot_nt(ar_b[i], states[i].astype(BF16)) for i in groups]
    x_b = [(from_state[i][:c] + from_v[i][:c]).astype(BF16) for i in groups]
    u_b = [_dot(t_cat[i], bdexp(x_b[i])).astype(BF16) for i in groups]
    upd = [_dot_tn(jnp.concatenate([u_b[i], v_b[i]], axis=0), end_b[i]) for i in groups]
    new_states = [jnp.where(same_block, states[i] * w_end[i] + upd[i], 0.0) for i in groups]
    outs = [from_state[i][c:] + from_v[i][c:] + _dot(l_rb[i], bdexp(u_b[i])) for i in groups]
    return outs, new_states


_VEC_W0, _VEC_A0, _VEC_KK, _VEC_KA, _VEC_RK, _VEC_GNW, _VEC_GNB, _VEC_V0 = range(8)


def _rwkv_kernel(*refs, n_chunks, n_groups, has_vfirst):
    if has_vfirst:
        (const_ref, vec_ref, w2_ref, a2_ref, g2_ref, v2_ref, r_ref, k_ref, v_ref, lora_ref, vf_ref,
         o_ref, state_ref, lw_s, a_s, kk_s, k2_s, v_s, o_s) = refs
    else:
        (const_ref, vec_ref, w2_ref, a2_ref, g2_ref, r_ref, k_ref, v_ref, lora_ref,
         o_ref, state_ref, lw_s, a_s, kk_s, k2_s, v_s, o_s) = refs
        v2_ref = vf_ref = None

    @pl.when(pl.program_id(1) == 0)
    def _():
        state_ref[...] = jnp.zeros_like(state_ref)

    c, gc = SCAN_CHUNK, SCAN_GROUP_CH
    n_seqs = r_ref.shape[0]
    ones_bd = const_ref[const_ref.shape[0] - 1]
    vec = lambda row, sl: vec_ref[row:row + 1, sl]

    o1 = LORA_WA
    o2 = o1 + LORA_GM

    for bi in range(n_seqs):
        tanh_wl = jnp.tanh(lora_ref[bi, :, 0:o1]).astype(BF16)
        al = lora_ref[bi, :, 0:o1].astype(BF16)
        for gi in range(n_groups):
            sl = slice(gi * gc, (gi + 1) * gc)
            zw = vec(_VEC_W0, sl) + _dot(tanh_wl, w2_ref[:, sl])
            lw_s[bi, :, sl] = -math.exp(-0.5) * _sigmoid(zw)
            a = _sigmoid(vec(_VEC_A0, sl) + _dot(al, a2_ref[:, sl]))
            a_s[bi, :, sl] = a
            v = v_ref[bi, :, sl]
            if has_vfirst:
                mix = _sigmoid(vec(_VEC_V0, sl)
                               + _dot(lora_ref[bi, :, o1:o2].astype(BF16), v2_ref[:, sl]))
                v = v + (vf_ref[bi, :, sl] - v) * mix
            v_s[bi, :, sl] = v
            k = k_ref[bi, :, sl]
            kk = k * vec(_VEC_KK, sl)
            ss = _head_sum(kk * kk, ones_bd)
            kk_s[bi, :, sl] = kk * lax.rsqrt(jnp.maximum(ss, 1e-24))
            k2_s[bi, :, sl] = k * (1.0 + (a - 1.0) * vec(_VEC_KA, sl))

    masks = _scan_masks()
    problems = [(bi, gi) for bi in range(n_seqs) for gi in range(n_groups)]

    def body(ci, carry):
        t0 = pl.multiple_of(ci * c, c)
        sls = [(bi, pl.ds(t0, c), slice(gi * gc, (gi + 1) * gc)) for bi, gi in problems]
        ops = [(lw_s[sl], r_ref[sl], k2_s[sl], v_s[sl], kk_s[sl], a_s[sl]) for sl in sls]
        outs, new_states = _scan_chunk(ops, [state_ref[pi] for pi in range(len(problems))],
                                       masks, const_ref)
        for pi in range(len(problems)):
            o_s[sls[pi]] = outs[pi]
            state_ref[pi] = new_states[pi]
        return carry

    lax.fori_loop(0, n_chunks, body, 0)

    inv_n = 1.0 / RWKV_HEAD_DIM
    for bi in range(n_seqs):
        sig_gl = _sigmoid(lora_ref[bi, :, o1:o2]).astype(BF16)
        for gi in range(n_groups):
            sl = slice(gi * gc, (gi + 1) * gc)
            o = o_s[bi, :, sl]
            dev = o - _head_sum(o, ones_bd) * inv_n
            var = _head_sum(dev * dev, ones_bd) * inv_n
            normed = dev * lax.rsqrt(var + GN_EPS) * vec(_VEC_GNW, sl) + vec(_VEC_GNB, sl)
            bonus = _head_sum(r_ref[bi, :, sl] * k2_s[bi, :, sl] * vec(_VEC_RK, sl), ones_bd,
                              split=True) * v_s[bi, :, sl]
            gate = _dot(sig_gl, g2_ref[:, sl])
            o_ref[bi, :, sl] = ((normed + bonus) * gate).astype(o_ref.dtype)


def rwkv7_mixer(pr, pr_first, vecs, w2p, a2p, g2p, v2p, batch, seq, width):
    t, n = pr.shape
    tb = _pick(seq, (256, 128, 64))
    nb = seq // tb
    n_seqs = 2 if batch % 2 == 0 else 1
    n_groups = width // SCAN_GROUP_CH
    has_vfirst = pr_first is not None
    consts = _scan_constants()
    lora_col = 3 * width // LORA_BLOCK
    row_spec = lambda col: pl.BlockSpec((n_seqs, tb, width), lambda b, s: (b, s, col))
    full = lambda arr: pl.BlockSpec(arr.shape, lambda b, s: (0,) * arr.ndim)
    weights = [w2p, a2p, g2p] + ([v2p] if has_vfirst else [])
    in_specs = ([full(consts), full(vecs)] + [full(w) for w in weights]
                + [row_spec(0), row_spec(1), row_spec(2),
                   pl.BlockSpec((n_seqs, tb, LORA_BLOCK), lambda b, s: (b, s, lora_col))]
                + ([row_spec(2)] if has_vfirst else []))
    pr3 = pr.reshape(batch, seq, n)
    args = ([consts, vecs] + weights + [pr3, pr3, pr3, pr3]
            + ([pr_first.reshape(batch, seq, n)] if has_vfirst else []))
    out = pl.pallas_call(
        functools.partial(_rwkv_kernel, n_chunks=tb // SCAN_CHUNK, n_groups=n_groups,
                          has_vfirst=has_vfirst),
        grid=(batch // n_seqs, nb),
        in_specs=in_specs,
        out_specs=pl.BlockSpec((n_seqs, tb, width), lambda b, s: (b, s, 0)),
        out_shape=jax.ShapeDtypeStruct((batch, seq, width), BF16),
        scratch_shapes=([pltpu.VMEM((n_seqs * n_groups, SCAN_GROUP_CH, SCAN_GROUP_CH), F32)]
                        + [pltpu.VMEM((n_seqs, tb, width), F32)] * 6),
        compiler_params=_params(("parallel", "arbitrary")),
        name="rwkv7_mixer",
    )(*args)
    return out.reshape(t, width)


def _diff_attn_kernel(lam_ref, q_ref, k_ref, v_ref, w_ref, o_ref,
                      vt_ref, m_ref, acc_ref, s0_ref, s1_ref, *, tq, scale_out):
    qi = pl.program_id(2)
    n_heads = q_ref.shape[1] // DIFF_V_DIM

    @pl.when(qi == 0)
    def _():
        for hh in range(n_heads):
            r0 = hh * ATTN_VT_ROWS
            for c0 in range(0, v_ref.shape[0], tq):
                vt_ref[r0:r0 + DIFF_V_DIM, c0:c0 + tq] = (
                    v_ref[c0:c0 + tq, hh * DIFF_V_DIM:(hh + 1) * DIFF_V_DIM].astype(F32).T.astype(BF16))
            vt_ref[r0 + DIFF_V_DIM:r0 + ATTN_VT_ROWS, :] = jnp.ones(
                (ATTN_VT_ROWS - DIFF_V_DIM, vt_ref.shape[1]), BF16)

    lane = lax.broadcasted_iota(jnp.int32, (tq, DIFF_V_DIM), 1)
    q_streams = []
    for hh in range(n_heads):
        q = q_ref[:, hh * DIFF_V_DIM:(hh + 1) * DIFF_V_DIM]
        zero = jnp.zeros_like(q)
        q_streams.append((hh, jnp.where(lane < DIFF_QK_DIM, q, zero)))
        q_streams.append((hh, jnp.where(lane >= DIFF_QK_DIM, q, zero)))
    n_streams = len(q_streams)

    m_ref[...] = jnp.full_like(m_ref, -jnp.inf)
    acc_ref[...] = jnp.zeros_like(acc_ref)

    tk = tq // 2
    s_bufs = (s0_ref, s1_ref)

    every_query = slice(0, tq)
    late_queries = slice(tk, tq)

    def scores(j, slot, cols=every_query):
        k0 = pl.multiple_of(j * tk, tk)
        for si, (hh, qm) in enumerate(q_streams):
            kb = k_ref[pl.ds(k0, tk), hh * DIFF_V_DIM:(hh + 1) * DIFF_V_DIM]
            s_bufs[slot][si, :, cols] = _dot_nt(kb, qm[cols])

    def softmax_pv(j, slot, diag=False, cols=every_query):
        k0 = pl.multiple_of(j * tk, tk)
        for si, (hh, _) in enumerate(q_streams):
            s = s_bufs[slot][si, :, cols]
            if diag:
                kpos = lax.broadcasted_iota(jnp.int32, s.shape, 0)
                qpos = lax.broadcasted_iota(jnp.int32, s.shape, 1)
                s = jnp.where(kpos <= qpos, s, -jnp.inf)
            m_old = m_ref[si, :, cols]
            m_new = jnp.maximum(m_old, jnp.max(s, axis=0, keepdims=True))
            alpha = jnp.exp2(m_old - m_new)
            p = jnp.exp2(s - m_new)
            m_ref[si, :, cols] = m_new
            vtb = vt_ref[hh * ATTN_VT_ROWS:(hh + 1) * ATTN_VT_ROWS, pl.ds(k0, tk)]
            acc_ref[si, :, cols] = acc_ref[si, :, cols] * alpha + _dot(vtb, p.astype(BF16))

    scores(0, 0)

    def body(jj, carry):
        scores(2 * jj + 1, 1)
        softmax_pv(2 * jj, 0)
        scores(2 * jj + 2, 0)
        softmax_pv(2 * jj + 1, 1)
        return carry

    lax.fori_loop(0, qi, body, 0)
    scores(2 * qi + 1, 1, cols=late_queries)
    softmax_pv(2 * qi, 0, diag=True)
    softmax_pv(2 * qi + 1, 1, diag=True, cols=late_queries)

    lam = lam_ref[0]
    for hh in range(n_heads):
        a1, a2 = acc_ref[2 * hh], acc_ref[2 * hh + 1]
        nv = DIFF_V_DIM
        o_t = a1[:nv] * (1.0 / a1[nv:nv + 1]) - a2[:nv] * (lam / a2[nv:nv + 1])
        o = o_t.T
        ms = jnp.mean(o * o, axis=-1, keepdims=True)
        o_ref[:, hh * DIFF_V_DIM:(hh + 1) * DIFF_V_DIM] = (
            o * lax.rsqrt(ms + SUBLN_EPS) * w_ref[...] * scale_out).astype(o_ref.dtype)


def diff_attention(qkv, lam, subln_w, lambda_init, batch, seq, heads):
    t = qkv.shape[0]
    tq = _pick(seq, (1024, 512, 256, 128))
    nq = seq // tq
    hb = 2 if heads % 2 == 0 else 1
    wb = hb * DIFF_V_DIM
    n_hb = heads // hb
    return pl.pallas_call(
        functools.partial(_diff_attn_kernel, tq=tq, scale_out=1.0 - lambda_init),
        grid=(batch, n_hb, nq),
        in_specs=[
            pl.BlockSpec(memory_space=pltpu.SMEM),
            pl.BlockSpec((tq, wb), lambda b, h, i: (b * nq + i, h)),
            pl.BlockSpec((seq, wb), lambda b, h, i: (b, n_hb + h)),
            pl.BlockSpec((seq, wb), lambda b, h, i: (b, 2 * n_hb + h)),
            pl.BlockSpec((1, DIFF_V_DIM), lambda b, h, i: (0, 0)),
        ],
        out_specs=pl.BlockSpec((tq, wb), lambda b, h, i: (b * nq + i, h)),
        out_shape=jax.ShapeDtypeStruct((t, heads * DIFF_V_DIM), BF16),
        scratch_shapes=[
            pltpu.VMEM((hb * ATTN_VT_ROWS, seq), BF16),
            pltpu.VMEM((2 * hb, 1, tq), F32),
            pltpu.VMEM((2 * hb, ATTN_VT_ROWS, tq), F32),
            pltpu.VMEM((2 * hb, tq // 2, tq), F32),
            pltpu.VMEM((2 * hb, tq // 2, tq), F32),
        ],
        compiler_params=_params(("parallel", "parallel", "arbitrary")),
        name="diff_attention",
    )(lam.reshape(1).astype(F32), qkv, qkv, qkv, subln_w.reshape(1, DIFF_V_DIM).astype(F32))


def _residual_norm_epilogue(m, g_ref, res_ref, gn_ref, x_ref, h_ref, rows=slice(None)):
    ms = jnp.mean(m * m, axis=-1, keepdims=True)
    x = res_ref[rows, :] + m * lax.rsqrt(ms + NORM_EPS) * g_ref[...]
    x_ref[rows, :] = x
    if h_ref is not None:
        ms_x = jnp.mean(x * x, axis=-1, keepdims=True)
        h_ref[rows, :] = (x * lax.rsqrt(ms_x + NORM_EPS) * gn_ref[...]).astype(h_ref.dtype)


def _out_proj_kernel(a1_ref, a2_ref, w_ref, g_ref, res_ref, gn_ref, x_ref, h_ref):
    k1 = a1_ref.shape[1]
    tm = a1_ref.shape[0]
    for r0 in range(0, tm, V7X_MXU_DIM):
        rows = slice(r0, min(r0 + V7X_MXU_DIM, tm))
        m = _dot(a1_ref[rows, :], w_ref[0:k1, :]) + _dot(a2_ref[rows, :], w_ref[k1:, :])
        _residual_norm_epilogue(m, g_ref, res_ref, gn_ref, x_ref, h_ref, rows)


def out_proj(a1, a2, w, layer, g, res, g_next):
    t, k1 = a1.shape
    k2 = a2.shape[1]
    d = w.shape[2]
    tm = _pick(t, (512, 256, 128))
    row = pl.BlockSpec((tm, d), lambda i: (i, 0))
    vec = pl.BlockSpec((1, d), lambda i: (0, 0))
    return pl.pallas_call(
        _out_proj_kernel,
        grid=(t // tm,),
        in_specs=[
            pl.BlockSpec((tm, k1), lambda i: (i, 0)),
            pl.BlockSpec((tm, k2), lambda i: (i, 0)),
            pl.BlockSpec((None, k1 + k2, d), lambda i: (layer, 0, 0)),
            vec, row, vec,
        ],
        out_specs=[row, row],
        out_shape=[jax.ShapeDtypeStruct((t, d), F32), jax.ShapeDtypeStruct((t, d), BF16)],
        compiler_params=_params(("parallel",)),
        name="out_proj",
    )(a1, a2, w, g.reshape(1, d).astype(F32), res, g_next.reshape(1, d).astype(F32))


def _ffn_up_kernel(h_ref, wg_ref, wu_ref, cw_ref, o_ref, carry_ref, *, tiles_per_seq):
    i, j = pl.program_id(0), pl.program_id(1)

    @pl.when(i % tiles_per_seq == 0)
    def _():
        carry_ref[j] = jnp.zeros(carry_ref.shape[1:], F32)

    h = h_ref[...]
    tm = h.shape[0]
    chunks = [slice(c0, c0 + V7X_MXU_DIM) for c0 in range(0, o_ref.shape[1], V7X_MXU_DIM)]
    gates = [_dot(h, wg_ref[:, cs].astype(BF16)) for cs in chunks]
    for cs, gate in zip(chunks, gates):
        carry = carry_ref[j, :, cs]
        conv = (_rows_from_prev(gate, 2, carry) * cw_ref[0:1, cs]
                + _rows_from_prev(gate, 1, carry) * cw_ref[1:2, cs]
                + gate * cw_ref[2:3, cs] + cw_ref[3:4, cs])
        carry_ref[j, :, cs] = gate[tm - V7X_SUBLANES:, :]
        inner = math.sqrt(2.0 / math.pi) * (conv + 0.044715 * (conv * conv * conv))
        act = 0.5 * conv * (1.0 + jnp.tanh(inner))
        o_ref[:, cs] = (act * _dot(h, wu_ref[:, cs].astype(BF16))).astype(o_ref.dtype)


def ffn_up(h, w_up, layer, conv_wb, seq):
    t, d = h.shape
    f = w_up.shape[2] // 2
    tm = _pick(seq, (1024, 512, 256, 128))
    tn = _pick(f, (512, 256))
    nj = f // tn
    return pl.pallas_call(
        functools.partial(_ffn_up_kernel, tiles_per_seq=seq // tm),
        grid=(t // tm, nj),
        in_specs=[
            pl.BlockSpec((tm, d), lambda i, j: (i, 0)),
            pl.BlockSpec((None, d, tn), lambda i, j: (layer, 0, j)),
            pl.BlockSpec((None, d, tn), lambda i, j: (layer, 0, j + nj)),
            pl.BlockSpec((CONV_WIDTH + 1, tn), lambda i, j: (0, j)),
        ],
        out_specs=pl.BlockSpec((tm, tn), lambda i, j: (i, j)),
        out_shape=jax.ShapeDtypeStruct((t, f), BF16),
        scratch_shapes=[pltpu.VMEM((nj, V7X_SUBLANES, tn), F32)],
        compiler_params=_params(("arbitrary", "arbitrary")),
        name="ffn_up",
    )(h, w_up, w_up, conv_wb)


def _ffn_down_kernel(a_ref, w_ref, g_ref, res_ref, gn_ref, x_ref, *maybe_h_ref):
    m = _dot(a_ref[...], w_ref[...])
    _residual_norm_epilogue(m, g_ref, res_ref, gn_ref, x_ref, maybe_h_ref[0] if maybe_h_ref else None)


def ffn_down(a, w, layer, g, res, g_next):
    t, kdim = a.shape
    d = w.shape[2]
    tm = _pick(t, (256, 128))
    row = pl.BlockSpec((tm, d), lambda i: (i, 0))
    vec = pl.BlockSpec((1, d), lambda i: (0, 0))
    has_next = g_next is not None
    gn = (g_next if has_next else g).reshape(1, d).astype(F32)
    outs = pl.pallas_call(
        _ffn_down_kernel,
        grid=(t // tm,),
        in_specs=[
            pl.BlockSpec((tm, kdim), lambda i: (i, 0)),
            pl.BlockSpec((None, kdim, d), lambda i: (layer, 0, 0), pipeline_mode=pl.Buffered(1)),
            vec, row, vec,
        ],
        out_specs=[row, row] if has_next else [row],
        out_shape=([jax.ShapeDtypeStruct((t, d), F32)]
                   + ([jax.ShapeDtypeStruct((t, d), BF16)] if has_next else [])),
        compiler_params=_params(("parallel",)),
        name="ffn_down",
    )(a, w, g.reshape(1, d).astype(F32), res, gn)
    return (outs[0], outs[1]) if has_next else (outs[0], None)


def _pad_cols(w, width):
    return jnp.pad(w, [(0, 0)] * (w.ndim - 1) + [(0, width - w.shape[-1])])


def kernel(x, positions, pre_mix_norm, post_mix_norm, pre_ffn_norm, post_ffn_norm, w_in, w_mv_down, shift_mu, shift_mu_mv, w0, w2, a0, a2, g2, k_k, k_a, r_k, gn_w, gn_b, v0, v2, lam_q1, lam_k1, lam_q2, lam_k2, subln_w, w_out, w_up, conv_w, conv_b, w_down):
    batch, seq, d_model = x.shape
    t = batch * seq
    depth = w_in.shape[0]
    width = w0.shape[1]
    diff_heads = (d_model - width) // DIFF_V_DIM
    rwkv_cols = shift_mu.shape[1]
    qk_cols = diff_heads * 2 * DIFF_QK_DIM
    rwkv_n = 3 * width + LORA_BLOCK
    assert rwkv_n % IN_RWKV_TN == 0 and (3 * width) % LORA_BLOCK == 0

    rope = _rope_tables(positions)
    w_out_b = w_out.astype(BF16)
    w_down_b = w_down.astype(BF16)
    xf = x.reshape(t, d_model)
    h = rms_norm_bf16(xf, pre_mix_norm[0])
    pr_first = None
    for l in range(depth):
        o_r, o_wl = 0, width
        o_k = o_wl + DECAY_LORA
        o_v = o_k + width
        o_al = o_v + width
        o_gl = o_al + AAA_LORA
        if l == 0:
            mv_w = jnp.zeros((d_model, MV_LORA), F32)
            mv_mu = jnp.zeros((MV_LORA,), F32)
        else:
            mv_w = w_mv_down[l - 1]
            mv_mu = shift_mu_mv[l - 1]
        arrange = lambda m, mv: jnp.concatenate(
            [m[..., o_r:o_wl], m[..., o_k:o_v], m[..., o_v:o_al],
             m[..., o_wl:o_k], m[..., o_al:o_gl],
             _pad_cols(jnp.concatenate([m[..., o_gl:rwkv_cols], mv], axis=-1), LORA_GM)], axis=-1)
        w_rwkv = _pad_cols(arrange(w_in[l], mv_w), rwkv_n).astype(BF16)
        mu_rwkv = _pad_cols(arrange(shift_mu[l], mv_mu), rwkv_n)
        pr = in_proj_rwkv(h, w_rwkv, mu_rwkv, seq)

        vecs = jnp.stack([w0[l], a0[l], k_k[l], k_a[l], r_k[l].reshape(width), gn_w[l], gn_b[l],
                          v0[l - 1] if l > 0 else jnp.zeros((width,), F32)]).astype(F32)
        lora_rows = lambda w, start, rows: jnp.pad(
            w, ((start, rows - start - w.shape[0]), (0, 0))).astype(BF16)
        w2p = lora_rows(w2[l], 0, LORA_WA)
        a2p = lora_rows(a2[l], DECAY_LORA, LORA_WA)
        g2p = lora_rows(g2[l], 0, LORA_GM)
        v2p = lora_rows(v2[l - 1], GATE_LORA, LORA_GM) if l > 0 else None
        o_rwkv = rwkv7_mixer(pr, pr_first, vecs, w2p, a2p, g2p, v2p, batch, seq, width)
        if l == 0:
            pr_first = pr

        qkv = in_proj_attn(h, w_in[l][:, rwkv_cols:].astype(BF16), rope, seq, qk_cols)
        lambda_init = 0.8 - 0.6 * math.exp(-0.3 * l)
        lam = (jnp.exp(jnp.sum(lam_q1[l] * lam_k1[l])) - jnp.exp(jnp.sum(lam_q2[l] * lam_k2[l]))
               + lambda_init)
        o_diff = diff_attention(qkv, lam, subln_w[l], lambda_init, batch, seq, diff_heads)

        xf, h = out_proj(o_rwkv, o_diff, w_out_b, l, post_mix_norm[l], xf, pre_ffn_norm[l])

        conv_wb = jnp.concatenate([conv_w[l], conv_b[l][None, :]], axis=0).astype(F32)
        act = ffn_up(h, w_up, l, conv_wb, seq)
        g_next = pre_mix_norm[l + 1] if l + 1 < depth else None
        xf, h = ffn_down(act, w_down_b, l, post_ffn_norm[l], xf, g_next)
    return xf.reshape(batch, seq, d_model)
```

```python
import functools
import math

import jax
import jax.numpy as jnp
from jax import lax
from jax.experimental import pallas as pl
from jax.experimental.pallas import tpu as pltpu

F32 = jnp.float32
BF16 = jnp.bfloat16

RWKV_HEAD_DIM = 64
DIFF_QK_DIM = 64
DIFF_V_DIM = 128
DECAY_LORA = 64
AAA_LORA = 64
MV_LORA = 32
GATE_LORA = 160
CONV_WIDTH = 3
ROPE_THETA = 500000.0
ROPE_DIM = DIFF_QK_DIM // 4
NORM_EPS = 1e-6
GN_EPS = 64e-5
SUBLN_EPS = 1e-5

V7X_LANES = 128
V7X_SUBLANES = 8
V7X_MXU_DIM = 256
V7X_VMEM_LIMIT_BYTES = 56 * 1024 * 1024

SCAN_CHUNK = 64
SCAN_GROUP_HEADS = V7X_MXU_DIM // RWKV_HEAD_DIM
SCAN_GROUP_CH = SCAN_GROUP_HEADS * RWKV_HEAD_DIM
assert SCAN_CHUNK == RWKV_HEAD_DIM

LORA_WA = 128
LORA_GM = 256
LORA_BLOCK = 512
IN_RWKV_TN = 1792
IN_ATTN_TN = 1536
ATTN_VT_ROWS = DIFF_V_DIM + 16


def _pick(n, candidates):
    for c in candidates:
        if n % c == 0:
            return c
    return n


def _params(semantics):
    return pltpu.CompilerParams(dimension_semantics=semantics,
                                vmem_limit_bytes=V7X_VMEM_LIMIT_BYTES)


def _dot(a, b):
    return jnp.dot(a, b, preferred_element_type=F32)


def _dot_nt(a, b):
    return lax.dot_general(a, b, (((1,), (1,)), ((), ())), preferred_element_type=F32)


def _dot_tn(a, b):
    return lax.dot_general(a, b, (((0,), (0,)), ((), ())), preferred_element_type=F32)


def _rms_normed(x_ref, g_ref):
    x = x_ref[...]
    ms = jnp.mean(x * x, axis=-1, keepdims=True)
    return (x * lax.rsqrt(ms + NORM_EPS) * g_ref[...]).astype(BF16)


def _rows_from_prev(x, n, carry):
    rolled = pltpu.roll(x, n, 0)
    row = lax.broadcasted_iota(jnp.int32, x.shape, 0)
    for r in range(n):
        src = V7X_SUBLANES - n + r
        rolled = jnp.where(row == r, carry[src:src + 1, :], rolled)
    return rolled


def _rms_norm_kernel(x_ref, g_ref, o_ref):
    o_ref[...] = _rms_normed(x_ref, g_ref)


def rms_norm_bf16(x, g):
    t, d = x.shape
    tm = _pick(t, (1024, 512, 256, 128))
    return pl.pallas_call(
        _rms_norm_kernel,
        grid=(t // tm,),
        in_specs=[pl.BlockSpec((tm, d), lambda i: (i, 0)), pl.BlockSpec((1, d), lambda i: (0, 0))],
        out_specs=pl.BlockSpec((tm, d), lambda i: (i, 0)),
        out_shape=jax.ShapeDtypeStruct((t, d), BF16),
        compiler_params=_params(("parallel",)),
        name="rms_norm",
    )(x, g.reshape(1, d).astype(F32))


def _in_rwkv_kernel(h_ref, w_ref, mu_ref, o_ref, carry_ref, *, tiles_per_seq):
    i, j = pl.program_id(0), pl.program_id(1)

    @pl.when(i % tiles_per_seq == 0)
    def _():
        carry_ref[j] = jnp.zeros(carry_ref.shape[1:], F32)

    h = h_ref[...]
    tm = h.shape[0]
    for c0 in range(0, o_ref.shape[1], V7X_MXU_DIM):
        cs = slice(c0, c0 + V7X_MXU_DIM)
        p = _dot(h, w_ref[:, cs])
        prev = _rows_from_prev(p, 1, carry_ref[j, :, cs])
        o_ref[:, cs] = p + mu_ref[:, cs] * (prev - p)
        carry_ref[j, :, cs] = p[tm - V7X_SUBLANES:, :]


def in_proj_rwkv(h, w, mu, seq):
    t, d = h.shape
    n = w.shape[1]
    tm = _pick(seq, (1024, 512, 256, 128))
    tn = IN_RWKV_TN
    nj = n // tn
    return pl.pallas_call(
        functools.partial(_in_rwkv_kernel, tiles_per_seq=seq // tm),
        grid=(t // tm, nj),
        in_specs=[
            pl.BlockSpec((tm, d), lambda i, j: (i, 0)),
            pl.BlockSpec((d, tn), lambda i, j: (0, j)),
            pl.BlockSpec((1, tn), lambda i, j: (0, j)),
        ],
        out_specs=pl.BlockSpec((tm, tn), lambda i, j: (i, j)),
        out_shape=jax.ShapeDtypeStruct((t, n), F32),
        scratch_shapes=[pltpu.VMEM((nj, V7X_SUBLANES, tn), F32)],
        compiler_params=_params(("arbitrary", "arbitrary")),
        name="in_proj_rwkv",
    )(h, w, mu.reshape(1, n).astype(F32))


def _in_attn_kernel(h_ref, w_ref, rope_ref, o_ref, *, qk_cols, n_tiles):
    j = pl.program_id(1)
    tn = o_ref.shape[1]
    half = ROPE_DIM // 2
    q_scale = DIFF_QK_DIM ** -0.5 * math.log2(math.e)

    def tile(jt):
        h = h_ref[...]
        tables = {}
        for c0 in range(0, tn, V7X_MXU_DIM):
            col = jt * tn + c0
            p = _dot(h, w_ref[:, c0:c0 + V7X_MXU_DIM])
            if col >= 2 * qk_cols:
                o_ref[:, c0:c0 + V7X_MXU_DIM] = p.astype(o_ref.dtype)
                continue
            scale = q_scale if col < qk_cols else 1.0
            if scale not in tables:
                tables[scale] = [rope_ref[:, k * V7X_LANES:(k + 1) * V7X_LANES] * scale
                                 for k in range(3)]
            cos, sin_lo, sin_hi = tables[scale]
            for g0 in range(0, V7X_MXU_DIM, V7X_LANES):
                xg = p[:, g0:g0 + V7X_LANES]
                rot = (xg * cos + pltpu.roll(xg, V7X_LANES - half, 1) * sin_lo
                       + pltpu.roll(xg, half, 1) * sin_hi)
                o_ref[:, c0 + g0:c0 + g0 + V7X_LANES] = rot.astype(o_ref.dtype)

    for jt in range(n_tiles):
        pl.when(j == jt)(functools.partial(tile, jt))


def in_proj_attn(h, w, rope, seq, qk_cols):
    t, d = h.shape
    n = w.shape[1]
    tm = _pick(seq, (1024, 512, 256, 128))
    tn = IN_ATTN_TN
    assert qk_cols % V7X_MXU_DIM == 0 and n % tn == 0 and tn % V7X_MXU_DIM == 0
    return pl.pallas_call(
        functools.partial(_in_attn_kernel, qk_cols=qk_cols, n_tiles=n // tn),
        grid=(t // tm, n // tn),
        in_specs=[
            pl.BlockSpec((tm, d), lambda i, j: (i, 0)),
            pl.BlockSpec((d, tn), lambda i, j: (0, j)),
            pl.BlockSpec((tm, 3 * V7X_LANES), lambda i, j: (i, 0)),
        ],
        out_specs=pl.BlockSpec((tm, tn), lambda i, j: (i, j)),
        out_shape=jax.ShapeDtypeStruct((t, n), BF16),
        compiler_params=_params(("parallel", "arbitrary")),
        name="in_proj_attn",
    )(h, w, rope)


def _rope_tables(positions):
    half = ROPE_DIM // 2
    inv_freq = ROPE_THETA ** (-jnp.arange(0, ROPE_DIM, 2, dtype=F32) / ROPE_DIM)
    ang = positions.astype(F32).reshape(-1, 1) * inv_freq
    cos, sin = jnp.cos(ang), jnp.sin(ang)
    t = ang.shape[0]
    pad = DIFF_QK_DIM - ROPE_DIM
    cos_map = jnp.concatenate([cos, cos, jnp.ones((t, pad), F32)], axis=-1)
    lo_map = jnp.concatenate([-sin, jnp.zeros((t, half + pad), F32)], axis=-1)
    hi_map = jnp.concatenate([jnp.zeros((t, half), F32), sin, jnp.zeros((t, pad), F32)], axis=-1)
    reps = V7X_LANES // DIFF_QK_DIM
    return jnp.concatenate([jnp.tile(m, (1, reps)) for m in (cos_map, lo_map, hi_map)], axis=-1)


def _scan_masks():
    c, g, gc = SCAN_CHUNK, SCAN_GROUP_HEADS, SCAN_GROUP_CH
    row = lax.broadcasted_iota(jnp.int32, (g * c, gc), 0)
    col = lax.broadcasted_iota(jnp.int32, (g * c, gc), 1)
    same_block = (row // c) == (col // c)
    rowc = lax.broadcasted_iota(jnp.int32, (c, gc), 0)
    colc = lax.broadcasted_iota(jnp.int32, (c, gc), 1) % c
    strict_cat = rowc > colc
    incl_cat = rowc >= colc
    level0_cat = strict_cat & ((rowc // 2) == (colc // 2))
    eye_cat = jnp.where(rowc == colc, 1.0, 0.0).astype(F32)
    rt = lax.broadcasted_iota(jnp.int32, (c, c), 0)
    ct = lax.broadcasted_iota(jnp.int32, (c, c), 1)
    tril = jnp.where(rt >= ct, 1.0, 0.0).astype(BF16)
    return same_block, strict_cat, incl_cat, level0_cat, eye_cat, tril


def _scan_constants():
    idx = jnp.arange(SCAN_GROUP_HEADS * SCAN_CHUNK)
    t, s = idx[:, None], idx[None, :]
    mats = [t // 2 == s // 2]
    b = 2
    while b < SCAN_CHUNK:
        mats.append((t // (2 * b) == s // (2 * b)) & (t % (2 * b) >= b) & (s % (2 * b) < b))
        b *= 2
    mats.append(t // RWKV_HEAD_DIM == s // RWKV_HEAD_DIM)
    return jnp.stack(mats).astype(BF16)


def _sigmoid(x):
    return 0.5 * jnp.tanh(0.5 * x) + 0.5


def _head_sum(x, ones_bd, split=False):
    hi = x.astype(BF16)
    total = _dot(hi, ones_bd)
    if split:
        total = total + _dot((x - hi.astype(F32)).astype(BF16), ones_bd)
    return total


def _scan_chunk(ops, states, masks, const_ref):
    c, g, gc = SCAN_CHUNK, SCAN_GROUP_HEADS, SCAN_GROUP_CH
    same_block, strict_cat, incl_cat, level0_cat, eye_cat, tril = masks
    n_levels = const_ref.shape[0] - 1
    groups = range(len(ops))

    def tile(x):
        return jnp.concatenate([x] * g, axis=0)

    def bdexp(x):
        return jnp.where(same_block, tile(x), jnp.zeros_like(tile(x)))

    cums = []
    for lw, *_ in ops:
        lw_hi = lw.astype(BF16)
        lw_lo = (lw - lw_hi.astype(F32)).astype(BF16)
        cums.append(_dot(tril, lw_hi) + _dot(tril, lw_lo))

    a_b, b_b, k_b, r_b, end_b, v_b, w_end = [], [], [], [], [], [], []
    for (lw, r, k, v, kk, a), cum in zip(ops, cums):
        w_inc = jnp.exp(cum)
        w_inv = jnp.exp(-cum)
        w_prev = jnp.exp(cum - lw)
        w_last = w_inc[c - 1:c, :]
        b_f = kk * a * w_inv
        k_f = k * w_inv
        a_b.append((-(kk * w_prev)).astype(BF16))
        b_b.append(b_f.astype(BF16))
        k_b.append(k_f.astype(BF16))
        r_b.append((r * w_inc).astype(BF16))
        end_b.append(jnp.concatenate([(b_f * w_last).astype(BF16), (k_f * w_last).astype(BF16)], axis=0))
        v_b.append(v.astype(BF16))
        w_end.append(w_last)

    ar_b = [jnp.concatenate([a_b[i], r_b[i]], axis=0) for i in groups]
    l_cat = [_dot_nt(ar_b[i], jnp.concatenate([bdexp(b_b[i]), bdexp(k_b[i])], axis=0))
             for i in groups]
    l_ab = [jnp.where(strict_cat, lc[:c, :gc], 0.0).astype(BF16) for lc in l_cat]

    l_bd = [bdexp(l) for l in l_ab]
    t_cat = [eye_cat + jnp.where(level0_cat, l, jnp.zeros_like(l)).astype(F32) for l in l_ab]
    for lvl in range(1, n_levels):
        t_b = [t.astype(BF16) for t in t_cat]
        half = [_dot(t_b[i], l_bd[i] * const_ref[lvl]).astype(BF16) for i in groups]
        t_cat = [t_cat[i] + _dot(half[i], bdexp(t_b[i])) for i in groups]
    t_cat = [t.astype(BF16) for t in t_cat]

    l_rb = [jnp.where(incl_cat, lc[c:, :gc], 0.0).astype(BF16) for lc in l_cat]
    l_k = [jnp.concatenate([jnp.where(strict_cat, lc[:c, gc:], 0.0).astype(BF16),
                            jnp.where(incl_cat, lc[c:, gc:], 0.0).astype(BF16)], axis=0)
           for lc in l_cat]
    from_v = [_dot(l_k[i], bdexp(v_b[i])) for i in groups]

    from_state = [_dot_nt(ar_b[i], states[i].astype(BF16)) for i in groups]
    x_b = [(from_state[i][:c] + from_v[i][:c]).astype(BF16) for i in groups]
    u_b = [_dot(t_cat[i], bdexp(x_b[i])).astype(BF16) for i in groups]
    upd = [_dot_tn(jnp.concatenate([u_b[i], v_b[i]], axis=0), end_b[i]) for i in groups]
    new_states = [jnp.where(same_block, states[i] * w_end[i] + upd[i], 0.0) for i in groups]
    outs = [from_state[i][c:] + from_v[i][c:] + _dot(l_rb[i], bdexp(u_b[i])) for i in groups]
    return outs, new_states


_VEC_W0, _VEC_A0, _VEC_KK, _VEC_KA, _VEC_RK, _VEC_GNW, _VEC_GNB, _VEC_V0 = range(8)


def _rwkv_kernel(*refs, n_chunks, n_groups, has_vfirst):
    if has_vfirst:
        (const_ref, vec_ref, w2_ref, a2_ref, g2_ref, v2_ref, r_ref, k_ref, v_ref, lora_ref, vf_ref,
         o_ref, state_ref, lw_s, a_s, kk_s, k2_s, v_s, o_s) = refs
    else:
        (const_ref, vec_ref, w2_ref, a2_ref, g2_ref, r_ref, k_ref, v_ref, lora_ref,
         o_ref, state_ref, lw_s, a_s, kk_s, k2_s, v_s, o_s) = refs
        v2_ref = vf_ref = None

    @pl.when(pl.program_id(1) == 0)
    def _():
        state_ref[...] = jnp.zeros_like(state_ref)

    c, gc = SCAN_CHUNK, SCAN_GROUP_CH
    n_seqs = r_ref.shape[0]
    ones_bd = const_ref[const_ref.shape[0] - 1]
    vec = lambda row, sl: vec_ref[row:row + 1, sl]

    o1 = LORA_WA
    o2 = o1 + LORA_GM

    for bi in range(n_seqs):
        tanh_wl = jnp.tanh(lora_ref[bi, :, 0:o1]).astype(BF16)
        al = lora_ref[bi, :, 0:o1].astype(BF16)
        for gi in range(n_groups):
            sl = slice(gi * gc, (gi + 1) * gc)
            zw = vec(_VEC_W0, sl) + _dot(tanh_wl, w2_ref[:, sl])
            lw_s[bi, :, sl] = -math.exp(-0.5) * _sigmoid(zw)
            a = _sigmoid(vec(_VEC_A0, sl) + _dot(al, a2_ref[:, sl]))
            a_s[bi, :, sl] = a
            v = v_ref[bi, :, sl]
            if has_vfirst:
                mix = _sigmoid(vec(_VEC_V0, sl)
                               + _dot(lora_ref[bi, :, o1:o2].astype(BF16), v2_ref[:, sl]))
                v = v + (vf_ref[bi, :, sl] - v) * mix
            v_s[bi, :, sl] = v
            k = k_ref[bi, :, sl]
            kk = k * vec(_VEC_KK, sl)
            ss = _head_sum(kk * kk, ones_bd)
            kk_s[bi, :, sl] = kk * lax.rsqrt(jnp.maximum(ss, 1e-24))
            k2_s[bi, :, sl] = k * (1.0 + (a - 1.0) * vec(_VEC_KA, sl))

    masks = _scan_masks()
    problems = [(bi, gi) for bi in range(n_seqs) for gi in range(n_groups)]

    def body(ci, carry):
        t0 = pl.multiple_of(ci * c, c)
        sls = [(bi, pl.ds(t0, c), slice(gi * gc, (gi + 1) * gc)) for bi, gi in problems]
        ops = [(lw_s[sl], r_ref[sl], k2_s[sl], v_s[sl], kk_s[sl], a_s[sl]) for sl in sls]
        outs, new_states = _scan_chunk(ops, [state_ref[pi] for pi in range(len(problems))],
                                       masks, const_ref)
        for pi in range(len(problems)):
            o_s[sls[pi]] = outs[pi]
            state_ref[pi] = new_states[pi]
        return carry

    lax.fori_loop(0, n_chunks, body, 0)

    inv_n = 1.0 / RWKV_HEAD_DIM
    for bi in range(n_seqs):
        sig_gl = _sigmoid(lora_ref[bi, :, o1:o2]).astype(BF16)
        for gi in range(n_groups):
            sl = slice(gi * gc, (gi + 1) * gc)
            o = o_s[bi, :, sl]
            dev = o - _head_sum(o, ones_bd) * inv_n
            var = _head_sum(dev * dev, ones_bd) * inv_n
            normed = dev * lax.rsqrt(var + GN_EPS) * vec(_VEC_GNW, sl) + vec(_VEC_GNB, sl)
            bonus = _head_sum(r_ref[bi, :, sl] * k2_s[bi, :, sl] * vec(_VEC_RK, sl), ones_bd,
                              split=True) * v_s[bi, :, sl]
            gate = _dot(sig_gl, g2_ref[:, sl])
            o_ref[bi, :, sl] = ((normed + bonus) * gate).astype(o_ref.dtype)


def rwkv7_mixer(pr, pr_first, vecs, w2p, a2p, g2p, v2p, batch, seq, width):
    t, n = pr.shape
    tb = _pick(seq, (256, 128, 64))
    nb = seq // tb
    n_seqs = 2 if batch % 2 == 0 else 1
    n_groups = width // SCAN_GROUP_CH
    has_vfirst = pr_first is not None
    consts = _scan_constants()
    lora_col = 3 * width // LORA_BLOCK
    row_spec = lambda col: pl.BlockSpec((n_seqs, tb, width), lambda b, s: (b, s, col))
    full = lambda arr: pl.BlockSpec(arr.shape, lambda b, s: (0,) * arr.ndim)
    weights = [w2p, a2p, g2p] + ([v2p] if has_vfirst else [])
    in_specs = ([full(consts), full(vecs)] + [full(w) for w in weights]
                + [row_spec(0), row_spec(1), row_spec(2),
                   pl.BlockSpec((n_seqs, tb, LORA_BLOCK), lambda b, s: (b, s, lora_col))]
                + ([row_spec(2)] if has_vfirst else []))
    pr3 = pr.reshape(batch, seq, n)
    args = ([consts, vecs] + weights + [pr3, pr3, pr3, pr3]
            + ([pr_first.reshape(batch, seq, n)] if has_vfirst else []))
    out = pl.pallas_call(
        functools.partial(_rwkv_kernel, n_chunks=tb // SCAN_CHUNK, n_groups=n_groups,
                          has_vfirst=has_vfirst),
        grid=(batch // n_seqs, nb),
        in_specs=in_specs,
        out_specs=pl.BlockSpec((n_seqs, tb, width), lambda b, s: (b, s, 0)),
        out_shape=jax.ShapeDtypeStruct((batch, seq, width), BF16),
        scratch_shapes=([pltpu.VMEM((n_seqs * n_groups, SCAN_GROUP_CH, SCAN_GROUP_CH), F32)]
                        + [pltpu.VMEM((n_seqs, tb, width), F32)] * 6),
        compiler_params=_params(("parallel", "arbitrary")),
        name="rwkv7_mixer",
    )(*args)
    return out.reshape(t, width)


def _diff_attn_kernel(lam_ref, q_ref, k_ref, v_ref, w_ref, o_ref,
                      vt_ref, m_ref, acc_ref, s0_ref, s1_ref, *, tq, scale_out):
    step = pl.program_id(2)
    n_heads = q_ref.shape[1] // DIFF_V_DIM
    n_tiles = q_ref.shape[0] // tq

    @pl.when(step == 0)
    def _():
        for hh in range(n_heads):
            r0 = hh * ATTN_VT_ROWS
            for c0 in range(0, v_ref.shape[0], tq):
                vt_ref[r0:r0 + DIFF_V_DIM, c0:c0 + tq] = (
                    v_ref[c0:c0 + tq, hh * DIFF_V_DIM:(hh + 1) * DIFF_V_DIM].astype(F32).T.astype(BF16))
            vt_ref[r0 + DIFF_V_DIM:r0 + ATTN_VT_ROWS, :] = jnp.ones(
                (ATTN_VT_ROWS - DIFF_V_DIM, vt_ref.shape[1]), BF16)

    lane = lax.broadcasted_iota(jnp.int32, (tq, DIFF_V_DIM), 1)
    stream_heads = [hh for hh in range(n_heads) for _ in range(2)]

    def q_streams(tile):
        streams = []
        for hh in range(n_heads):
            q = q_ref[tile * tq:(tile + 1) * tq, hh * DIFF_V_DIM:(hh + 1) * DIFF_V_DIM]
            zero = jnp.zeros_like(q)
            streams.append(jnp.where(lane < DIFF_QK_DIM, q, zero))
            streams.append(jnp.where(lane >= DIFF_QK_DIM, q, zero))
        return streams

    tk = tq // 2
    s_bufs = (s0_ref, s1_ref)

    every_query = slice(0, tq)
    late_queries = slice(tk, tq)

    def scores(qs, j, slot, cols=every_query):
        k0 = pl.multiple_of(j * tk, tk)
        for si, (hh, qm) in enumerate(zip(stream_heads, qs)):
            kb = k_ref[pl.ds(k0, tk), hh * DIFF_V_DIM:(hh + 1) * DIFF_V_DIM]
            s_bufs[slot][si, :, cols] = _dot_nt(kb, qm[cols])

    def softmax_pv(j, slot, diag=False, cols=every_query):
        k0 = pl.multiple_of(j * tk, tk)
        for si, hh in enumerate(stream_heads):
            s = s_bufs[slot][si, :, cols]
            if diag:
                kpos = lax.broadcasted_iota(jnp.int32, s.shape, 0)
                qpos = lax.broadcasted_iota(jnp.int32, s.shape, 1)
                s = jnp.where(kpos <= qpos, s, -jnp.inf)
            m_old = m_ref[si, :, cols]
            m_new = jnp.maximum(m_old, jnp.max(s, axis=0, keepdims=True))
            alpha = jnp.exp2(m_old - m_new)
            p = jnp.exp2(s - m_new)
            m_ref[si, :, cols] = m_new
            vtb = vt_ref[hh * ATTN_VT_ROWS:(hh + 1) * ATTN_VT_ROWS, pl.ds(k0, tk)]
            acc_ref[si, :, cols] = acc_ref[si, :, cols] * alpha + _dot(vtb, p.astype(BF16))

    lam = lam_ref[0]

    def finish(tile):
        for hh in range(n_heads):
            a1, a2 = acc_ref[2 * hh], acc_ref[2 * hh + 1]
            nv = DIFF_V_DIM
            o_t = a1[:nv] * (1.0 / a1[nv:nv + 1]) - a2[:nv] * (lam / a2[nv:nv + 1])
            o = o_t.T
            ms = jnp.mean(o * o, axis=-1, keepdims=True)
            o_ref[tile * tq:(tile + 1) * tq, hh * DIFF_V_DIM:(hh + 1) * DIFF_V_DIM] = (
                o * lax.rsqrt(ms + SUBLN_EPS) * w_ref[...] * scale_out).astype(o_ref.dtype)

    qs = q_streams(0)
    scores(qs, 0, 0)
    for tile in range(n_tiles):
        qi = step * n_tiles + tile
        m_ref[...] = jnp.full_like(m_ref, -jnp.inf)
        acc_ref[...] = jnp.zeros_like(acc_ref)

        def body(jj, carry, qs=qs):
            scores(qs, 2 * jj + 1, 1)
            softmax_pv(2 * jj, 0)
            scores(qs, 2 * jj + 2, 0)
            softmax_pv(2 * jj + 1, 1)
            return carry

        lax.fori_loop(0, qi, body, 0)
        scores(qs, 2 * qi + 1, 1, cols=late_queries)
        softmax_pv(2 * qi, 0, diag=True)
        if tile + 1 < n_tiles:
            qs = q_streams(tile + 1)
            scores(qs, 0, 0)
        softmax_pv(2 * qi + 1, 1, diag=True, cols=late_queries)
        finish(tile)


def diff_attention(qkv, lam, subln_w, lambda_init, batch, seq, heads):
    t = qkv.shape[0]
    tq = _pick(seq, (1024, 512, 256, 128))
    nq = seq // tq
    hb = 2 if heads % 2 == 0 else 1
    wb = hb * DIFF_V_DIM
    n_hb = heads // hb
    tiles = 2 if nq % 2 == 0 else 1
    steps = nq // tiles
    q_rows = pl.BlockSpec((tiles * tq, wb), lambda b, h, i: (b * steps + i, h))
    return pl.pallas_call(
        functools.partial(_diff_attn_kernel, tq=tq, scale_out=1.0 - lambda_init),
        grid=(batch, n_hb, steps),
        in_specs=[
            pl.BlockSpec(memory_space=pltpu.SMEM),
            q_rows,
            pl.BlockSpec((seq, wb), lambda b, h, i: (b, n_hb + h)),
            pl.BlockSpec((seq, wb), lambda b, h, i: (b, 2 * n_hb + h)),
            pl.BlockSpec((1, DIFF_V_DIM), lambda b, h, i: (0, 0)),
        ],
        out_specs=q_rows,
        out_shape=jax.ShapeDtypeStruct((t, heads * DIFF_V_DIM), BF16),
        scratch_shapes=[
            pltpu.VMEM((hb * ATTN_VT_ROWS, seq), BF16),
            pltpu.VMEM((2 * hb, 1, tq), F32),
            pltpu.VMEM((2 * hb, ATTN_VT_ROWS, tq), F32),
            pltpu.VMEM((2 * hb, tq // 2, tq), F32),
            pltpu.VMEM((2 * hb, tq // 2, tq), F32),
        ],
        compiler_params=_params(("parallel", "parallel", "arbitrary")),
        name="diff_attention",
    )(lam.reshape(1).astype(F32), qkv, qkv, qkv, subln_w.reshape(1, DIFF_V_DIM).astype(F32))


def _residual_norm_epilogue(m, g_ref, res_ref, gn_ref, x_ref, h_ref, rows=slice(None)):
    ms = jnp.mean(m * m, axis=-1, keepdims=True)
    x = res_ref[rows, :] + m * lax.rsqrt(ms + NORM_EPS) * g_ref[...]
    x_ref[rows, :] = x
    if h_ref is not None:
        ms_x = jnp.mean(x * x, axis=-1, keepdims=True)
        h_ref[rows, :] = (x * lax.rsqrt(ms_x + NORM_EPS) * gn_ref[...]).astype(h_ref.dtype)


def _out_proj_kernel(a1_ref, a2_ref, w_ref, g_ref, res_ref, gn_ref, x_ref, h_ref):
    k1 = a1_ref.shape[1]
    tm = a1_ref.shape[0]
    for r0 in range(0, tm, V7X_MXU_DIM):
        rows = slice(r0, min(r0 + V7X_MXU_DIM, tm))
        m = _dot(a1_ref[rows, :], w_ref[0:k1, :]) + _dot(a2_ref[rows, :], w_ref[k1:, :])
        _residual_norm_epilogue(m, g_ref, res_ref, gn_ref, x_ref, h_ref, rows)


def out_proj(a1, a2, w, layer, g, res, g_next):
    t, k1 = a1.shape
    k2 = a2.shape[1]
    d = w.shape[2]
    tm = _pick(t, (512, 256, 128))
    row = pl.BlockSpec((tm, d), lambda i: (i, 0))
    vec = pl.BlockSpec((1, d), lambda i: (0, 0))
    return pl.pallas_call(
        _out_proj_kernel,
        grid=(t // tm,),
        in_specs=[
            pl.BlockSpec((tm, k1), lambda i: (i, 0)),
            pl.BlockSpec((tm, k2), lambda i: (i, 0)),
            pl.BlockSpec((None, k1 + k2, d), lambda i: (layer, 0, 0)),
            vec, row, vec,
        ],
        out_specs=[row, row],
        out_shape=[jax.ShapeDtypeStruct((t, d), F32), jax.ShapeDtypeStruct((t, d), BF16)],
        compiler_params=_params(("parallel",)),
        name="out_proj",
    )(a1, a2, w, g.reshape(1, d).astype(F32), res, g_next.reshape(1, d).astype(F32))


def _ffn_up_kernel(h_ref, wg_ref, wu_ref, cw_ref, o_ref, carry_ref, *, tiles_per_seq):
    i, j = pl.program_id(0), pl.program_id(1)

    @pl.when(i % tiles_per_seq == 0)
    def _():
        carry_ref[j] = jnp.zeros(carry_ref.shape[1:], F32)

    h = h_ref[...]
    tm = h.shape[0]
    chunks = [slice(c0, c0 + V7X_MXU_DIM) for c0 in range(0, o_ref.shape[1], V7X_MXU_DIM)]
    gates = [_dot(h, wg_ref[:, cs].astype(BF16)) for cs in chunks]
    for cs, gate in zip(chunks, gates):
        carry = carry_ref[j, :, cs]
        conv = (_rows_from_prev(gate, 2, carry) * cw_ref[0:1, cs]
                + _rows_from_prev(gate, 1, carry) * cw_ref[1:2, cs]
                + gate * cw_ref[2:3, cs] + cw_ref[3:4, cs])
        carry_ref[j, :, cs] = gate[tm - V7X_SUBLANES:, :]
        inner = math.sqrt(2.0 / math.pi) * (conv + 0.044715 * (conv * conv * conv))
        act = 0.5 * conv * (1.0 + jnp.tanh(inner))
        o_ref[:, cs] = (act * _dot(h, wu_ref[:, cs].astype(BF16))).astype(o_ref.dtype)


def ffn_up(h, w_up, layer, conv_wb, seq):
    t, d = h.shape
    f = w_up.shape[2] // 2
    tm = _pick(seq, (1024, 512, 256, 128))
    tn = _pick(f, (512, 256))
    nj = f // tn
    return pl.pallas_call(
        functools.partial(_ffn_up_kernel, tiles_per_seq=seq // tm),
        grid=(t // tm, nj),
        in_specs=[
            pl.BlockSpec((tm, d), lambda i, j: (i, 0)),
            pl.BlockSpec((None, d, tn), lambda i, j: (layer, 0, j)),
            pl.BlockSpec((None, d, tn), lambda i, j: (layer, 0, j + nj)),
            pl.BlockSpec((CONV_WIDTH + 1, tn), lambda i, j: (0, j)),
        ],
        out_specs=pl.BlockSpec((tm, tn), lambda i, j: (i, j)),
        out_shape=jax.ShapeDtypeStruct((t, f), BF16),
        scratch_shapes=[pltpu.VMEM((nj, V7X_SUBLANES, tn), F32)],
        compiler_params=_params(("arbitrary", "arbitrary")),
        name="ffn_up",
    )(h, w_up, w_up, conv_wb)


def _ffn_down_kernel(a_ref, w_ref, g_ref, res_ref, gn_ref, x_ref, *maybe_h_ref):
    m = _dot(a_ref[...], w_ref[...])
    _residual_norm_epilogue(m, g_ref, res_ref, gn_ref, x_ref, maybe_h_ref[0] if maybe_h_ref else None)


def ffn_down(a, w, layer, g, res, g_next):
    t, kdim = a.shape
    d = w.shape[2]
    tm = _pick(t, (256, 128))
    row = pl.BlockSpec((tm, d), lambda i: (i, 0))
    vec = pl.BlockSpec((1, d), lambda i: (0, 0))
    has_next = g_next is not None
    gn = (g_next if has_next else g).reshape(1, d).astype(F32)
    outs = pl.pallas_call(
        _ffn_down_kernel,
        grid=(t // tm,),
        in_specs=[
            pl.BlockSpec((tm, kdim), lambda i: (i, 0)),
            pl.BlockSpec((None, kdim, d), lambda i: (layer, 0, 0), pipeline_mode=pl.Buffered(1)),
            vec, row, vec,
        ],
        out_specs=[row, row] if has_next else [row],
        out_shape=([jax.ShapeDtypeStruct((t, d), F32)]
                   + ([jax.ShapeDtypeStruct((t, d), BF16)] if has_next else [])),
        compiler_params=_params(("parallel",)),
        name="ffn_down",
    )(a, w, g.reshape(1, d).astype(F32), res, gn)
    return (outs[0], outs[1]) if has_next else (outs[0], None)


def _pad_cols(w, width):
    return jnp.pad(w, [(0, 0)] * (w.ndim - 1) + [(0, width - w.shape[-1])])


def kernel(x, positions, pre_mix_norm, post_mix_norm, pre_ffn_norm, post_ffn_norm, w_in, w_mv_down, shift_mu, shift_mu_mv, w0, w2, a0, a2, g2, k_k, k_a, r_k, gn_w, gn_b, v0, v2, lam_q1, lam_k1, lam_q2, lam_k2, subln_w, w_out, w_up, conv_w, conv_b, w_down):
    batch, seq, d_model = x.shape
    t = batch * seq
    depth = w_in.shape[0]
    width = w0.shape[1]
    diff_heads = (d_model - width) // DIFF_V_DIM
    rwkv_cols = shift_mu.shape[1]
    qk_cols = diff_heads * 2 * DIFF_QK_DIM
    rwkv_n = 3 * width + LORA_BLOCK
    assert rwkv_n % IN_RWKV_TN == 0 and (3 * width) % LORA_BLOCK == 0

    rope = _rope_tables(positions)
    w_out_b = w_out.astype(BF16)
    w_down_b = w_down.astype(BF16)
    xf = x.reshape(t, d_model)
    h = rms_norm_bf16(xf, pre_mix_norm[0])
    pr_first = None
    for l in range(depth):
        o_r, o_wl = 0, width
        o_k = o_wl + DECAY_LORA
        o_v = o_k + width
        o_al = o_v + width
        o_gl = o_al + AAA_LORA
        if l == 0:
            mv_w = jnp.zeros((d_model, MV_LORA), F32)
            mv_mu = jnp.zeros((MV_LORA,), F32)
        else:
            mv_w = w_mv_down[l - 1]
            mv_mu = shift_mu_mv[l - 1]
        arrange = lambda m, mv: jnp.concatenate(
            [m[..., o_r:o_wl], m[..., o_k:o_v], m[..., o_v:o_al],
             m[..., o_wl:o_k], m[..., o_al:o_gl],
             _pad_cols(jnp.concatenate([m[..., o_gl:rwkv_cols], mv], axis=-1), LORA_GM)], axis=-1)
        w_rwkv = _pad_cols(arrange(w_in[l], mv_w), rwkv_n).astype(BF16)
        mu_rwkv = _pad_cols(arrange(shift_mu[l], mv_mu), rwkv_n)
        pr = in_proj_rwkv(h, w_rwkv, mu_rwkv, seq)

        vecs = jnp.stack([w0[l], a0[l], k_k[l], k_a[l], r_k[l].reshape(width), gn_w[l], gn_b[l],
                          v0[l - 1] if l > 0 else jnp.zeros((width,), F32)]).astype(F32)
        lora_rows = lambda w, start, rows: jnp.pad(
            w, ((start, rows - start - w.shape[0]), (0, 0))).astype(BF16)
        w2p = lora_rows(w2[l], 0, LORA_WA)
        a2p = lora_rows(a2[l], DECAY_LORA, LORA_WA)
        g2p = lora_rows(g2[l], 0, LORA_GM)
        v2p = lora_rows(v2[l - 1], GATE_LORA, LORA_GM) if l > 0 else None
        o_rwkv = rwkv7_mixer(pr, pr_first, vecs, w2p, a2p, g2p, v2p, batch, seq, width)
        if l == 0:
            pr_first = pr

        qkv = in_proj_attn(h, w_in[l][:, rwkv_cols:].astype(BF16), rope, seq, qk_cols)
        lambda_init = 0.8 - 0.6 * math.exp(-0.3 * l)
        lam = (jnp.exp(jnp.sum(lam_q1[l] * lam_k1[l])) - jnp.exp(jnp.sum(lam_q2[l] * lam_k2[l]))
               + lambda_init)
        o_diff = diff_attention(qkv, lam, subln_w[l], lambda_init, batch, seq, diff_heads)

        xf, h = out_proj(o_rwkv, o_diff, w_out_b, l, post_mix_norm[l], xf, pre_ffn_norm[l])

        conv_wb = jnp.concatenate([conv_w[l], conv_b[l][None, :]], axis=0).astype(F32)
        act = ffn_up(h, w_up, l, conv_wb, seq)
        g_next = pre_mix_norm[l + 1] if l + 1 < depth else None
        xf, h = ffn_down(act, w_down_b, l, post_ffn_norm[l], xf, g_next)
    return xf.reshape(batch, seq, d_model)
```

```python
import functools
import math

import jax
import jax.numpy as jnp
from jax import lax
from jax.experimental import pallas as pl
from jax.experimental.pallas import tpu as pltpu

F32 = jnp.float32
BF16 = jnp.bfloat16

RWKV_HEAD_DIM = 64
DIFF_QK_DIM = 64
DIFF_V_DIM = 128
DECAY_LORA = 64
AAA_LORA = 64
MV_LORA = 32
GATE_LORA = 160
CONV_WIDTH = 3
ROPE_THETA = 500000.0
ROPE_DIM = DIFF_QK_DIM // 4
NORM_EPS = 1e-6
GN_EPS = 64e-5
SUBLN_EPS = 1e-5

V7X_LANES = 128
V7X_SUBLANES = 8
V7X_MXU_DIM = 256
V7X_VMEM_LIMIT_BYTES = 56 * 1024 * 1024

SCAN_CHUNK = 64
SCAN_GROUP_HEADS = V7X_MXU_DIM // RWKV_HEAD_DIM
SCAN_GROUP_CH = SCAN_GROUP_HEADS * RWKV_HEAD_DIM
assert SCAN_CHUNK == RWKV_HEAD_DIM

LORA_WA = 128
LORA_GM = 256
LORA_BLOCK = 512
IN_RWKV_TN = 1792
IN_ATTN_TN = 1536
ATTN_VT_ROWS = DIFF_V_DIM + 16


def _pick(n, candidates):
    for c in candidates:
        if n % c == 0:
            return c
    return n


def _params(semantics):
    return pltpu.CompilerParams(dimension_semantics=semantics,
                                vmem_limit_bytes=V7X_VMEM_LIMIT_BYTES)


def _dot(a, b):
    return jnp.dot(a, b, preferred_element_type=F32)


def _dot_nt(a, b):
    return lax.dot_general(a, b, (((1,), (1,)), ((), ())), preferred_element_type=F32)


def _dot_tn(a, b):
    return lax.dot_general(a, b, (((0,), (0,)), ((), ())), preferred_element_type=F32)


def _rms_normed(x_ref, g_ref):
    x = x_ref[...]
    ms = jnp.mean(x * x, axis=-1, keepdims=True)
    return (x * lax.rsqrt(ms + NORM_EPS) * g_ref[...]).astype(BF16)


def _rows_from_prev(x, n, carry):
    rolled = pltpu.roll(x, n, 0)
    row = lax.broadcasted_iota(jnp.int32, x.shape, 0)
    for r in range(n):
        src = V7X_SUBLANES - n + r
        rolled = jnp.where(row == r, carry[src:src + 1, :], rolled)
    return rolled


def _rms_norm_kernel(x_ref, g_ref, o_ref):
    o_ref[...] = _rms_normed(x_ref, g_ref)


def rms_norm_bf16(x, g):
    t, d = x.shape
    tm = _pick(t, (1024, 512, 256, 128))
    return pl.pallas_call(
        _rms_norm_kernel,
        grid=(t // tm,),
        in_specs=[pl.BlockSpec((tm, d), lambda i: (i, 0)), pl.BlockSpec((1, d), lambda i: (0, 0))],
        out_specs=pl.BlockSpec((tm, d), lambda i: (i, 0)),
        out_shape=jax.ShapeDtypeStruct((t, d), BF16),
        compiler_params=_params(("parallel",)),
        name="rms_norm",
    )(x, g.reshape(1, d).astype(F32))


def _in_rwkv_kernel(h_ref, w_ref, mu_ref, o_ref, carry_ref, *, tiles_per_seq):
    i, j = pl.program_id(0), pl.program_id(1)

    @pl.when(i % tiles_per_seq == 0)
    def _():
        carry_ref[j] = jnp.zeros(carry_ref.shape[1:], F32)

    h = h_ref[...]
    tm = h.shape[0]
    for c0 in range(0, o_ref.shape[1], V7X_MXU_DIM):
        cs = slice(c0, c0 + V7X_MXU_DIM)
        p = _dot(h, w_ref[:, cs])
        prev = _rows_from_prev(p, 1, carry_ref[j, :, cs])
        o_ref[:, cs] = p + mu_ref[:, cs] * (prev - p)
        carry_ref[j, :, cs] = p[tm - V7X_SUBLANES:, :]


def in_proj_rwkv(h, w, mu, seq):
    t, d = h.shape
    n = w.shape[1]
    tm = _pick(seq, (1024, 512, 256, 128))
    tn = IN_RWKV_TN
    nj = n // tn
    return pl.pallas_call(
        functools.partial(_in_rwkv_kernel, tiles_per_seq=seq // tm),
        grid=(t // tm, nj),
        in_specs=[
            pl.BlockSpec((tm, d), lambda i, j: (i, 0)),
            pl.BlockSpec((d, tn), lambda i, j: (0, j)),
            pl.BlockSpec((1, tn), lambda i, j: (0, j)),
        ],
        out_specs=pl.BlockSpec((tm, tn), lambda i, j: (i, j)),
        out_shape=jax.ShapeDtypeStruct((t, n), F32),
        scratch_shapes=[pltpu.VMEM((nj, V7X_SUBLANES, tn), F32)],
        compiler_params=_params(("arbitrary", "arbitrary")),
        name="in_proj_rwkv",
    )(h, w, mu.reshape(1, n).astype(F32))


def _in_attn_kernel(h_ref, w_ref, rope_ref, o_ref, *, qk_cols, n_tiles):
    j = pl.program_id(1)
    tn = o_ref.shape[1]
    half = ROPE_DIM // 2
    q_scale = DIFF_QK_DIM ** -0.5 * math.log2(math.e)

    def tile(jt):
        h = h_ref[...]
        tables = {}
        for c0 in range(0, tn, V7X_MXU_DIM):
            col = jt * tn + c0
            p = _dot(h, w_ref[:, c0:c0 + V7X_MXU_DIM])
            if col >= 2 * qk_cols:
                o_ref[:, c0:c0 + V7X_MXU_DIM] = p.astype(o_ref.dtype)
                continue
            scale = q_scale if col < qk_cols else 1.0
            if scale not in tables:
                tables[scale] = [rope_ref[:, k * V7X_LANES:(k + 1) * V7X_LANES] * scale
                                 for k in range(3)]
            cos, sin_lo, sin_hi = tables[scale]
            for g0 in range(0, V7X_MXU_DIM, V7X_LANES):
                xg = p[:, g0:g0 + V7X_LANES]
                rot = (xg * cos + pltpu.roll(xg, V7X_LANES - half, 1) * sin_lo
                       + pltpu.roll(xg, half, 1) * sin_hi)
                o_ref[:, c0 + g0:c0 + g0 + V7X_LANES] = rot.astype(o_ref.dtype)

    for jt in range(n_tiles):
        pl.when(j == jt)(functools.partial(tile, jt))


def in_proj_attn(h, w, rope, seq, qk_cols):
    t, d = h.shape
    n = w.shape[1]
    tm = _pick(seq, (1024, 512, 256, 128))
    tn = IN_ATTN_TN
    assert qk_cols % V7X_MXU_DIM == 0 and n % tn == 0 and tn % V7X_MXU_DIM == 0
    return pl.pallas_call(
        functools.partial(_in_attn_kernel, qk_cols=qk_cols, n_tiles=n // tn),
        grid=(t // tm, n // tn),
        in_specs=[
            pl.BlockSpec((tm, d), lambda i, j: (i, 0)),
            pl.BlockSpec((d, tn), lambda i, j: (0, j)),
            pl.BlockSpec((tm, 3 * V7X_LANES), lambda i, j: (i, 0)),
        ],
        out_specs=pl.BlockSpec((tm, tn), lambda i, j: (i, j)),
        out_shape=jax.ShapeDtypeStruct((t, n), BF16),
        compiler_params=_params(("parallel", "arbitrary")),
        name="in_proj_attn",
    )(h, w, rope)


def _rope_tables(positions):
    half = ROPE_DIM // 2
    inv_freq = ROPE_THETA ** (-jnp.arange(0, ROPE_DIM, 2, dtype=F32) / ROPE_DIM)
    ang = positions.astype(F32).reshape(-1, 1) * inv_freq
    cos, sin = jnp.cos(ang), jnp.sin(ang)
    t = ang.shape[0]
    pad = DIFF_QK_DIM - ROPE_DIM
    cos_map = jnp.concatenate([cos, cos, jnp.ones((t, pad), F32)], axis=-1)
    lo_map = jnp.concatenate([-sin, jnp.zeros((t, half + pad), F32)], axis=-1)
    hi_map = jnp.concatenate([jnp.zeros((t, half), F32), sin, jnp.zeros((t, pad), F32)], axis=-1)
    reps = V7X_LANES // DIFF_QK_DIM
    return jnp.concatenate([jnp.tile(m, (1, reps)) for m in (cos_map, lo_map, hi_map)], axis=-1)


def _scan_masks():
    c, g, gc = SCAN_CHUNK, SCAN_GROUP_HEADS, SCAN_GROUP_CH
    row = lax.broadcasted_iota(jnp.int32, (g * c, gc), 0)
    col = lax.broadcasted_iota(jnp.int32, (g * c, gc), 1)
    same_block = (row // c) == (col // c)
    rowc = lax.broadcasted_iota(jnp.int32, (c, gc), 0)
    colc = lax.broadcasted_iota(jnp.int32, (c, gc), 1) % c
    strict_cat = rowc > colc
    incl_cat = rowc >= colc
    level0_cat = strict_cat & ((rowc // 2) == (colc // 2))
    eye_cat = jnp.where(rowc == colc, 1.0, 0.0).astype(F32)
    rt = lax.broadcasted_iota(jnp.int32, (c, c), 0)
    ct = lax.broadcasted_iota(jnp.int32, (c, c), 1)
    tril = jnp.where(rt >= ct, 1.0, 0.0).astype(BF16)
    return same_block, strict_cat, incl_cat, level0_cat, eye_cat, tril


def _scan_constants():
    idx = jnp.arange(SCAN_GROUP_HEADS * SCAN_CHUNK)
    t, s = idx[:, None], idx[None, :]
    mats = [t // 2 == s // 2]
    b = 2
    while b < SCAN_CHUNK:
        mats.append((t // (2 * b) == s // (2 * b)) & (t % (2 * b) >= b) & (s % (2 * b) < b))
        b *= 2
    mats.append(t // RWKV_HEAD_DIM == s // RWKV_HEAD_DIM)
    return jnp.stack(mats).astype(BF16)


def _sigmoid(x):
    return 0.5 * jnp.tanh(0.5 * x) + 0.5


def _head_sum(x, ones_bd, split=False):
    hi = x.astype(BF16)
    total = _dot(hi, ones_bd)
    if split:
        total = total + _dot((x - hi.astype(F32)).astype(BF16), ones_bd)
    return total


def _scan_chunk(ops, states, masks, const_ref):
    c, g, gc = SCAN_CHUNK, SCAN_GROUP_HEADS, SCAN_GROUP_CH
    same_block, strict_cat, incl_cat, level0_cat, eye_cat, tril = masks
    n_levels = const_ref.shape[0] - 1
    groups = range(len(ops))

    def tile(x):
        return jnp.concatenate([x] * g, axis=0)

    def bdexp(x):
        return jnp.where(same_block, tile(x), jnp.zeros_like(tile(x)))

    cums = []
    for lw, *_ in ops:
        lw_hi = lw.astype(BF16)
        lw_lo = (lw - lw_hi.astype(F32)).astype(BF16)
        cums.append(_dot(tril, lw_hi) + _dot(tril, lw_lo))

    a_b, b_b, k_b, r_b, end_b, v_b, w_end = [], [], [], [], [], [], []
    for (lw, r, k, v, kk, a), cum in zip(ops, cums):
        w_inc = jnp.exp(cum)
        w_inv = jnp.exp(-cum)
        w_prev = jnp.exp(cum - lw)
        w_last = w_inc[c - 1:c, :]
        b_f = kk * a * w_inv
        k_f = k * w_inv
        a_b.append((-(kk * w_prev)).astype(BF16))
        b_b.append(b_f.astype(BF16))
        k_b.append(k_f.astype(BF16))
        r_b.append((r * w_inc).astype(BF16))
        end_b.append(jnp.concatenate([(b_f * w_last).astype(BF16), (k_f * w_last).astype(BF16)], axis=0))
        v_b.append(v.astype(BF16))
        w_end.append(w_last)

    ar_b = [jnp.concatenate([a_b[i], r_b[i]], axis=0) for i in groups]
    l_cat = [_dot_nt(ar_b[i], jnp.concatenate([bdexp(b_b[i]), bdexp(k_b[i])], axis=0))
             for i in groups]
    l_ab = [jnp.where(strict_cat, lc[:c, :gc], 0.0).astype(BF16) for lc in l_cat]

    l_bd = [bdexp(l) for l in l_ab]
    t_cat = [eye_cat + jnp.where(level0_cat, l, jnp.zeros_like(l)).astype(F32) for l in l_ab]
    for lvl in range(1, n_levels):
        t_b = [t.astype(BF16) for t in t_cat]
        half = [_dot(t_b[i], l_bd[i] * const_ref[lvl]).astype(BF16) for i in groups]
        t_cat = [t_cat[i] + _dot(half[i], bdexp(t_b[i])) for i in groups]
    t_cat = [t.astype(BF16) for t in t_cat]

    l_rb = [jnp.where(incl_cat, lc[c:, :gc], 0.0).astype(BF16) for lc in l_cat]
    l_k = [jnp.concatenate([jnp.where(strict_cat, lc[:c, gc:], 0.0).astype(BF16),
                            jnp.where(incl_cat, lc[c:, gc:], 0.0).astype(BF16)], axis=0)
           for lc in l_cat]
    from_v = [_dot(l_k[i], bdexp(v_b[i])) for i in groups]

    from_state = [_dot_nt(ar_b[i], states[i].astype(BF16)) for i in groups]
    x_b = [(from_state[i][:c] + from_v[i][:c]).astype(BF16) for i in groups]
    u_b = [_dot(t_cat[i], bdexp(x_b[i])).astype(BF16) for i in groups]
    upd = [_dot_tn(jnp.concatenate([u_b[i], v_b[i]], axis=0), end_b[i]) for i in groups]
    new_states = [jnp.where(same_block, states[i] * w_end[i] + upd[i], 0.0) for i in groups]
    outs = [from_state[i][c:] + from_v[i][c:] + _dot(l_rb[i], bdexp(u_b[i])) for i in groups]
    return outs, new_states


_VEC_W0, _VEC_A0, _VEC_KK, _VEC_KA, _VEC_RK, _VEC_GNW, _VEC_GNB, _VEC_V0 = range(8)


def _rwkv_kernel(*refs, n_chunks, n_groups, has_vfirst):
    if has_vfirst:
        (const_ref, vec_ref, w2_ref, a2_ref, g2_ref, v2_ref, r_ref, k_ref, v_ref, lora_ref, vf_ref,
         o_ref, state_ref, lw_s, a_s, kk_s, k2_s, v_s, o_s) = refs
    else:
        (const_ref, vec_ref, w2_ref, a2_ref, g2_ref, r_ref, k_ref, v_ref, lora_ref,
         o_ref, state_ref, lw_s, a_s, kk_s, k2_s, v_s, o_s) = refs
        v2_ref = vf_ref = None

    @pl.when(pl.program_id(1) == 0)
    def _():
        state_ref[...] = jnp.zeros_like(state_ref)

    c, gc = SCAN_CHUNK, SCAN_GROUP_CH
    n_seqs = r_ref.shape[0]
    ones_bd = const_ref[const_ref.shape[0] - 1]
    vec = lambda row, sl: vec_ref[row:row + 1, sl]

    o1 = LORA_WA
    o2 = o1 + LORA_GM

    for bi in range(n_seqs):
        tanh_wl = jnp.tanh(lora_ref[bi, :, 0:o1]).astype(BF16)
        al = lora_ref[bi, :, 0:o1].astype(BF16)
        for gi in range(n_groups):
            sl = slice(gi * gc, (gi + 1) * gc)
            zw = vec(_VEC_W0, sl) + _dot(tanh_wl, w2_ref[:, sl])
            lw_s[bi, :, sl] = -math.exp(-0.5) * _sigmoid(zw)
            a = _sigmoid(vec(_VEC_A0, sl) + _dot(al, a2_ref[:, sl]))
            a_s[bi, :, sl] = a
            v = v_ref[bi, :, sl]
            if has_vfirst:
                mix = _sigmoid(vec(_VEC_V0, sl)
                               + _dot(lora_ref[bi, :, o1:o2].astype(BF16), v2_ref[:, sl]))
                v = v + (vf_ref[bi, :, sl] - v) * mix
            v_s[bi, :, sl] = v
            k = k_ref[bi, :, sl]
            kk = k * vec(_VEC_KK, sl)
            ss = _head_sum(kk * kk, ones_bd)
            kk_s[bi, :, sl] = kk * lax.rsqrt(jnp.maximum(ss, 1e-24))
            k2_s[bi, :, sl] = k * (1.0 + (a - 1.0) * vec(_VEC_KA, sl))

    masks = _scan_masks()
    problems = [(bi, gi) for bi in range(n_seqs) for gi in range(n_groups)]

    def body(ci, carry):
        t0 = pl.multiple_of(ci * c, c)
        sls = [(bi, pl.ds(t0, c), slice(gi * gc, (gi + 1) * gc)) for bi, gi in problems]
        ops = [(lw_s[sl], r_ref[sl], k2_s[sl], v_s[sl], kk_s[sl], a_s[sl]) for sl in sls]
        outs, new_states = _scan_chunk(ops, [state_ref[pi] for pi in range(len(problems))],
                                       masks, const_ref)
        for pi in range(len(problems)):
            o_s[sls[pi]] = outs[pi]
            state_ref[pi] = new_states[pi]
        return carry

    lax.fori_loop(0, n_chunks, body, 0)

    inv_n = 1.0 / RWKV_HEAD_DIM
    for bi in range(n_seqs):
        sig_gl = _sigmoid(lora_ref[bi, :, o1:o2]).astype(BF16)
        for gi in range(n_groups):
            sl = slice(gi * gc, (gi + 1) * gc)
            o = o_s[bi, :, sl]
            dev = o - _head_sum(o, ones_bd) * inv_n
            var = _head_sum(dev * dev, ones_bd) * inv_n
            normed = dev * lax.rsqrt(var + GN_EPS) * vec(_VEC_GNW, sl) + vec(_VEC_GNB, sl)
            bonus = _head_sum(r_ref[bi, :, sl] * k2_s[bi, :, sl] * vec(_VEC_RK, sl), ones_bd,
                              split=True) * v_s[bi, :, sl]
            gate = _dot(sig_gl, g2_ref[:, sl])
            o_ref[bi, :, sl] = ((normed + bonus) * gate).astype(o_ref.dtype)


def rwkv7_mixer(pr, pr_first, vecs, w2p, a2p, g2p, v2p, batch, seq, width):
    t, n = pr.shape
    tb = _pick(seq, (256, 128, 64))
    nb = seq // tb
    n_seqs = 2 if batch % 2 == 0 else 1
    n_groups = width // SCAN_GROUP_CH
    has_vfirst = pr_first is not None
    consts = _scan_constants()
    lora_col = 3 * width // LORA_BLOCK
    row_spec = lambda col: pl.BlockSpec((n_seqs, tb, width), lambda b, s: (b, s, col))
    full = lambda arr: pl.BlockSpec(arr.shape, lambda b, s: (0,) * arr.ndim)
    weights = [w2p, a2p, g2p] + ([v2p] if has_vfirst else [])
    in_specs = ([full(consts), full(vecs)] + [full(w) for w in weights]
                + [row_spec(0), row_spec(1), row_spec(2),
                   pl.BlockSpec((n_seqs, tb, LORA_BLOCK), lambda b, s: (b, s, lora_col))]
                + ([row_spec(2)] if has_vfirst else []))
    pr3 = pr.reshape(batch, seq, n)
    args = ([consts, vecs] + weights + [pr3, pr3, pr3, pr3]
            + ([pr_first.reshape(batch, seq, n)] if has_vfirst else []))
    out = pl.pallas_call(
        functools.partial(_rwkv_kernel, n_chunks=tb // SCAN_CHUNK, n_groups=n_groups,
                          has_vfirst=has_vfirst),
        grid=(batch // n_seqs, nb),
        in_specs=in_specs,
        out_specs=pl.BlockSpec((n_seqs, tb, width), lambda b, s: (b, s, 0)),
        out_shape=jax.ShapeDtypeStruct((batch, seq, width), BF16),
        scratch_shapes=([pltpu.VMEM((n_seqs * n_groups, SCAN_GROUP_CH, SCAN_GROUP_CH), F32)]
                        + [pltpu.VMEM((n_seqs, tb, width), F32)] * 6),
        compiler_params=_params(("parallel", "arbitrary")),
        name="rwkv7_mixer",
    )(*args)
    return out.reshape(t, width)


def _diff_attn_kernel(lam_ref, q_ref, k_ref, v_ref, w_ref, o_ref,
                      vt_ref, m_ref, acc_ref, s0_ref, s1_ref, *, tq, scale_out):
    step = pl.program_id(2)
    n_heads = q_ref.shape[1] // DIFF_V_DIM
    n_tiles = q_ref.shape[0] // tq

    @pl.when(step == 0)
    def _():
        for hh in range(n_heads):
            r0 = hh * ATTN_VT_ROWS
            for c0 in range(0, v_ref.shape[0], tq):
                vt_ref[r0:r0 + DIFF_V_DIM, c0:c0 + tq] = (
                    v_ref[c0:c0 + tq, hh * DIFF_V_DIM:(hh + 1) * DIFF_V_DIM].astype(F32).T.astype(BF16))
            vt_ref[r0 + DIFF_V_DIM:r0 + ATTN_VT_ROWS, :] = jnp.ones(
                (ATTN_VT_ROWS - DIFF_V_DIM, vt_ref.shape[1]), BF16)

    lane = lax.broadcasted_iota(jnp.int32, (tq, DIFF_V_DIM), 1)
    stream_heads = [hh for hh in range(n_heads) for _ in range(2)]

    def q_streams(tile):
        streams = []
        for hh in range(n_heads):
            q = q_ref[tile * tq:(tile + 1) * tq, hh * DIFF_V_DIM:(hh + 1) * DIFF_V_DIM]
            zero = jnp.zeros_like(q)
            streams.append(jnp.where(lane < DIFF_QK_DIM, q, zero))
            streams.append(jnp.where(lane >= DIFF_QK_DIM, q, zero))
        return streams

    tk = tq // 2
    s_bufs = (s0_ref, s1_ref)

    every_query = slice(0, tq)
    late_queries = slice(tk, tq)

    def scores(qs, j, slot, cols=every_query):
        k0 = pl.multiple_of(j * tk, tk)
        for si, (hh, qm) in enumerate(zip(stream_heads, qs)):
            kb = k_ref[pl.ds(k0, tk), hh * DIFF_V_DIM:(hh + 1) * DIFF_V_DIM]
            s_bufs[slot][si, :, cols] = _dot_nt(kb, qm[cols])

    def softmax_pv(j, slot, diag=False, cols=every_query):
        k0 = pl.multiple_of(j * tk, tk)
        for si, hh in enumerate(stream_heads):
            s = s_bufs[slot][si, :, cols]
            if diag:
                kpos = lax.broadcasted_iota(jnp.int32, s.shape, 0)
                qpos = lax.broadcasted_iota(jnp.int32, s.shape, 1)
                s = jnp.where(kpos <= qpos, s, -jnp.inf)
            m_old = m_ref[si, :, cols]
            m_new = jnp.maximum(m_old, jnp.max(s, axis=0, keepdims=True))
            alpha = jnp.exp2(m_old - m_new)
            p = jnp.exp2(s - m_new)
            m_ref[si, :, cols] = m_new
            vtb = vt_ref[hh * ATTN_VT_ROWS:(hh + 1) * ATTN_VT_ROWS, pl.ds(k0, tk)]
            acc_ref[si, :, cols] = acc_ref[si, :, cols] * alpha + _dot(vtb, p.astype(BF16))

    lam = lam_ref[0]

    def finish(tile):
        for hh in range(n_heads):
            a1, a2 = acc_ref[2 * hh], acc_ref[2 * hh + 1]
            nv = DIFF_V_DIM
            o_t = a1[:nv] * (1.0 / a1[nv:nv + 1]) - a2[:nv] * (lam / a2[nv:nv + 1])
            o = o_t.T
            ms = jnp.mean(o * o, axis=-1, keepdims=True)
            o_ref[tile * tq:(tile + 1) * tq, hh * DIFF_V_DIM:(hh + 1) * DIFF_V_DIM] = (
                o * lax.rsqrt(ms + SUBLN_EPS) * w_ref[...] * scale_out).astype(o_ref.dtype)

    qs = q_streams(0)
    scores(qs, 0, 0)
    for tile in range(n_tiles):
        qi = step * n_tiles + tile
        m_ref[...] = jnp.full_like(m_ref, -jnp.inf)
        acc_ref[...] = jnp.zeros_like(acc_ref)

        def body(jj, carry, qs=qs):
            scores(qs, 2 * jj + 1, 1)
            softmax_pv(2 * jj, 0)
            scores(qs, 2 * jj + 2, 0)
            softmax_pv(2 * jj + 1, 1)
            return carry

        lax.fori_loop(0, qi, body, 0)
        scores(qs, 2 * qi + 1, 1, cols=late_queries)
        softmax_pv(2 * qi, 0, diag=True)
        if tile + 1 < n_tiles:
            qs = q_streams(tile + 1)
            scores(qs, 0, 0)
        softmax_pv(2 * qi + 1, 1, diag=True, cols=late_queries)
        finish(tile)


def diff_attention(qkv, lam, subln_w, lambda_init, batch, seq, heads):
    t = qkv.shape[0]
    tq = _pick(seq, (1024, 512, 256, 128))
    nq = seq // tq
    hb = 2 if heads % 2 == 0 else 1
    wb = hb * DIFF_V_DIM
    n_hb = heads // hb
    tiles = _pick(nq, (4, 2))
    steps = nq // tiles
    q_rows = pl.BlockSpec((tiles * tq, wb), lambda b, h, i: (b * steps + i, h))
    return pl.pallas_call(
        functools.partial(_diff_attn_kernel, tq=tq, scale_out=1.0 - lambda_init),
        grid=(batch, n_hb, steps),
        in_specs=[
            pl.BlockSpec(memory_space=pltpu.SMEM),
            q_rows,
            pl.BlockSpec((seq, wb), lambda b, h, i: (b, n_hb + h)),
            pl.BlockSpec((seq, wb), lambda b, h, i: (b, 2 * n_hb + h)),
            pl.BlockSpec((1, DIFF_V_DIM), lambda b, h, i: (0, 0)),
        ],
        out_specs=q_rows,
        out_shape=jax.ShapeDtypeStruct((t, heads * DIFF_V_DIM), BF16),
        scratch_shapes=[
            pltpu.VMEM((hb * ATTN_VT_ROWS, seq), BF16),
            pltpu.VMEM((2 * hb, 1, tq), F32),
            pltpu.VMEM((2 * hb, ATTN_VT_ROWS, tq), F32),
            pltpu.VMEM((2 * hb, tq // 2, tq), F32),
            pltpu.VMEM((2 * hb, tq // 2, tq), F32),
        ],
        compiler_params=_params(("parallel", "parallel", "arbitrary")),
        name="diff_attention",
    )(lam.reshape(1).astype(F32), qkv, qkv, qkv, subln_w.reshape(1, DIFF_V_DIM).astype(F32))


def _residual_norm_epilogue(m, g_ref, res_ref, gn_ref, x_ref, h_ref, rows=slice(None)):
    ms = jnp.mean(m * m, axis=-1, keepdims=True)
    x = res_ref[rows, :] + m * lax.rsqrt(ms + NORM_EPS) * g_ref[...]
    x_ref[rows, :] = x
    if h_ref is not None:
        ms_x = jnp.mean(x * x, axis=-1, keepdims=True)
        h_ref[rows, :] = (x * lax.rsqrt(ms_x + NORM_EPS) * gn_ref[...]).astype(h_ref.dtype)


def _out_proj_kernel(a1_ref, a2_ref, w_ref, g_ref, res_ref, gn_ref, x_ref, h_ref):
    k1 = a1_ref.shape[1]
    tm = a1_ref.shape[0]
    for r0 in range(0, tm, V7X_MXU_DIM):
        rows = slice(r0, min(r0 + V7X_MXU_DIM, tm))
        m = _dot(a1_ref[rows, :], w_ref[0:k1, :]) + _dot(a2_ref[rows, :], w_ref[k1:, :])
        _residual_norm_epilogue(m, g_ref, res_ref, gn_ref, x_ref, h_ref, rows)


def out_proj(a1, a2, w, layer, g, res, g_next):
    t, k1 = a1.shape
    k2 = a2.shape[1]
    d = w.shape[2]
    tm = _pick(t, (512, 256, 128))
    row = pl.BlockSpec((tm, d), lambda i: (i, 0))
    vec = pl.BlockSpec((1, d), lambda i: (0, 0))
    return pl.pallas_call(
        _out_proj_kernel,
        grid=(t // tm,),
        in_specs=[
            pl.BlockSpec((tm, k1), lambda i: (i, 0)),
            pl.BlockSpec((tm, k2), lambda i: (i, 0)),
            pl.BlockSpec((None, k1 + k2, d), lambda i: (layer, 0, 0)),
            vec, row, vec,
        ],
        out_specs=[row, row],
        out_shape=[jax.ShapeDtypeStruct((t, d), F32), jax.ShapeDtypeStruct((t, d), BF16)],
        compiler_params=_params(("parallel",)),
        name="out_proj",
    )(a1, a2, w, g.reshape(1, d).astype(F32), res, g_next.reshape(1, d).astype(F32))


def _ffn_up_kernel(h_ref, wg_ref, wu_ref, cw_ref, o_ref, carry_ref, *, tiles_per_seq):
    i, j = pl.program_id(0), pl.program_id(1)

    @pl.when(i % tiles_per_seq == 0)
    def _():
        carry_ref[j] = jnp.zeros(carry_ref.shape[1:], F32)

    h = h_ref[...]
    tm = h.shape[0]
    chunks = [slice(c0, c0 + V7X_MXU_DIM) for c0 in range(0, o_ref.shape[1], V7X_MXU_DIM)]
    gates = [_dot(h, wg_ref[:, cs].astype(BF16)) for cs in chunks]
    for cs, gate in zip(chunks, gates):
        carry = carry_ref[j, :, cs]
        conv = (_rows_from_prev(gate, 2, carry) * cw_ref[0:1, cs]
                + _rows_from_prev(gate, 1, carry) * cw_ref[1:2, cs]
                + gate * cw_ref[2:3, cs] + cw_ref[3:4, cs])
        carry_ref[j, :, cs] = gate[tm - V7X_SUBLANES:, :]
        inner = math.sqrt(2.0 / math.pi) * (conv + 0.044715 * (conv * conv * conv))
        act = 0.5 * conv * (1.0 + jnp.tanh(inner))
        o_ref[:, cs] = (act * _dot(h, wu_ref[:, cs].astype(BF16))).astype(o_ref.dtype)


def ffn_up(h, w_up, layer, conv_wb, seq):
    t, d = h.shape
    f = w_up.shape[2] // 2
    tm = _pick(seq, (1024, 512, 256, 128))
    tn = _pick(f, (512, 256))
    nj = f // tn
    return pl.pallas_call(
        functools.partial(_ffn_up_kernel, tiles_per_seq=seq // tm),
        grid=(t // tm, nj),
        in_specs=[
            pl.BlockSpec((tm, d), lambda i, j: (i, 0)),
            pl.BlockSpec((None, d, tn), lambda i, j: (layer, 0, j)),
            pl.BlockSpec((None, d, tn), lambda i, j: (layer, 0, j + nj)),
            pl.BlockSpec((CONV_WIDTH + 1, tn), lambda i, j: (0, j)),
        ],
        out_specs=pl.BlockSpec((tm, tn), lambda i, j: (i, j)),
        out_shape=jax.ShapeDtypeStruct((t, f), BF16),
        scratch_shapes=[pltpu.VMEM((nj, V7X_SUBLANES, tn), F32)],
        compiler_params=_params(("arbitrary", "arbitrary")),
        name="ffn_up",
    )(h, w_up, w_up, conv_wb)


def _ffn_down_kernel(a_ref, w_ref, g_ref, res_ref, gn_ref, x_ref, *maybe_h_ref):
    m = _dot(a_ref[...], w_ref[...])
    _residual_norm_epilogue(m, g_ref, res_ref, gn_ref, x_ref, maybe_h_ref[0] if maybe_h_ref else None)


def ffn_down(a, w, layer, g, res, g_next):
    t, kdim = a.shape
    d = w.shape[2]
    tm = _pick(t, (256, 128))
    row = pl.BlockSpec((tm, d), lambda i: (i, 0))
    vec = pl.BlockSpec((1, d), lambda i: (0, 0))
    has_next = g_next is not None
    gn = (g_next if has_next else g).reshape(1, d).astype(F32)
    outs = pl.pallas_call(
        _ffn_down_kernel,
        grid=(t // tm,),
        in_specs=[
            pl.BlockSpec((tm, kdim), lambda i: (i, 0)),
            pl.BlockSpec((None, kdim, d), lambda i: (layer, 0, 0), pipeline_mode=pl.Buffered(1)),
            vec, row, vec,
        ],
        out_specs=[row, row] if has_next else [row],
        out_shape=([jax.ShapeDtypeStruct((t, d), F32)]
                   + ([jax.ShapeDtypeStruct((t, d), BF16)] if has_next else [])),
        compiler_params=_params(("parallel",)),
        name="ffn_down",
    )(a, w, g.reshape(1, d).astype(F32), res, gn)
    return (outs[0], outs[1]) if has_next else (outs[0], None)


def _pad_cols(w, width):
    return jnp.pad(w, [(0, 0)] * (w.ndim - 1) + [(0, width - w.shape[-1])])


def kernel(x, positions, pre_mix_norm, post_mix_norm, pre_ffn_norm, post_ffn_norm, w_in, w_mv_down, shift_mu, shift_mu_mv, w0, w2, a0, a2, g2, k_k, k_a, r_k, gn_w, gn_b, v0, v2, lam_q1, lam_k1, lam_q2, lam_k2, subln_w, w_out, w_up, conv_w, conv_b, w_down):
    batch, seq, d_model = x.shape
    t = batch * seq
    depth = w_in.shape[0]
    width = w0.shape[1]
    diff_heads = (d_model - width) // DIFF_V_DIM
    rwkv_cols = shift_mu.shape[1]
    qk_cols = diff_heads * 2 * DIFF_QK_DIM
    rwkv_n = 3 * width + LORA_BLOCK
    assert rwkv_n % IN_RWKV_TN == 0 and (3 * width) % LORA_BLOCK == 0

    rope = _rope_tables(positions)
    w_out_b = w_out.astype(BF16)
    w_down_b = w_down.astype(BF16)
    xf = x.reshape(t, d_model)
    h = rms_norm_bf16(xf, pre_mix_norm[0])
    pr_first = None
    for l in range(depth):
        o_r, o_wl = 0, width
        o_k = o_wl + DECAY_LORA
        o_v = o_k + width
        o_al = o_v + width
        o_gl = o_al + AAA_LORA
        if l == 0:
            mv_w = jnp.zeros((d_model, MV_LORA), F32)
            mv_mu = jnp.zeros((MV_LORA,), F32)
        else:
            mv_w = w_mv_down[l - 1]
            mv_mu = shift_mu_mv[l - 1]
        arrange = lambda m, mv: jnp.concatenate(
            [m[..., o_r:o_wl], m[..., o_k:o_v], m[..., o_v:o_al],
             m[..., o_wl:o_k], m[..., o_al:o_gl],
             _pad_cols(jnp.concatenate([m[..., o_gl:rwkv_cols], mv], axis=-1), LORA_GM)], axis=-1)
        w_rwkv = _pad_cols(arrange(w_in[l], mv_w), rwkv_n).astype(BF16)
        mu_rwkv = _pad_cols(arrange(shift_mu[l], mv_mu), rwkv_n)
        pr = in_proj_rwkv(h, w_rwkv, mu_rwkv, seq)

        vecs = jnp.stack([w0[l], a0[l], k_k[l], k_a[l], r_k[l].reshape(width), gn_w[l], gn_b[l],
                          v0[l - 1] if l > 0 else jnp.zeros((width,), F32)]).astype(F32)
        lora_rows = lambda w, start, rows: jnp.pad(
            w, ((start, rows - start - w.shape[0]), (0, 0))).astype(BF16)
        w2p = lora_rows(w2[l], 0, LORA_WA)
        a2p = lora_rows(a2[l], DECAY_LORA, LORA_WA)
        g2p = lora_rows(g2[l], 0, LORA_GM)
        v2p = lora_rows(v2[l - 1], GATE_LORA, LORA_GM) if l > 0 else None
        o_rwkv = rwkv7_mixer(pr, pr_first, vecs, w2p, a2p, g2p, v2p, batch, seq, width)
        if l == 0:
            pr_first = pr

        qkv = in_proj_attn(h, w_in[l][:, rwkv_cols:].astype(BF16), rope, seq, qk_cols)
        lambda_init = 0.8 - 0.6 * math.exp(-0.3 * l)
        lam = (jnp.exp(jnp.sum(lam_q1[l] * lam_k1[l])) - jnp.exp(jnp.sum(lam_q2[l] * lam_k2[l]))
               + lambda_init)
        o_diff = diff_attention(qkv, lam, subln_w[l], lambda_init, batch, seq, diff_heads)

        xf, h = out_proj(o_rwkv, o_diff, w_out_b, l, post_mix_norm[l], xf, pre_ffn_norm[l])

        conv_wb = jnp.concatenate([conv_w[l], conv_b[l][None, :]], axis=0).astype(F32)
        act = ffn_up(h, w_up, l, conv_wb, seq)
        g_next = pre_mix_norm[l + 1] if l + 1 < depth else None
        xf, h = ffn_down(act, w_down_b, l, post_ffn_norm[l], xf, g_next)
    return xf.reshape(batch, seq, d_model)
```

```python
import functools
import math

import jax
import jax.numpy as jnp
import numpy as np
from jax import lax
from jax.experimental import pallas as pl
from jax.experimental.pallas import tpu as pltpu

F32 = jnp.float32
BF16 = jnp.bfloat16

RWKV_HEAD_DIM = 64
DIFF_QK_DIM = 64
DIFF_V_DIM = 128
DECAY_LORA = 64
AAA_LORA = 64
MV_LORA = 32
GATE_LORA = 160
CONV_WIDTH = 3
ROPE_THETA = 500000.0
ROPE_DIM = DIFF_QK_DIM // 4
NORM_EPS = 1e-6
GN_EPS = 64e-5
SUBLN_EPS = 1e-5

V7X_LANES = 128
V7X_SUBLANES = 8
V7X_MXU_DIM = 256
V7X_VMEM_LIMIT_BYTES = 56 * 1024 * 1024

SCAN_CHUNK = 64
SCAN_GROUP_HEADS = V7X_MXU_DIM // RWKV_HEAD_DIM
SCAN_GROUP_CH = SCAN_GROUP_HEADS * RWKV_HEAD_DIM
assert SCAN_CHUNK == RWKV_HEAD_DIM

LORA_WA = 128
LORA_GM = 256
LORA_BLOCK = 512
IN_RWKV_TN = 1792
IN_ATTN_TN = 1536
ATTN_VT_ROWS = DIFF_V_DIM + 16


def _pick(n, candidates):
    for c in candidates:
        if n % c == 0:
            return c
    return n


def _params(semantics):
    return pltpu.CompilerParams(dimension_semantics=semantics,
                                vmem_limit_bytes=V7X_VMEM_LIMIT_BYTES)


def _dot(a, b):
    return jnp.dot(a, b, preferred_element_type=F32)


def _dot_nt(a, b):
    return lax.dot_general(a, b, (((1,), (1,)), ((), ())), preferred_element_type=F32)


def _dot_tn(a, b):
    return lax.dot_general(a, b, (((0,), (0,)), ((), ())), preferred_element_type=F32)


def _rms_normed(x_ref, g_ref):
    x = x_ref[...]
    ms = jnp.mean(x * x, axis=-1, keepdims=True)
    return (x * lax.rsqrt(ms + NORM_EPS) * g_ref[...]).astype(BF16)


def _rows_from_prev(x, n, carry):
    rolled = pltpu.roll(x, n, 0)
    row = lax.broadcasted_iota(jnp.int32, x.shape, 0)
    for r in range(n):
        src = V7X_SUBLANES - n + r
        rolled = jnp.where(row == r, carry[src:src + 1, :], rolled)
    return rolled


def _rms_norm_kernel(x_ref, g_ref, o_ref):
    o_ref[...] = _rms_normed(x_ref, g_ref)


def rms_norm_bf16(x, g):
    t, d = x.shape
    tm = _pick(t, (1024, 512, 256, 128))
    return pl.pallas_call(
        _rms_norm_kernel,
        grid=(t // tm,),
        in_specs=[pl.BlockSpec((tm, d), lambda i: (i, 0)), pl.BlockSpec((1, d), lambda i: (0, 0))],
        out_specs=pl.BlockSpec((tm, d), lambda i: (i, 0)),
        out_shape=jax.ShapeDtypeStruct((t, d), BF16),
        compiler_params=_params(("parallel",)),
        name="rms_norm",
    )(x, g.reshape(1, d).astype(F32))


def _in_rwkv_kernel(h_ref, w_ref, mu_ref, o_ref, carry_ref, *, tiles_per_seq):
    i, j = pl.program_id(0), pl.program_id(1)

    @pl.when(i % tiles_per_seq == 0)
    def _():
        carry_ref[j] = jnp.zeros(carry_ref.shape[1:], F32)

    h = h_ref[...]
    tm = h.shape[0]
    for c0 in range(0, o_ref.shape[1], V7X_MXU_DIM):
        cs = slice(c0, c0 + V7X_MXU_DIM)
        p = _dot(h, w_ref[:, cs])
        prev = _rows_from_prev(p, 1, carry_ref[j, :, cs])
        o_ref[:, cs] = p + mu_ref[:, cs] * (prev - p)
        carry_ref[j, :, cs] = p[tm - V7X_SUBLANES:, :]


def in_proj_rwkv(h, w, mu, seq):
    t, d = h.shape
    n = w.shape[1]
    tm = _pick(seq, (1024, 512, 256, 128))
    tn = IN_RWKV_TN
    nj = n // tn
    return pl.pallas_call(
        functools.partial(_in_rwkv_kernel, tiles_per_seq=seq // tm),
        grid=(t // tm, nj),
        in_specs=[
            pl.BlockSpec((tm, d), lambda i, j: (i, 0)),
            pl.BlockSpec((d, tn), lambda i, j: (0, j)),
            pl.BlockSpec((1, tn), lambda i, j: (0, j)),
        ],
        out_specs=pl.BlockSpec((tm, tn), lambda i, j: (i, j)),
        out_shape=jax.ShapeDtypeStruct((t, n), F32),
        scratch_shapes=[pltpu.VMEM((nj, V7X_SUBLANES, tn), F32)],
        compiler_params=_params(("arbitrary", "arbitrary")),
        name="in_proj_rwkv",
    )(h, w, mu.reshape(1, n).astype(F32))


def _in_attn_kernel(h_ref, w_ref, rope_ref, o_ref, *, qk_cols, n_tiles):
    j = pl.program_id(1)
    tn = o_ref.shape[1]
    half = ROPE_DIM // 2
    q_scale = DIFF_QK_DIM ** -0.5 * math.log2(math.e)

    def tile(jt):
        h = h_ref[...]
        tables = {}
        for c0 in range(0, tn, V7X_MXU_DIM):
            col = jt * tn + c0
            p = _dot(h, w_ref[:, c0:c0 + V7X_MXU_DIM])
            if col >= 2 * qk_cols:
                o_ref[:, c0:c0 + V7X_MXU_DIM] = p.astype(o_ref.dtype)
                continue
            scale = q_scale if col < qk_cols else 1.0
            if scale not in tables:
                tables[scale] = [rope_ref[:, k * V7X_LANES:(k + 1) * V7X_LANES] * scale
                                 for k in range(3)]
            cos, sin_lo, sin_hi = tables[scale]
            for g0 in range(0, V7X_MXU_DIM, V7X_LANES):
                xg = p[:, g0:g0 + V7X_LANES]
                rot = (xg * cos + pltpu.roll(xg, V7X_LANES - half, 1) * sin_lo
                       + pltpu.roll(xg, half, 1) * sin_hi)
                o_ref[:, c0 + g0:c0 + g0 + V7X_LANES] = rot.astype(o_ref.dtype)

    for jt in range(n_tiles):
        pl.when(j == jt)(functools.partial(tile, jt))


def in_proj_attn(h, w, rope, seq, qk_cols):
    t, d = h.shape
    n = w.shape[1]
    tm = _pick(seq, (1024, 512, 256, 128))
    tn = IN_ATTN_TN
    assert qk_cols % V7X_MXU_DIM == 0 and n % tn == 0 and tn % V7X_MXU_DIM == 0
    return pl.pallas_call(
        functools.partial(_in_attn_kernel, qk_cols=qk_cols, n_tiles=n // tn),
        grid=(t // tm, n // tn),
        in_specs=[
            pl.BlockSpec((tm, d), lambda i, j: (i, 0)),
            pl.BlockSpec((d, tn), lambda i, j: (0, j)),
            pl.BlockSpec((tm, 3 * V7X_LANES), lambda i, j: (i, 0)),
        ],
        out_specs=pl.BlockSpec((tm, tn), lambda i, j: (i, j)),
        out_shape=jax.ShapeDtypeStruct((t, n), BF16),
        compiler_params=_params(("parallel", "arbitrary")),
        name="in_proj_attn",
    )(h, w, rope)


def _rope_tables(positions):
    half = ROPE_DIM // 2
    inv_freq = ROPE_THETA ** (-jnp.arange(0, ROPE_DIM, 2, dtype=F32) / ROPE_DIM)
    ang = positions.astype(F32).reshape(-1, 1) * inv_freq
    feats = jnp.concatenate([jnp.cos(ang), jnp.sin(ang), jnp.ones((ang.shape[0], 1), F32)], axis=-1)
    select = np.zeros((2 * half + 1, 3 * V7X_LANES), np.float32)
    for lane in range(V7X_LANES):
        ll = lane % DIFF_QK_DIM
        if ll < ROPE_DIM:
            select[ll % half, lane] = 1.0
        else:
            select[2 * half, lane] = 1.0
        if ll < half:
            select[half + ll, V7X_LANES + lane] = -1.0
        elif ll < ROPE_DIM:
            select[half + ll - half, 2 * V7X_LANES + lane] = 1.0
    return jnp.dot(feats, jnp.asarray(select), precision=lax.Precision.HIGHEST)


def _scan_masks():
    c, g, gc = SCAN_CHUNK, SCAN_GROUP_HEADS, SCAN_GROUP_CH
    row = lax.broadcasted_iota(jnp.int32, (g * c, gc), 0)
    col = lax.broadcasted_iota(jnp.int32, (g * c, gc), 1)
    same_block = (row // c) == (col // c)
    rowc = lax.broadcasted_iota(jnp.int32, (c, gc), 0)
    colc = lax.broadcasted_iota(jnp.int32, (c, gc), 1) % c
    strict_cat = rowc > colc
    incl_cat = rowc >= colc
    level0_cat = strict_cat & ((rowc // 2) == (colc // 2))
    eye_cat = jnp.where(rowc == colc, 1.0, 0.0).astype(F32)
    rt = lax.broadcasted_iota(jnp.int32, (c, c), 0)
    ct = lax.broadcasted_iota(jnp.int32, (c, c), 1)
    tril = jnp.where(rt >= ct, 1.0, 0.0).astype(BF16)
    return same_block, strict_cat, incl_cat, level0_cat, eye_cat, tril


def _scan_constants():
    idx = jnp.arange(SCAN_GROUP_HEADS * SCAN_CHUNK)
    t, s = idx[:, None], idx[None, :]
    mats = [t // 2 == s // 2]
    b = 2
    while b < SCAN_CHUNK:
        mats.append((t // (2 * b) == s // (2 * b)) & (t % (2 * b) >= b) & (s % (2 * b) < b))
        b *= 2
    mats.append(t // RWKV_HEAD_DIM == s // RWKV_HEAD_DIM)
    return jnp.stack(mats).astype(BF16)


def _sigmoid(x):
    return 0.5 * jnp.tanh(0.5 * x) + 0.5


def _head_sum(x, ones_bd, split=False):
    hi = x.astype(BF16)
    total = _dot(hi, ones_bd)
    if split:
        total = total + _dot((x - hi.astype(F32)).astype(BF16), ones_bd)
    return total


def _scan_chunk(ops, states, masks, const_ref):
    c, g, gc = SCAN_CHUNK, SCAN_GROUP_HEADS, SCAN_GROUP_CH
    same_block, strict_cat, incl_cat, level0_cat, eye_cat, tril = masks
    n_levels = const_ref.shape[0] - 1
    groups = range(len(ops))

    def tile(x):
        return jnp.concatenate([x] * g, axis=0)

    def bdexp(x):
        return jnp.where(same_block, tile(x), jnp.zeros_like(tile(x)))

    cums = []
    for lw, *_ in ops:
        lw_hi = lw.astype(BF16)
        lw_lo = (lw - lw_hi.astype(F32)).astype(BF16)
        cums.append(_dot(tril, lw_hi) + _dot(tril, lw_lo))

    a_b, b_b, k_b, r_b, end_b, v_b, w_end = [], [], [], [], [], [], []
    for (lw, r, k, v, kk, a), cum in zip(ops, cums):
        w_inc = jnp.exp(cum)
        w_inv = jnp.exp(-cum)
        w_prev = jnp.exp(cum - lw)
        w_last = w_inc[c - 1:c, :]
        b_f = kk * a * w_inv
        k_f = k * w_inv
        a_b.append((-(kk * w_prev)).astype(BF16))
        b_b.append(b_f.astype(BF16))
        k_b.append(k_f.astype(BF16))
        r_b.append((r * w_inc).astype(BF16))
        end_b.append(jnp.concatenate([(b_f * w_last).astype(BF16), (k_f * w_last).astype(BF16)], axis=0))
        v_b.append(v.astype(BF16))
        w_end.append(w_last)

    ar_b = [jnp.concatenate([a_b[i], r_b[i]], axis=0) for i in groups]
    l_cat = [_dot_nt(ar_b[i], jnp.concatenate([bdexp(b_b[i]), bdexp(k_b[i])], axis=0))
             for i in groups]
    l_ab = [jnp.where(strict_cat, lc[:c, :gc], 0.0).astype(BF16) for lc in l_cat]

    l_bd = [bdexp(l) for l in l_ab]
    t_cat = [eye_cat + jnp.where(level0_cat, l, jnp.zeros_like(l)).astype(F32) for l in l_ab]
    for lvl in range(1, n_levels):
        t_b = [t.astype(BF16) for t in t_cat]
        half = [_dot(t_b[i], l_bd[i] * const_ref[lvl]).astype(BF16) for i in groups]
        t_cat = [t_cat[i] + _dot(half[i], bdexp(t_b[i])) for i in groups]
    t_cat = [t.astype(BF16) for t in t_cat]

    l_rb = [jnp.where(incl_cat, lc[c:, :gc], 0.0).astype(BF16) for lc in l_cat]
    l_k = [jnp.concatenate([jnp.where(strict_cat, lc[:c, gc:], 0.0).astype(BF16),
                            jnp.where(incl_cat, lc[c:, gc:], 0.0).astype(BF16)], axis=0)
           for lc in l_cat]
    from_v = [_dot(l_k[i], bdexp(v_b[i])) for i in groups]

    from_state = [_dot_nt(ar_b[i], states[i].astype(BF16)) for i in groups]
    x_b = [(from_state[i][:c] + from_v[i][:c]).astype(BF16) for i in groups]
    u_b = [_dot(t_cat[i], bdexp(x_b[i])).astype(BF16) for i in groups]
    upd = [_dot_tn(jnp.concatenate([u_b[i], v_b[i]], axis=0), end_b[i]) for i in groups]
    new_states = [jnp.where(same_block, states[i] * w_end[i] + upd[i], 0.0) for i in groups]
    outs = [from_state[i][c:] + from_v[i][c:] + _dot(l_rb[i], bdexp(u_b[i])) for i in groups]
    return outs, new_states


_VEC_W0, _VEC_A0, _VEC_KK, _VEC_KA, _VEC_RK, _VEC_GNW, _VEC_GNB, _VEC_V0 = range(8)


def _rwkv_kernel(*refs, n_chunks, n_groups, has_vfirst):
    if has_vfirst:
        (const_ref, vec_ref, w2_ref, a2_ref, g2_ref, v2_ref, r_ref, k_ref, v_ref, lora_ref, vf_ref,
         o_ref, state_ref, lw_s, a_s, kk_s, k2_s, v_s, o_s) = refs
    else:
        (const_ref, vec_ref, w2_ref, a2_ref, g2_ref, r_ref, k_ref, v_ref, lora_ref,
         o_ref, state_ref, lw_s, a_s, kk_s, k2_s, v_s, o_s) = refs
        v2_ref = vf_ref = None

    @pl.when(pl.program_id(1) == 0)
    def _():
        state_ref[...] = jnp.zeros_like(state_ref)

    c, gc = SCAN_CHUNK, SCAN_GROUP_CH
    n_seqs = r_ref.shape[0]
    ones_bd = const_ref[const_ref.shape[0] - 1]
    vec = lambda row, sl: vec_ref[row:row + 1, sl]

    o1 = LORA_WA
    o2 = o1 + LORA_GM

    for bi in range(n_seqs):
        tanh_wl = jnp.tanh(lora_ref[bi, :, 0:o1]).astype(BF16)
        al = lora_ref[bi, :, 0:o1].astype(BF16)
        for gi in range(n_groups):
            sl = slice(gi * gc, (gi + 1) * gc)
            zw = vec(_VEC_W0, sl) + _dot(tanh_wl, w2_ref[:, sl])
            lw_s[bi, :, sl] = -math.exp(-0.5) * _sigmoid(zw)
            a = _sigmoid(vec(_VEC_A0, sl) + _dot(al, a2_ref[:, sl]))
            a_s[bi, :, sl] = a
            v = v_ref[bi, :, sl]
            if has_vfirst:
                mix = _sigmoid(vec(_VEC_V0, sl)
                               + _dot(lora_ref[bi, :, o1:o2].astype(BF16), v2_ref[:, sl]))
                v = v + (vf_ref[bi, :, sl] - v) * mix
            v_s[bi, :, sl] = v
            k = k_ref[bi, :, sl]
            kk = k * vec(_VEC_KK, sl)
            ss = _head_sum(kk * kk, ones_bd)
            kk_s[bi, :, sl] = kk * lax.rsqrt(jnp.maximum(ss, 1e-24))
            k2_s[bi, :, sl] = k * (1.0 + (a - 1.0) * vec(_VEC_KA, sl))

    masks = _scan_masks()
    problems = [(bi, gi) for bi in range(n_seqs) for gi in range(n_groups)]

    def body(ci, carry):
        t0 = pl.multiple_of(ci * c, c)
        sls = [(bi, pl.ds(t0, c), slice(gi * gc, (gi + 1) * gc)) for bi, gi in problems]
        ops = [(lw_s[sl], r_ref[sl], k2_s[sl], v_s[sl], kk_s[sl], a_s[sl]) for sl in sls]
        outs, new_states = _scan_chunk(ops, [state_ref[pi] for pi in range(len(problems))],
                                       masks, const_ref)
        for pi in range(len(problems)):
            o_s[sls[pi]] = outs[pi]
            state_ref[pi] = new_states[pi]
        return carry

    lax.fori_loop(0, n_chunks, body, 0, unroll=2)

    inv_n = 1.0 / RWKV_HEAD_DIM
    for bi in range(n_seqs):
        sig_gl = _sigmoid(lora_ref[bi, :, o1:o2]).astype(BF16)
        for gi in range(n_groups):
            sl = slice(gi * gc, (gi + 1) * gc)
            o = o_s[bi, :, sl]
            dev = o - _head_sum(o, ones_bd) * inv_n
            var = _head_sum(dev * dev, ones_bd) * inv_n
            normed = dev * lax.rsqrt(var + GN_EPS) * vec(_VEC_GNW, sl) + vec(_VEC_GNB, sl)
            bonus = _head_sum(r_ref[bi, :, sl] * k2_s[bi, :, sl] * vec(_VEC_RK, sl), ones_bd,
                              split=True) * v_s[bi, :, sl]
            gate = _dot(sig_gl, g2_ref[:, sl])
            o_ref[bi, :, sl] = ((normed + bonus) * gate).astype(o_ref.dtype)


def rwkv7_mixer(pr, pr_first, vecs, w2p, a2p, g2p, v2p, batch, seq, width):
    t, n = pr.shape
    tb = _pick(seq, (256, 128, 64))
    nb = seq // tb
    n_seqs = 2 if batch % 2 == 0 else 1
    n_groups = width // SCAN_GROUP_CH
    has_vfirst = pr_first is not None
    consts = _scan_constants()
    lora_col = 3 * width // LORA_BLOCK
    row_spec = lambda col: pl.BlockSpec((n_seqs, tb, width), lambda b, s: (b, s, col))
    full = lambda arr: pl.BlockSpec(arr.shape, lambda b, s: (0,) * arr.ndim)
    weights = [w2p, a2p, g2p] + ([v2p] if has_vfirst else [])
    in_specs = ([full(consts), full(vecs)] + [full(w) for w in weights]
                + [row_spec(0), row_spec(1), row_spec(2),
                   pl.BlockSpec((n_seqs, tb, LORA_BLOCK), lambda b, s: (b, s, lora_col))]
                + ([row_spec(2)] if has_vfirst else []))
    pr3 = pr.reshape(batch, seq, n)
    args = ([consts, vecs] + weights + [pr3, pr3, pr3, pr3]
            + ([pr_first.reshape(batch, seq, n)] if has_vfirst else []))
    out = pl.pallas_call(
        functools.partial(_rwkv_kernel, n_chunks=tb // SCAN_CHUNK, n_groups=n_groups,
                          has_vfirst=has_vfirst),
        grid=(batch // n_seqs, nb),
        in_specs=in_specs,
        out_specs=pl.BlockSpec((n_seqs, tb, width), lambda b, s: (b, s, 0)),
        out_shape=jax.ShapeDtypeStruct((batch, seq, width), BF16),
        scratch_shapes=([pltpu.VMEM((n_seqs * n_groups, SCAN_GROUP_CH, SCAN_GROUP_CH), F32)]
                        + [pltpu.VMEM((n_seqs, tb, width), F32)] * 6),
        compiler_params=_params(("parallel", "arbitrary")),
        name="rwkv7_mixer",
    )(*args)
    return out.reshape(t, width)


def _diff_attn_kernel(lam_ref, q_ref, k_ref, v_ref, w_ref, o_ref,
                      vt_ref, m_ref, acc_ref, s0_ref, s1_ref, *, tq, scale_out):
    step = pl.program_id(2)
    n_heads = q_ref.shape[1] // DIFF_V_DIM
    n_tiles = q_ref.shape[0] // tq

    @pl.when(step == 0)
    def _():
        for hh in range(n_heads):
            r0 = hh * ATTN_VT_ROWS
            for c0 in range(0, v_ref.shape[0], tq):
                vt_ref[r0:r0 + DIFF_V_DIM, c0:c0 + tq] = (
                    v_ref[c0:c0 + tq, hh * DIFF_V_DIM:(hh + 1) * DIFF_V_DIM].astype(F32).T.astype(BF16))
            vt_ref[r0 + DIFF_V_DIM:r0 + ATTN_VT_ROWS, :] = jnp.ones(
                (ATTN_VT_ROWS - DIFF_V_DIM, vt_ref.shape[1]), BF16)

    lane = lax.broadcasted_iota(jnp.int32, (tq, DIFF_V_DIM), 1)
    stream_heads = [hh for hh in range(n_heads) for _ in range(2)]

    def q_streams(tile):
        streams = []
        for hh in range(n_heads):
            q = q_ref[tile * tq:(tile + 1) * tq, hh * DIFF_V_DIM:(hh + 1) * DIFF_V_DIM]
            zero = jnp.zeros_like(q)
            streams.append(jnp.where(lane < DIFF_QK_DIM, q, zero))
            streams.append(jnp.where(lane >= DIFF_QK_DIM, q, zero))
        return streams

    tk = tq // 2
    s_bufs = (s0_ref, s1_ref)

    every_query = slice(0, tq)
    late_queries = slice(tk, tq)

    def scores(qs, j, slot, cols=every_query):
        k0 = pl.multiple_of(j * tk, tk)
        for si, (hh, qm) in enumerate(zip(stream_heads, qs)):
            kb = k_ref[pl.ds(k0, tk), hh * DIFF_V_DIM:(hh + 1) * DIFF_V_DIM]
            s_bufs[slot][si, :, cols] = _dot_nt(kb, qm[cols])

    def softmax_pv(j, slot, diag=False, cols=every_query):
        k0 = pl.multiple_of(j * tk, tk)
        for si, hh in enumerate(stream_heads):
            s = s_bufs[slot][si, :, cols]
            if diag:
                kpos = lax.broadcasted_iota(jnp.int32, s.shape, 0)
                qpos = lax.broadcasted_iota(jnp.int32, s.shape, 1)
                s = jnp.where(kpos <= qpos, s, -jnp.inf)
            m_old = m_ref[si, :, cols]
            m_new = jnp.maximum(m_old, jnp.max(s, axis=0, keepdims=True))
            alpha = jnp.exp2(m_old - m_new)
            p = jnp.exp2(s - m_new)
            m_ref[si, :, cols] = m_new
            vtb = vt_ref[hh * ATTN_VT_ROWS:(hh + 1) * ATTN_VT_ROWS, pl.ds(k0, tk)]
            acc_ref[si, :, cols] = acc_ref[si, :, cols] * alpha + _dot(vtb, p.astype(BF16))

    lam = lam_ref[0]

    def finish(tile):
        for hh in range(n_heads):
            a1, a2 = acc_ref[2 * hh], acc_ref[2 * hh + 1]
            nv = DIFF_V_DIM
            o_t = a1[:nv] * (1.0 / a1[nv:nv + 1]) - a2[:nv] * (lam / a2[nv:nv + 1])
            o = o_t.T
            ms = jnp.mean(o * o, axis=-1, keepdims=True)
            o_ref[tile * tq:(tile + 1) * tq, hh * DIFF_V_DIM:(hh + 1) * DIFF_V_DIM] = (
                o * lax.rsqrt(ms + SUBLN_EPS) * w_ref[...] * scale_out).astype(o_ref.dtype)

    qs = q_streams(0)
    scores(qs, 0, 0)
    for tile in range(n_tiles):
        qi = step * n_tiles + tile
        m_ref[...] = jnp.full_like(m_ref, -jnp.inf)
        acc_ref[...] = jnp.zeros_like(acc_ref)

        def body(jj, carry, qs=qs):
            scores(qs, 2 * jj + 1, 1)
            softmax_pv(2 * jj, 0)
            scores(qs, 2 * jj + 2, 0)
            softmax_pv(2 * jj + 1, 1)
            return carry

        lax.fori_loop(0, qi, body, 0)
        scores(qs, 2 * qi + 1, 1, cols=late_queries)
        softmax_pv(2 * qi, 0, diag=True)
        if tile + 1 < n_tiles:
            qs = q_streams(tile + 1)
            scores(qs, 0, 0)
        softmax_pv(2 * qi + 1, 1, diag=True, cols=late_queries)
        finish(tile)


def diff_attention(qkv, lam, subln_w, lambda_init, batch, seq, heads):
    t = qkv.shape[0]
    tq = _pick(seq, (1024, 512, 256, 128))
    nq = seq // tq
    hb = 2 if heads % 2 == 0 else 1
    wb = hb * DIFF_V_DIM
    n_hb = heads // hb
    tiles = _pick(nq, (4, 2))
    steps = nq // tiles
    q_rows = pl.BlockSpec((tiles * tq, wb), lambda b, h, i: (b * steps + i, h))
    return pl.pallas_call(
        functools.partial(_diff_attn_kernel, tq=tq, scale_out=1.0 - lambda_init),
        grid=(batch, n_hb, steps),
        in_specs=[
            pl.BlockSpec(memory_space=pltpu.SMEM),
            q_rows,
            pl.BlockSpec((seq, wb), lambda b, h, i: (b, n_hb + h)),
            pl.BlockSpec((seq, wb), lambda b, h, i: (b, 2 * n_hb + h)),
            pl.BlockSpec((1, DIFF_V_DIM), lambda b, h, i: (0, 0)),
        ],
        out_specs=q_rows,
        out_shape=jax.ShapeDtypeStruct((t, heads * DIFF_V_DIM), BF16),
        scratch_shapes=[
            pltpu.VMEM((hb * ATTN_VT_ROWS, seq), BF16),
            pltpu.VMEM((2 * hb, 1, tq), F32),
            pltpu.VMEM((2 * hb, ATTN_VT_ROWS, tq), F32),
            pltpu.VMEM((2 * hb, tq // 2, tq), F32),
            pltpu.VMEM((2 * hb, tq // 2, tq), F32),
        ],
        compiler_params=_params(("parallel", "parallel", "arbitrary")),
        name="diff_attention",
    )(lam.reshape(1).astype(F32), qkv, qkv, qkv, subln_w.reshape(1, DIFF_V_DIM).astype(F32))


def _residual_norm_epilogue(m, g_ref, res_ref, gn_ref, x_ref, h_ref, rows=slice(None)):
    ms = jnp.mean(m * m, axis=-1, keepdims=True)
    x = res_ref[rows, :] + m * lax.rsqrt(ms + NORM_EPS) * g_ref[...]
    x_ref[rows, :] = x
    if h_ref is not None:
        ms_x = jnp.mean(x * x, axis=-1, keepdims=True)
        h_ref[rows, :] = (x * lax.rsqrt(ms_x + NORM_EPS) * gn_ref[...]).astype(h_ref.dtype)


def _out_proj_kernel(a1_ref, a2_ref, w_ref, g_ref, res_ref, gn_ref, x_ref, h_ref):
    k1 = a1_ref.shape[1]
    tm = a1_ref.shape[0]
    for r0 in range(0, tm, V7X_MXU_DIM):
        rows = slice(r0, min(r0 + V7X_MXU_DIM, tm))
        m = _dot(a1_ref[rows, :], w_ref[0:k1, :]) + _dot(a2_ref[rows, :], w_ref[k1:, :])
        _residual_norm_epilogue(m, g_ref, res_ref, gn_ref, x_ref, h_ref, rows)


def out_proj(a1, a2, w, layer, g, res, g_next):
    t, k1 = a1.shape
    k2 = a2.shape[1]
    d = w.shape[2]
    tm = _pick(t, (512, 256, 128))
    row = pl.BlockSpec((tm, d), lambda i: (i, 0))
    vec = pl.BlockSpec((1, d), lambda i: (0, 0))
    return pl.pallas_call(
        _out_proj_kernel,
        grid=(t // tm,),
        in_specs=[
            pl.BlockSpec((tm, k1), lambda i: (i, 0)),
            pl.BlockSpec((tm, k2), lambda i: (i, 0)),
            pl.BlockSpec((None, k1 + k2, d), lambda i: (layer, 0, 0)),
            vec, row, vec,
        ],
        out_specs=[row, row],
        out_shape=[jax.ShapeDtypeStruct((t, d), F32), jax.ShapeDtypeStruct((t, d), BF16)],
        compiler_params=_params(("parallel",)),
        name="out_proj",
    )(a1, a2, w, g.reshape(1, d).astype(F32), res, g_next.reshape(1, d).astype(F32))


def _ffn_up_kernel(h_ref, wg_ref, wu_ref, cw_ref, o_ref, carry_ref, *, tiles_per_seq):
    i, j = pl.program_id(0), pl.program_id(1)

    @pl.when(i % tiles_per_seq == 0)
    def _():
        carry_ref[j] = jnp.zeros(carry_ref.shape[1:], F32)

    h = h_ref[...]
    tm = h.shape[0]
    chunks = [slice(c0, c0 + V7X_MXU_DIM) for c0 in range(0, o_ref.shape[1], V7X_MXU_DIM)]
    gates = [_dot(h, wg_ref[:, cs].astype(BF16)) for cs in chunks]
    for cs, gate in zip(chunks, gates):
        carry = carry_ref[j, :, cs]
        conv = (_rows_from_prev(gate, 2, carry) * cw_ref[0:1, cs]
                + _rows_from_prev(gate, 1, carry) * cw_ref[1:2, cs]
                + gate * cw_ref[2:3, cs] + cw_ref[3:4, cs])
        carry_ref[j, :, cs] = gate[tm - V7X_SUBLANES:, :]
        inner = math.sqrt(2.0 / math.pi) * (conv + 0.044715 * (conv * conv * conv))
        act = 0.5 * conv * (1.0 + jnp.tanh(inner))
        o_ref[:, cs] = (act * _dot(h, wu_ref[:, cs].astype(BF16))).astype(o_ref.dtype)


def ffn_up(h, w_up, layer, conv_wb, seq):
    t, d = h.shape
    f = w_up.shape[2] // 2
    tm = _pick(seq, (1024, 512, 256, 128))
    tn = _pick(f, (512, 256))
    nj = f // tn
    return pl.pallas_call(
        functools.partial(_ffn_up_kernel, tiles_per_seq=seq // tm),
        grid=(t // tm, nj),
        in_specs=[
            pl.BlockSpec((tm, d), lambda i, j: (i, 0)),
            pl.BlockSpec((None, d, tn), lambda i, j: (layer, 0, j)),
            pl.BlockSpec((None, d, tn), lambda i, j: (layer, 0, j + nj)),
            pl.BlockSpec((CONV_WIDTH + 1, tn), lambda i, j: (0, j)),
        ],
        out_specs=pl.BlockSpec((tm, tn), lambda i, j: (i, j)),
        out_shape=jax.ShapeDtypeStruct((t, f), BF16),
        scratch_shapes=[pltpu.VMEM((nj, V7X_SUBLANES, tn), F32)],
        compiler_params=_params(("arbitrary", "arbitrary")),
        name="ffn_up",
    )(h, w_up, w_up, conv_wb)


def _ffn_down_kernel(a_ref, w_ref, g_ref, res_ref, gn_ref, x_ref, *maybe_h_ref):
    m = _dot(a_ref[...], w_ref[...])
    _residual_norm_epilogue(m, g_ref, res_ref, gn_ref, x_ref, maybe_h_ref[0] if maybe_h_ref else None)


def ffn_down(a, w, layer, g, res, g_next):
    t, kdim = a.shape
    d = w.shape[2]
    tm = _pick(t, (256, 128))
    row = pl.BlockSpec((tm, d), lambda i: (i, 0))
    vec = pl.BlockSpec((1, d), lambda i: (0, 0))
    has_next = g_next is not None
    gn = (g_next if has_next else g).reshape(1, d).astype(F32)
    outs = pl.pallas_call(
        _ffn_down_kernel,
        grid=(t // tm,),
        in_specs=[
            pl.BlockSpec((tm, kdim), lambda i: (i, 0)),
            pl.BlockSpec((None, kdim, d), lambda i: (layer, 0, 0), pipeline_mode=pl.Buffered(1)),
            vec, row, vec,
        ],
        out_specs=[row, row] if has_next else [row],
        out_shape=([jax.ShapeDtypeStruct((t, d), F32)]
                   + ([jax.ShapeDtypeStruct((t, d), BF16)] if has_next else [])),
        compiler_params=_params(("parallel",)),
        name="ffn_down",
    )(a, w, g.reshape(1, d).astype(F32), res, gn)
    return (outs[0], outs[1]) if has_next else (outs[0], None)


def _pad_cols(w, width):
    return jnp.pad(w, [(0, 0)] * (w.ndim - 1) + [(0, width - w.shape[-1])])


def kernel(x, positions, pre_mix_norm, post_mix_norm, pre_ffn_norm, post_ffn_norm, w_in, w_mv_down, shift_mu, shift_mu_mv, w0, w2, a0, a2, g2, k_k, k_a, r_k, gn_w, gn_b, v0, v2, lam_q1, lam_k1, lam_q2, lam_k2, subln_w, w_out, w_up, conv_w, conv_b, w_down):
    batch, seq, d_model = x.shape
    t = batch * seq
    depth = w_in.shape[0]
    width = w0.shape[1]
    diff_heads = (d_model - width) // DIFF_V_DIM
    rwkv_cols = shift_mu.shape[1]
    qk_cols = diff_heads * 2 * DIFF_QK_DIM
    rwkv_n = 3 * width + LORA_BLOCK
    assert rwkv_n % IN_RWKV_TN == 0 and (3 * width) % LORA_BLOCK == 0

    rope = _rope_tables(positions)
    w_out_b = w_out.astype(BF16)
    w_down_b = w_down.astype(BF16)
    xf = x.reshape(t, d_model)
    h = rms_norm_bf16(xf, pre_mix_norm[0])
    pr_first = None
    for l in range(depth):
        o_r, o_wl = 0, width
        o_k = o_wl + DECAY_LORA
        o_v = o_k + width
        o_al = o_v + width
        o_gl = o_al + AAA_LORA
        if l == 0:
            mv_w = jnp.zeros((d_model, MV_LORA), F32)
            mv_mu = jnp.zeros((MV_LORA,), F32)
        else:
            mv_w = w_mv_down[l - 1]
            mv_mu = shift_mu_mv[l - 1]
        arrange = lambda m, mv: jnp.concatenate(
            [m[..., o_r:o_wl], m[..., o_k:o_v], m[..., o_v:o_al],
             m[..., o_wl:o_k], m[..., o_al:o_gl],
             _pad_cols(jnp.concatenate([m[..., o_gl:rwkv_cols], mv], axis=-1), LORA_GM)], axis=-1)
        w_rwkv = _pad_cols(arrange(w_in[l], mv_w), rwkv_n).astype(BF16)
        mu_rwkv = _pad_cols(arrange(shift_mu[l], mv_mu), rwkv_n)
        pr = in_proj_rwkv(h, w_rwkv, mu_rwkv, seq)

        vecs = jnp.stack([w0[l], a0[l], k_k[l], k_a[l], r_k[l].reshape(width), gn_w[l], gn_b[l],
                          v0[l - 1] if l > 0 else jnp.zeros((width,), F32)]).astype(F32)
        lora_rows = lambda w, start, rows: jnp.pad(
            w, ((start, rows - start - w.shape[0]), (0, 0))).astype(BF16)
        w2p = lora_rows(w2[l], 0, LORA_WA)
        a2p = lora_rows(a2[l], DECAY_LORA, LORA_WA)
        g2p = lora_rows(g2[l], 0, LORA_GM)
        v2p = lora_rows(v2[l - 1], GATE_LORA, LORA_GM) if l > 0 else None
        o_rwkv = rwkv7_mixer(pr, pr_first, vecs, w2p, a2p, g2p, v2p, batch, seq, width)
        if l == 0:
            pr_first = pr

        qkv = in_proj_attn(h, w_in[l][:, rwkv_cols:].astype(BF16), rope, seq, qk_cols)
        lambda_init = 0.8 - 0.6 * math.exp(-0.3 * l)
        lam = (jnp.exp(jnp.sum(lam_q1[l] * lam_k1[l])) - jnp.exp(jnp.sum(lam_q2[l] * lam_k2[l]))
               + lambda_init)
        o_diff = diff_attention(qkv, lam, subln_w[l], lambda_init, batch, seq, diff_heads)

        xf, h = out_proj(o_rwkv, o_diff, w_out_b, l, post_mix_norm[l], xf, pre_ffn_norm[l])

        conv_wb = jnp.concatenate([conv_w[l], conv_b[l][None, :]], axis=0).astype(F32)
        act = ffn_up(h, w_up, l, conv_wb, seq)
        g_next = pre_mix_norm[l + 1] if l + 1 < depth else None
        xf, h = ffn_down(act, w_down_b, l, post_ffn_norm[l], xf, g_next)
    return xf.reshape(batch, seq, d_model)
```

```python
import functools
import math

import jax
import jax.numpy as jnp
import numpy as np
from jax import lax
from jax.experimental import pallas as pl
from jax.experimental.pallas import tpu as pltpu

F32 = jnp.float32
BF16 = jnp.bfloat16

RWKV_HEAD_DIM = 64
DIFF_QK_DIM = 64
DIFF_V_DIM = 128
DECAY_LORA = 64
AAA_LORA = 64
MV_LORA = 32
GATE_LORA = 160
CONV_WIDTH = 3
ROPE_THETA = 500000.0
ROPE_DIM = DIFF_QK_DIM // 4
NORM_EPS = 1e-6
GN_EPS = 64e-5
SUBLN_EPS = 1e-5

V7X_LANES = 128
V7X_SUBLANES = 8
V7X_MXU_DIM = 256
V7X_VMEM_LIMIT_BYTES = 56 * 1024 * 1024

SCAN_CHUNK = 64
SCAN_GROUP_HEADS = V7X_MXU_DIM // RWKV_HEAD_DIM
SCAN_GROUP_CH = SCAN_GROUP_HEADS * RWKV_HEAD_DIM
assert SCAN_CHUNK == RWKV_HEAD_DIM

LORA_WA = 128
LORA_GM = 256
LORA_BLOCK = 512
IN_RWKV_TN = 1792
IN_ATTN_TN = 1536
ATTN_VT_ROWS = DIFF_V_DIM + 16


def _pick(n, candidates):
    for c in candidates:
        if n % c == 0:
            return c
    return n


def _params(semantics):
    return pltpu.CompilerParams(dimension_semantics=semantics,
                                vmem_limit_bytes=V7X_VMEM_LIMIT_BYTES)


def _dot(a, b):
    return jnp.dot(a, b, preferred_element_type=F32)


def _dot_nt(a, b):
    return lax.dot_general(a, b, (((1,), (1,)), ((), ())), preferred_element_type=F32)


def _dot_tn(a, b):
    return lax.dot_general(a, b, (((0,), (0,)), ((), ())), preferred_element_type=F32)


def _rms_normed(x_ref, g_ref):
    x = x_ref[...]
    ms = jnp.mean(x * x, axis=-1, keepdims=True)
    return (x * lax.rsqrt(ms + NORM_EPS) * g_ref[...]).astype(BF16)


def _rows_from_prev(x, n, carry):
    rolled = pltpu.roll(x, n, 0)
    row = lax.broadcasted_iota(jnp.int32, x.shape, 0)
    for r in range(n):
        src = V7X_SUBLANES - n + r
        rolled = jnp.where(row == r, carry[src:src + 1, :], rolled)
    return rolled


def _rms_norm_kernel(x_ref, g_ref, o_ref):
    o_ref[...] = _rms_normed(x_ref, g_ref)


def rms_norm_bf16(x, g):
    t, d = x.shape
    tm = _pick(t, (1024, 512, 256, 128))
    return pl.pallas_call(
        _rms_norm_kernel,
        grid=(t // tm,),
        in_specs=[pl.BlockSpec((tm, d), lambda i: (i, 0)), pl.BlockSpec((1, d), lambda i: (0, 0))],
        out_specs=pl.BlockSpec((tm, d), lambda i: (i, 0)),
        out_shape=jax.ShapeDtypeStruct((t, d), BF16),
        compiler_params=_params(("parallel",)),
        name="rms_norm",
    )(x, g.reshape(1, d).astype(F32))


def _in_rwkv_kernel(h_ref, w_ref, mu_ref, o_ref, carry_ref, *, tiles_per_seq):
    i, j = pl.program_id(0), pl.program_id(1)

    @pl.when(i % tiles_per_seq == 0)
    def _():
        carry_ref[j] = jnp.zeros(carry_ref.shape[1:], F32)

    h = h_ref[...]
    tm = h.shape[0]
    for c0 in range(0, o_ref.shape[1], V7X_MXU_DIM):
        cs = slice(c0, c0 + V7X_MXU_DIM)
        p = _dot(h, w_ref[:, cs])
        prev = _rows_from_prev(p, 1, carry_ref[j, :, cs])
        o_ref[:, cs] = p + mu_ref[:, cs] * (prev - p)
        carry_ref[j, :, cs] = p[tm - V7X_SUBLANES:, :]


def in_proj_rwkv(h, w, mu, seq):
    t, d = h.shape
    n = w.shape[1]
    tm = _pick(seq, (1024, 512, 256, 128))
    tn = IN_RWKV_TN
    nj = n // tn
    return pl.pallas_call(
        functools.partial(_in_rwkv_kernel, tiles_per_seq=seq // tm),
        grid=(t // tm, nj),
        in_specs=[
            pl.BlockSpec((tm, d), lambda i, j: (i, 0)),
            pl.BlockSpec((d, tn), lambda i, j: (0, j)),
            pl.BlockSpec((1, tn), lambda i, j: (0, j)),
        ],
        out_specs=pl.BlockSpec((tm, tn), lambda i, j: (i, j)),
        out_shape=jax.ShapeDtypeStruct((t, n), F32),
        scratch_shapes=[pltpu.VMEM((nj, V7X_SUBLANES, tn), F32)],
        compiler_params=_params(("arbitrary", "arbitrary")),
        name="in_proj_rwkv",
    )(h, w, mu.reshape(1, n).astype(F32))


def _in_attn_kernel(h_ref, w_ref, rope_ref, o_ref, *, qk_cols, n_tiles):
    j = pl.program_id(1)
    tn = o_ref.shape[1]
    half = ROPE_DIM // 2
    q_scale = DIFF_QK_DIM ** -0.5 * math.log2(math.e)

    def tile(jt):
        h = h_ref[...]
        tables = {}
        for c0 in range(0, tn, V7X_MXU_DIM):
            col = jt * tn + c0
            p = _dot(h, w_ref[:, c0:c0 + V7X_MXU_DIM])
            if col >= 2 * qk_cols:
                o_ref[:, c0:c0 + V7X_MXU_DIM] = p.astype(o_ref.dtype)
                continue
            scale = q_scale if col < qk_cols else 1.0
            if scale not in tables:
                tables[scale] = [rope_ref[:, k * V7X_LANES:(k + 1) * V7X_LANES] * scale
                                 for k in range(3)]
            cos, sin_lo, sin_hi = tables[scale]
            for g0 in range(0, V7X_MXU_DIM, V7X_LANES):
                xg = p[:, g0:g0 + V7X_LANES]
                rot = (xg * cos + pltpu.roll(xg, V7X_LANES - half, 1) * sin_lo
                       + pltpu.roll(xg, half, 1) * sin_hi)
                o_ref[:, c0 + g0:c0 + g0 + V7X_LANES] = rot.astype(o_ref.dtype)

    for jt in range(n_tiles):
        pl.when(j == jt)(functools.partial(tile, jt))


def in_proj_attn(h, w, rope, seq, qk_cols):
    t, d = h.shape
    n = w.shape[1]
    tm = _pick(seq, (1024, 512, 256, 128))
    tn = IN_ATTN_TN
    assert qk_cols % V7X_MXU_DIM == 0 and n % tn == 0 and tn % V7X_MXU_DIM == 0
    return pl.pallas_call(
        functools.partial(_in_attn_kernel, qk_cols=qk_cols, n_tiles=n // tn),
        grid=(t // tm, n // tn),
        in_specs=[
            pl.BlockSpec((tm, d), lambda i, j: (i, 0)),
            pl.BlockSpec((d, tn), lambda i, j: (0, j)),
            pl.BlockSpec((tm, 3 * V7X_LANES), lambda i, j: (i, 0)),
        ],
        out_specs=pl.BlockSpec((tm, tn), lambda i, j: (i, j)),
        out_shape=jax.ShapeDtypeStruct((t, n), BF16),
        compiler_params=_params(("parallel", "arbitrary")),
        name="in_proj_attn",
    )(h, w, rope)


def _rope_tables(positions):
    half = ROPE_DIM // 2
    inv_freq = ROPE_THETA ** (-jnp.arange(0, ROPE_DIM, 2, dtype=F32) / ROPE_DIM)
    ang = positions.astype(F32).reshape(-1, 1) * inv_freq
    feats = jnp.concatenate([jnp.cos(ang), jnp.sin(ang), jnp.ones((ang.shape[0], 1), F32)], axis=-1)
    select = np.zeros((2 * half + 1, 3 * V7X_LANES), np.float32)
    for lane in range(V7X_LANES):
        ll = lane % DIFF_QK_DIM
        if ll < ROPE_DIM:
            select[ll % half, lane] = 1.0
        else:
            select[2 * half, lane] = 1.0
        if ll < half:
            select[half + ll, V7X_LANES + lane] = -1.0
        elif ll < ROPE_DIM:
            select[half + ll - half, 2 * V7X_LANES + lane] = 1.0
    return jnp.dot(feats, jnp.asarray(select), precision=lax.Precision.HIGHEST)


def _scan_masks():
    c, g, gc = SCAN_CHUNK, SCAN_GROUP_HEADS, SCAN_GROUP_CH
    row = lax.broadcasted_iota(jnp.int32, (g * c, gc), 0)
    col = lax.broadcasted_iota(jnp.int32, (g * c, gc), 1)
    same_block = (row // c) == (col // c)
    rowc = lax.broadcasted_iota(jnp.int32, (c, gc), 0)
    colc = lax.broadcasted_iota(jnp.int32, (c, gc), 1) % c
    strict_cat = rowc > colc
    incl_cat = rowc >= colc
    level0_cat = strict_cat & ((rowc // 2) == (colc // 2))
    eye_cat = jnp.where(rowc == colc, 1.0, 0.0).astype(F32)
    rt = lax.broadcasted_iota(jnp.int32, (c, c), 0)
    ct = lax.broadcasted_iota(jnp.int32, (c, c), 1)
    tril = jnp.where(rt >= ct, 1.0, 0.0).astype(BF16)
    return same_block, strict_cat, incl_cat, level0_cat, eye_cat, tril


def _scan_constants():
    idx = jnp.arange(SCAN_GROUP_HEADS * SCAN_CHUNK)
    t, s = idx[:, None], idx[None, :]
    mats = [t // 2 == s // 2]
    b = 2
    while b < SCAN_CHUNK:
        mats.append((t // (2 * b) == s // (2 * b)) & (t % (2 * b) >= b) & (s % (2 * b) < b))
        b *= 2
    mats.append(t // RWKV_HEAD_DIM == s // RWKV_HEAD_DIM)
    return jnp.stack(mats).astype(BF16)


def _sigmoid(x):
    return 0.5 * jnp.tanh(0.5 * x) + 0.5


def _head_sum(x, ones_bd, split=False):
    hi = x.astype(BF16)
    total = _dot(hi, ones_bd)
    if split:
        total = total + _dot((x - hi.astype(F32)).astype(BF16), ones_bd)
    return total


def _scan_chunk(ops, states, masks, const_ref):
    c, g, gc = SCAN_CHUNK, SCAN_GROUP_HEADS, SCAN_GROUP_CH
    same_block, strict_cat, incl_cat, level0_cat, eye_cat, tril = masks
    n_levels = const_ref.shape[0] - 1
    groups = range(len(ops))

    def tile(x):
        return jnp.concatenate([x] * g, axis=0)

    def bdexp(x):
        return jnp.where(same_block, tile(x), jnp.zeros_like(tile(x)))

    cums = []
    for lw, *_ in ops:
        lw_hi = lw.astype(BF16)
        lw_lo = (lw - lw_hi.astype(F32)).astype(BF16)
        cums.append(_dot(tril, lw_hi) + _dot(tril, lw_lo))

    a_b, b_b, k_b, r_b, end_b, v_b, w_end = [], [], [], [], [], [], []
    for (lw, r, k, v, kk, a), cum in zip(ops, cums):
        w_inc = jnp.exp(cum)
        w_inv = jnp.exp(-cum)
        w_prev = jnp.exp(cum - lw)
        w_last = w_inc[c - 1:c, :]
        b_f = kk * a * w_inv
        k_f = k * w_inv
        a_b.append((-(kk * w_prev)).astype(BF16))
        b_b.append(b_f.astype(BF16))
        k_b.append(k_f.astype(BF16))
        r_b.append((r * w_inc).astype(BF16))
        end_b.append(jnp.concatenate([(b_f * w_last).astype(BF16), (k_f * w_last).astype(BF16)], axis=0))
        v_b.append(v.astype(BF16))
        w_end.append(w_last)

    ar_b = [jnp.concatenate([a_b[i], r_b[i]], axis=0) for i in groups]
    l_cat = [_dot_nt(ar_b[i], jnp.concatenate([bdexp(b_b[i]), bdexp(k_b[i])], axis=0))
             for i in groups]
    l_ab = [jnp.where(strict_cat, lc[:c, :gc], 0.0).astype(BF16) for lc in l_cat]

    l_bd = [bdexp(l) for l in l_ab]
    t_cat = [eye_cat + jnp.where(level0_cat, l, jnp.zeros_like(l)).astype(F32) for l in l_ab]
    for lvl in range(1, n_levels):
        t_b = [t.astype(BF16) for t in t_cat]
        half = [_dot(t_b[i], l_bd[i] * const_ref[lvl]).astype(BF16) for i in groups]
        t_cat = [t_cat[i] + _dot(half[i], bdexp(t_b[i])) for i in groups]
    t_cat = [t.astype(BF16) for t in t_cat]

    l_rb = [jnp.where(incl_cat, lc[c:, :gc], 0.0).astype(BF16) for lc in l_cat]
    l_k = [jnp.concatenate([jnp.where(strict_cat, lc[:c, gc:], 0.0).astype(BF16),
                            jnp.where(incl_cat, lc[c:, gc:], 0.0).astype(BF16)], axis=0)
           for lc in l_cat]
    from_v = [_dot(l_k[i], bdexp(v_b[i])) for i in groups]

    from_state = [_dot_nt(ar_b[i], states[i].astype(BF16)) for i in groups]
    x_b = [(from_state[i][:c] + from_v[i][:c]).astype(BF16) for i in groups]
    u_b = [_dot(t_cat[i], bdexp(x_b[i])).astype(BF16) for i in groups]
    upd = [_dot_tn(jnp.concatenate([u_b[i], v_b[i]], axis=0), end_b[i]) for i in groups]
    new_states = [jnp.where(same_block, states[i] * w_end[i] + upd[i], 0.0) for i in groups]
    outs = [from_state[i][c:] + from_v[i][c:] + _dot(l_rb[i], bdexp(u_b[i])) for i in groups]
    return outs, new_states


_VEC_W0, _VEC_A0, _VEC_KK, _VEC_KA, _VEC_RK, _VEC_GNW, _VEC_GNB, _VEC_V0 = range(8)


def _rwkv_kernel(*refs, n_chunks, n_groups, has_vfirst):
    if has_vfirst:
        (const_ref, vec_ref, w2_ref, a2_ref, g2_ref, v2_ref, r_ref, k_ref, v_ref, lora_ref, vf_ref,
         o_ref, state_ref, lw_s, a_s, kk_s, k2_s, v_s, o_s) = refs
    else:
        (const_ref, vec_ref, w2_ref, a2_ref, g2_ref, r_ref, k_ref, v_ref, lora_ref,
         o_ref, state_ref, lw_s, a_s, kk_s, k2_s, v_s, o_s) = refs
        v2_ref = vf_ref = None

    @pl.when(pl.program_id(1) == 0)
    def _():
        state_ref[...] = jnp.zeros_like(state_ref)

    c, gc = SCAN_CHUNK, SCAN_GROUP_CH
    n_seqs = r_ref.shape[0]
    ones_bd = const_ref[const_ref.shape[0] - 1]
    vec = lambda row, sl: vec_ref[row:row + 1, sl]

    o1 = LORA_WA
    o2 = o1 + LORA_GM

    for bi in range(n_seqs):
        tanh_wl = jnp.tanh(lora_ref[bi, :, 0:o1]).astype(BF16)
        al = lora_ref[bi, :, 0:o1].astype(BF16)
        for gi in range(n_groups):
            sl = slice(gi * gc, (gi + 1) * gc)
            zw = vec(_VEC_W0, sl) + _dot(tanh_wl, w2_ref[:, sl])
            lw_s[bi, :, sl] = -math.exp(-0.5) * _sigmoid(zw)
            a = _sigmoid(vec(_VEC_A0, sl) + _dot(al, a2_ref[:, sl]))
            a_s[bi, :, sl] = a
            v = v_ref[bi, :, sl]
            if has_vfirst:
                mix = _sigmoid(vec(_VEC_V0, sl)
                               + _dot(lora_ref[bi, :, o1:o2].astype(BF16), v2_ref[:, sl]))
                v = v + (vf_ref[bi, :, sl] - v) * mix
            v_s[bi, :, sl] = v
            k = k_ref[bi, :, sl]
            kk = k * vec(_VEC_KK, sl)
            ss = _head_sum(kk * kk, ones_bd)
            kk_s[bi, :, sl] = kk * lax.rsqrt(jnp.maximum(ss, 1e-24))
            k2_s[bi, :, sl] = k * (1.0 + (a - 1.0) * vec(_VEC_KA, sl))

    masks = _scan_masks()
    problems = [(bi, gi) for bi in range(n_seqs) for gi in range(n_groups)]

    def body(ci, carry):
        t0 = pl.multiple_of(ci * c, c)
        sls = [(bi, pl.ds(t0, c), slice(gi * gc, (gi + 1) * gc)) for bi, gi in problems]
        ops = [(lw_s[sl], r_ref[sl], k2_s[sl], v_s[sl], kk_s[sl], a_s[sl]) for sl in sls]
        outs, new_states = _scan_chunk(ops, [state_ref[pi] for pi in range(len(problems))],
                                       masks, const_ref)
        for pi in range(len(problems)):
            o_s[sls[pi]] = outs[pi]
            state_ref[pi] = new_states[pi]
        return carry

    lax.fori_loop(0, n_chunks, body, 0, unroll=True)

    inv_n = 1.0 / RWKV_HEAD_DIM
    for bi in range(n_seqs):
        sig_gl = _sigmoid(lora_ref[bi, :, o1:o2]).astype(BF16)
        for gi in range(n_groups):
            sl = slice(gi * gc, (gi + 1) * gc)
            o = o_s[bi, :, sl]
            dev = o - _head_sum(o, ones_bd) * inv_n
            var = _head_sum(dev * dev, ones_bd) * inv_n
            normed = dev * lax.rsqrt(var + GN_EPS) * vec(_VEC_GNW, sl) + vec(_VEC_GNB, sl)
            bonus = _head_sum(r_ref[bi, :, sl] * k2_s[bi, :, sl] * vec(_VEC_RK, sl), ones_bd,
                              split=True) * v_s[bi, :, sl]
            gate = _dot(sig_gl, g2_ref[:, sl])
            o_ref[bi, :, sl] = ((normed + bonus) * gate).astype(o_ref.dtype)


def rwkv7_mixer(pr, pr_first, vecs, w2p, a2p, g2p, v2p, batch, seq, width):
    t, n = pr.shape
    tb = _pick(seq, (256, 128, 64))
    nb = seq // tb
    n_seqs = 2 if batch % 2 == 0 else 1
    n_groups = width // SCAN_GROUP_CH
    has_vfirst = pr_first is not None
    consts = _scan_constants()
    lora_col = 3 * width // LORA_BLOCK
    row_spec = lambda col: pl.BlockSpec((n_seqs, tb, width), lambda b, s: (b, s, col))
    full = lambda arr: pl.BlockSpec(arr.shape, lambda b, s: (0,) * arr.ndim)
    weights = [w2p, a2p, g2p] + ([v2p] if has_vfirst else [])
    in_specs = ([full(consts), full(vecs)] + [full(w) for w in weights]
                + [row_spec(0), row_spec(1), row_spec(2),
                   pl.BlockSpec((n_seqs, tb, LORA_BLOCK), lambda b, s: (b, s, lora_col))]
                + ([row_spec(2)] if has_vfirst else []))
    pr3 = pr.reshape(batch, seq, n)
    args = ([consts, vecs] + weights + [pr3, pr3, pr3, pr3]
            + ([pr_first.reshape(batch, seq, n)] if has_vfirst else []))
    out = pl.pallas_call(
        functools.partial(_rwkv_kernel, n_chunks=tb // SCAN_CHUNK, n_groups=n_groups,
                          has_vfirst=has_vfirst),
        grid=(batch // n_seqs, nb),
        in_specs=in_specs,
        out_specs=pl.BlockSpec((n_seqs, tb, width), lambda b, s: (b, s, 0)),
        out_shape=jax.ShapeDtypeStruct((batch, seq, width), BF16),
        scratch_shapes=([pltpu.VMEM((n_seqs * n_groups, SCAN_GROUP_CH, SCAN_GROUP_CH), F32)]
                        + [pltpu.VMEM((n_seqs, tb, width), F32)] * 6),
        compiler_params=_params(("parallel", "arbitrary")),
        name="rwkv7_mixer",
    )(*args)
    return out.reshape(t, width)


def _diff_attn_kernel(lam_ref, q_ref, k_ref, v_ref, w_ref, o_ref,
                      vt_ref, m_ref, acc_ref, s0_ref, s1_ref, *, tq, scale_out):
    step = pl.program_id(2)
    n_heads = q_ref.shape[1] // DIFF_V_DIM
    n_tiles = q_ref.shape[0] // tq

    @pl.when(step == 0)
    def _():
        for hh in range(n_heads):
            r0 = hh * ATTN_VT_ROWS
            for c0 in range(0, v_ref.shape[0], tq):
                vt_ref[r0:r0 + DIFF_V_DIM, c0:c0 + tq] = (
                    v_ref[c0:c0 + tq, hh * DIFF_V_DIM:(hh + 1) * DIFF_V_DIM].astype(F32).T.astype(BF16))
            vt_ref[r0 + DIFF_V_DIM:r0 + ATTN_VT_ROWS, :] = jnp.ones(
                (ATTN_VT_ROWS - DIFF_V_DIM, vt_ref.shape[1]), BF16)

    lane = lax.broadcasted_iota(jnp.int32, (tq, DIFF_V_DIM), 1)
    stream_heads = [hh for hh in range(n_heads) for _ in range(2)]

    def q_streams(tile):
        streams = []
        for hh in range(n_heads):
            q = q_ref[tile * tq:(tile + 1) * tq, hh * DIFF_V_DIM:(hh + 1) * DIFF_V_DIM]
            zero = jnp.zeros_like(q)
            streams.append(jnp.where(lane < DIFF_QK_DIM, q, zero))
            streams.append(jnp.where(lane >= DIFF_QK_DIM, q, zero))
        return streams

    tk = tq // 2
    s_bufs = (s0_ref, s1_ref)

    every_query = slice(0, tq)
    late_queries = slice(tk, tq)

    def scores(qs, j, slot, cols=every_query):
        k0 = pl.multiple_of(j * tk, tk)
        for si, (hh, qm) in enumerate(zip(stream_heads, qs)):
            kb = k_ref[pl.ds(k0, tk), hh * DIFF_V_DIM:(hh + 1) * DIFF_V_DIM]
            s_bufs[slot][si, :, cols] = _dot_nt(kb, qm[cols])

    def softmax_pv(j, slot, diag=False, cols=every_query):
        k0 = pl.multiple_of(j * tk, tk)
        for si, hh in enumerate(stream_heads):
            s = s_bufs[slot][si, :, cols]
            if diag:
                kpos = lax.broadcasted_iota(jnp.int32, s.shape, 0)
                qpos = lax.broadcasted_iota(jnp.int32, s.shape, 1)
                s = jnp.where(kpos <= qpos, s, -jnp.inf)
            m_old = m_ref[si, :, cols]
            m_new = jnp.maximum(m_old, jnp.max(s, axis=0, keepdims=True))
            alpha = jnp.exp2(m_old - m_new)
            p = jnp.exp2(s - m_new)
            m_ref[si, :, cols] = m_new
            vtb = vt_ref[hh * ATTN_VT_ROWS:(hh + 1) * ATTN_VT_ROWS, pl.ds(k0, tk)]
            acc_ref[si, :, cols] = acc_ref[si, :, cols] * alpha + _dot(vtb, p.astype(BF16))

    lam = lam_ref[0]

    def finish(tile):
        for hh in range(n_heads):
            a1, a2 = acc_ref[2 * hh], acc_ref[2 * hh + 1]
            nv = DIFF_V_DIM
            o_t = a1[:nv] * (1.0 / a1[nv:nv + 1]) - a2[:nv] * (lam / a2[nv:nv + 1])
            o = o_t.T
            ms = jnp.mean(o * o, axis=-1, keepdims=True)
            o_ref[tile * tq:(tile + 1) * tq, hh * DIFF_V_DIM:(hh + 1) * DIFF_V_DIM] = (
                o * lax.rsqrt(ms + SUBLN_EPS) * w_ref[...] * scale_out).astype(o_ref.dtype)

    qs = q_streams(0)
    scores(qs, 0, 0)
    for tile in range(n_tiles):
        qi = step * n_tiles + tile
        m_ref[...] = jnp.full_like(m_ref, -jnp.inf)
        acc_ref[...] = jnp.zeros_like(acc_ref)

        def body(jj, carry, qs=qs):
            scores(qs, 2 * jj + 1, 1)
            softmax_pv(2 * jj, 0)
            scores(qs, 2 * jj + 2, 0)
            softmax_pv(2 * jj + 1, 1)
            return carry

        lax.fori_loop(0, qi, body, 0)
        scores(qs, 2 * qi + 1, 1, cols=late_queries)
        softmax_pv(2 * qi, 0, diag=True)
        if tile + 1 < n_tiles:
            qs = q_streams(tile + 1)
            scores(qs, 0, 0)
        softmax_pv(2 * qi + 1, 1, diag=True, cols=late_queries)
        finish(tile)


def diff_attention(qkv, lam, subln_w, lambda_init, batch, seq, heads):
    t = qkv.shape[0]
    tq = _pick(seq, (1024, 512, 256, 128))
    nq = seq // tq
    hb = 2 if heads % 2 == 0 else 1
    wb = hb * DIFF_V_DIM
    n_hb = heads // hb
    tiles = _pick(nq, (4, 2))
    steps = nq // tiles
    q_rows = pl.BlockSpec((tiles * tq, wb), lambda b, h, i: (b * steps + i, h))
    return pl.pallas_call(
        functools.partial(_diff_attn_kernel, tq=tq, scale_out=1.0 - lambda_init),
        grid=(batch, n_hb, steps),
        in_specs=[
            pl.BlockSpec(memory_space=pltpu.SMEM),
            q_rows,
            pl.BlockSpec((seq, wb), lambda b, h, i: (b, n_hb + h)),
            pl.BlockSpec((seq, wb), lambda b, h, i: (b, 2 * n_hb + h)),
            pl.BlockSpec((1, DIFF_V_DIM), lambda b, h, i: (0, 0)),
        ],
        out_specs=q_rows,
        out_shape=jax.ShapeDtypeStruct((t, heads * DIFF_V_DIM), BF16),
        scratch_shapes=[
            pltpu.VMEM((hb * ATTN_VT_ROWS, seq), BF16),
            pltpu.VMEM((2 * hb, 1, tq), F32),
            pltpu.VMEM((2 * hb, ATTN_VT_ROWS, tq), F32),
            pltpu.VMEM((2 * hb, tq // 2, tq), F32),
            pltpu.VMEM((2 * hb, tq // 2, tq), F32),
        ],
        compiler_params=_params(("parallel", "parallel", "arbitrary")),
        name="diff_attention",
    )(lam.reshape(1).astype(F32), qkv, qkv, qkv, subln_w.reshape(1, DIFF_V_DIM).astype(F32))


def _residual_norm_epilogue(m, g_ref, res_ref, gn_ref, x_ref, h_ref, rows=slice(None)):
    ms = jnp.mean(m * m, axis=-1, keepdims=True)
    x = res_ref[rows, :] + m * lax.rsqrt(ms + NORM_EPS) * g_ref[...]
    x_ref[rows, :] = x
    if h_ref is not None:
        ms_x = jnp.mean(x * x, axis=-1, keepdims=True)
        h_ref[rows, :] = (x * lax.rsqrt(ms_x + NORM_EPS) * gn_ref[...]).astype(h_ref.dtype)


def _out_proj_kernel(a1_ref, a2_ref, w_ref, g_ref, res_ref, gn_ref, x_ref, h_ref):
    k1 = a1_ref.shape[1]
    tm = a1_ref.shape[0]
    for r0 in range(0, tm, V7X_MXU_DIM):
        rows = slice(r0, min(r0 + V7X_MXU_DIM, tm))
        m = _dot(a1_ref[rows, :], w_ref[0:k1, :]) + _dot(a2_ref[rows, :], w_ref[k1:, :])
        _residual_norm_epilogue(m, g_ref, res_ref, gn_ref, x_ref, h_ref, rows)


def out_proj(a1, a2, w, layer, g, res, g_next):
    t, k1 = a1.shape
    k2 = a2.shape[1]
    d = w.shape[2]
    tm = _pick(t, (512, 256, 128))
    row = pl.BlockSpec((tm, d), lambda i: (i, 0))
    vec = pl.BlockSpec((1, d), lambda i: (0, 0))
    return pl.pallas_call(
        _out_proj_kernel,
        grid=(t // tm,),
        in_specs=[
            pl.BlockSpec((tm, k1), lambda i: (i, 0)),
            pl.BlockSpec((tm, k2), lambda i: (i, 0)),
            pl.BlockSpec((None, k1 + k2, d), lambda i: (layer, 0, 0)),
            vec, row, vec,
        ],
        out_specs=[row, row],
        out_shape=[jax.ShapeDtypeStruct((t, d), F32), jax.ShapeDtypeStruct((t, d), BF16)],
        compiler_params=_params(("parallel",)),
        name="out_proj",
    )(a1, a2, w, g.reshape(1, d).astype(F32), res, g_next.reshape(1, d).astype(F32))


def _ffn_up_kernel(h_ref, wg_ref, wu_ref, cw_ref, o_ref, carry_ref, *, tiles_per_seq):
    i, j = pl.program_id(0), pl.program_id(1)

    @pl.when(i % tiles_per_seq == 0)
    def _():
        carry_ref[j] = jnp.zeros(carry_ref.shape[1:], F32)

    h = h_ref[...]
    tm = h.shape[0]
    chunks = [slice(c0, c0 + V7X_MXU_DIM) for c0 in range(0, o_ref.shape[1], V7X_MXU_DIM)]
    gates = [_dot(h, wg_ref[:, cs].astype(BF16)) for cs in chunks]
    for cs, gate in zip(chunks, gates):
        carry = carry_ref[j, :, cs]
        conv = (_rows_from_prev(gate, 2, carry) * cw_ref[0:1, cs]
                + _rows_from_prev(gate, 1, carry) * cw_ref[1:2, cs]
                + gate * cw_ref[2:3, cs] + cw_ref[3:4, cs])
        carry_ref[j, :, cs] = gate[tm - V7X_SUBLANES:, :]
        inner = math.sqrt(2.0 / math.pi) * (conv + 0.044715 * (conv * conv * conv))
        act = 0.5 * conv * (1.0 + jnp.tanh(inner))
        o_ref[:, cs] = (act * _dot(h, wu_ref[:, cs].astype(BF16))).astype(o_ref.dtype)


def ffn_up(h, w_up, layer, conv_wb, seq):
    t, d = h.shape
    f = w_up.shape[2] // 2
    tm = _pick(seq, (1024, 512, 256, 128))
    tn = _pick(f, (512, 256))
    nj = f // tn
    return pl.pallas_call(
        functools.partial(_ffn_up_kernel, tiles_per_seq=seq // tm),
        grid=(t // tm, nj),
        in_specs=[
            pl.BlockSpec((tm, d), lambda i, j: (i, 0)),
            pl.BlockSpec((None, d, tn), lambda i, j: (layer, 0, j)),
            pl.BlockSpec((None, d, tn), lambda i, j: (layer, 0, j + nj)),
            pl.BlockSpec((CONV_WIDTH + 1, tn), lambda i, j: (0, j)),
        ],
        out_specs=pl.BlockSpec((tm, tn), lambda i, j: (i, j)),
        out_shape=jax.ShapeDtypeStruct((t, f), BF16),
        scratch_shapes=[pltpu.VMEM((nj, V7X_SUBLANES, tn), F32)],
        compiler_params=_params(("arbitrary", "arbitrary")),
        name="ffn_up",
    )(h, w_up, w_up, conv_wb)


def _ffn_down_kernel(a_ref, w_ref, g_ref, res_ref, gn_ref, x_ref, *maybe_h_ref):
    m = _dot(a_ref[...], w_ref[...])
    _residual_norm_epilogue(m, g_ref, res_ref, gn_ref, x_ref, maybe_h_ref[0] if maybe_h_ref else None)


def ffn_down(a, w, layer, g, res, g_next):
    t, kdim = a.shape
    d = w.shape[2]
    tm = _pick(t, (256, 128))
    row = pl.BlockSpec((tm, d), lambda i: (i, 0))
    vec = pl.BlockSpec((1, d), lambda i: (0, 0))
    has_next = g_next is not None
    gn = (g_next if has_next else g).reshape(1, d).astype(F32)
    outs = pl.pallas_call(
        _ffn_down_kernel,
        grid=(t // tm,),
        in_specs=[
            pl.BlockSpec((tm, kdim), lambda i: (i, 0)),
            pl.BlockSpec((None, kdim, d), lambda i: (layer, 0, 0), pipeline_mode=pl.Buffered(1)),
            vec, row, vec,
        ],
        out_specs=[row, row] if has_next else [row],
        out_shape=([jax.ShapeDtypeStruct((t, d), F32)]
                   + ([jax.ShapeDtypeStruct((t, d), BF16)] if has_next else [])),
        compiler_params=_params(("parallel",)),
        name="ffn_down",
    )(a, w, g.reshape(1, d).astype(F32), res, gn)
    return (outs[0], outs[1]) if has_next else (outs[0], None)


def _pad_cols(w, width):
    return jnp.pad(w, [(0, 0)] * (w.ndim - 1) + [(0, width - w.shape[-1])])


def kernel(x, positions, pre_mix_norm, post_mix_norm, pre_ffn_norm, post_ffn_norm, w_in, w_mv_down, shift_mu, shift_mu_mv, w0, w2, a0, a2, g2, k_k, k_a, r_k, gn_w, gn_b, v0, v2, lam_q1, lam_k1, lam_q2, lam_k2, subln_w, w_out, w_up, conv_w, conv_b, w_down):
    batch, seq, d_model = x.shape
    t = batch * seq
    depth = w_in.shape[0]
    width = w0.shape[1]
    diff_heads = (d_model - width) // DIFF_V_DIM
    rwkv_cols = shift_mu.shape[1]
    qk_cols = diff_heads * 2 * DIFF_QK_DIM
    rwkv_n = 3 * width + LORA_BLOCK
    assert rwkv_n % IN_RWKV_TN == 0 and (3 * width) % LORA_BLOCK == 0

    rope = _rope_tables(positions)
    w_out_b = w_out.astype(BF16)
    w_down_b = w_down.astype(BF16)
    xf = x.reshape(t, d_model)
    h = rms_norm_bf16(xf, pre_mix_norm[0])
    pr_first = None
    for l in range(depth):
        o_r, o_wl = 0, width
        o_k = o_wl + DECAY_LORA
        o_v = o_k + width
        o_al = o_v + width
        o_gl = o_al + AAA_LORA
        if l == 0:
            mv_w = jnp.zeros((d_model, MV_LORA), F32)
            mv_mu = jnp.zeros((MV_LORA,), F32)
        else:
            mv_w = w_mv_down[l - 1]
            mv_mu = shift_mu_mv[l - 1]
        arrange = lambda m, mv: jnp.concatenate(
            [m[..., o_r:o_wl], m[..., o_k:o_v], m[..., o_v:o_al],
             m[..., o_wl:o_k], m[..., o_al:o_gl],
             _pad_cols(jnp.concatenate([m[..., o_gl:rwkv_cols], mv], axis=-1), LORA_GM)], axis=-1)
        w_rwkv = _pad_cols(arrange(w_in[l], mv_w), rwkv_n).astype(BF16)
        mu_rwkv = _pad_cols(arrange(shift_mu[l], mv_mu), rwkv_n)
        pr = in_proj_rwkv(h, w_rwkv, mu_rwkv, seq)

        vecs = jnp.stack([w0[l], a0[l], k_k[l], k_a[l], r_k[l].reshape(width), gn_w[l], gn_b[l],
                          v0[l - 1] if l > 0 else jnp.zeros((width,), F32)]).astype(F32)
        lora_rows = lambda w, start, rows: jnp.pad(
            w, ((start, rows - start - w.shape[0]), (0, 0))).astype(BF16)
        w2p = lora_rows(w2[l], 0, LORA_WA)
        a2p = lora_rows(a2[l], DECAY_LORA, LORA_WA)
        g2p = lora_rows(g2[l], 0, LORA_GM)
        v2p = lora_rows(v2[l - 1], GATE_LORA, LORA_GM) if l > 0 else None
        o_rwkv = rwkv7_mixer(pr, pr_first, vecs, w2p, a2p, g2p, v2p, batch, seq, width)
        if l == 0:
            pr_first = pr

        qkv = in_proj_attn(h, w_in[l][:, rwkv_cols:].astype(BF16), rope, seq, qk_cols)
        lambda_init = 0.8 - 0.6 * math.exp(-0.3 * l)
        lam = (jnp.exp(jnp.sum(lam_q1[l] * lam_k1[l])) - jnp.exp(jnp.sum(lam_q2[l] * lam_k2[l]))
               + lambda_init)
        o_diff = diff_attention(qkv, lam, subln_w[l], lambda_init, batch, seq, diff_heads)

        xf, h = out_proj(o_rwkv, o_diff, w_out_b, l, post_mix_norm[l], xf, pre_ffn_norm[l])

        conv_wb = jnp.concatenate([conv_w[l], conv_b[l][None, :]], axis=0).astype(F32)
        act = ffn_up(h, w_up, l, conv_wb, seq)
        g_next = pre_mix_norm[l + 1] if l + 1 < depth else None
        xf, h = ffn_down(act, w_down_b, l, post_ffn_norm[l], xf, g_next)
    return xf.reshape(batch, seq, d_model)
```

```python
import functools
import math

import jax
import jax.numpy as jnp
import numpy as np
from jax import lax
from jax.experimental import pallas as pl
from jax.experimental.pallas import tpu as pltpu

F32 = jnp.float32
BF16 = jnp.bfloat16

RWKV_HEAD_DIM = 64
DIFF_QK_DIM = 64
DIFF_V_DIM = 128
DECAY_LORA = 64
AAA_LORA = 64
MV_LORA = 32
GATE_LORA = 160
CONV_WIDTH = 3
ROPE_THETA = 500000.0
ROPE_DIM = DIFF_QK_DIM // 4
NORM_EPS = 1e-6
GN_EPS = 64e-5
SUBLN_EPS = 1e-5

V7X_LANES = 128
V7X_SUBLANES = 8
V7X_MXU_DIM = 256
V7X_VMEM_LIMIT_BYTES = 56 * 1024 * 1024

SCAN_CHUNK = 64
SCAN_GROUP_HEADS = V7X_MXU_DIM // RWKV_HEAD_DIM
SCAN_GROUP_CH = SCAN_GROUP_HEADS * RWKV_HEAD_DIM
assert SCAN_CHUNK == RWKV_HEAD_DIM

LORA_WA = 128
LORA_GM = 256
LORA_BLOCK = 512
IN_RWKV_TN = 1792
IN_ATTN_TN = 1536
ATTN_VT_ROWS = DIFF_V_DIM + 16


def _pick(n, candidates):
    for c in candidates:
        if n % c == 0:
            return c
    return n


def _params(semantics):
    return pltpu.CompilerParams(dimension_semantics=semantics,
                                vmem_limit_bytes=V7X_VMEM_LIMIT_BYTES)


def _dot(a, b):
    return jnp.dot(a, b, preferred_element_type=F32)


def _dot_nt(a, b):
    return lax.dot_general(a, b, (((1,), (1,)), ((), ())), preferred_element_type=F32)


def _dot_tn(a, b):
    return lax.dot_general(a, b, (((0,), (0,)), ((), ())), preferred_element_type=F32)


def _rms_normed(x_ref, g_ref):
    x = x_ref[...]
    ms = jnp.mean(x * x, axis=-1, keepdims=True)
    return (x * lax.rsqrt(ms + NORM_EPS) * g_ref[...]).astype(BF16)


def _rows_from_prev(x, n, carry):
    rolled = pltpu.roll(x, n, 0)
    row = lax.broadcasted_iota(jnp.int32, x.shape, 0)
    for r in range(n):
        src = V7X_SUBLANES - n + r
        rolled = jnp.where(row == r, carry[src:src + 1, :], rolled)
    return rolled


def _rms_norm_kernel(x_ref, g_ref, o_ref):
    o_ref[...] = _rms_normed(x_ref, g_ref)


def rms_norm_bf16(x, g):
    t, d = x.shape
    tm = _pick(t, (1024, 512, 256, 128))
    return pl.pallas_call(
        _rms_norm_kernel,
        grid=(t // tm,),
        in_specs=[pl.BlockSpec((tm, d), lambda i: (i, 0)), pl.BlockSpec((1, d), lambda i: (0, 0))],
        out_specs=pl.BlockSpec((tm, d), lambda i: (i, 0)),
        out_shape=jax.ShapeDtypeStruct((t, d), BF16),
        compiler_params=_params(("parallel",)),
        name="rms_norm",
    )(x, g.reshape(1, d).astype(F32))


def _in_rwkv_kernel(h_ref, w_ref, mu_ref, o_ref, carry_ref, *, tiles_per_seq):
    i, j = pl.program_id(0), pl.program_id(1)

    @pl.when(i % tiles_per_seq == 0)
    def _():
        carry_ref[j] = jnp.zeros(carry_ref.shape[1:], F32)

    h = h_ref[...]
    tm = h.shape[0]
    for c0 in range(0, o_ref.shape[1], V7X_MXU_DIM):
        cs = slice(c0, c0 + V7X_MXU_DIM)
        p = _dot(h, w_ref[:, cs])
        prev = _rows_from_prev(p, 1, carry_ref[j, :, cs])
        o_ref[:, cs] = p + mu_ref[:, cs] * (prev - p)
        carry_ref[j, :, cs] = p[tm - V7X_SUBLANES:, :]


def in_proj_rwkv(h, w, mu, seq):
    t, d = h.shape
    n = w.shape[1]
    tm = _pick(seq, (1024, 512, 256, 128))
    tn = IN_RWKV_TN
    nj = n // tn
    return pl.pallas_call(
        functools.partial(_in_rwkv_kernel, tiles_per_seq=seq // tm),
        grid=(t // tm, nj),
        in_specs=[
            pl.BlockSpec((tm, d), lambda i, j: (i, 0)),
            pl.BlockSpec((d, tn), lambda i, j: (0, j)),
            pl.BlockSpec((1, tn), lambda i, j: (0, j)),
        ],
        out_specs=pl.BlockSpec((tm, tn), lambda i, j: (i, j)),
        out_shape=jax.ShapeDtypeStruct((t, n), F32),
        scratch_shapes=[pltpu.VMEM((nj, V7X_SUBLANES, tn), F32)],
        compiler_params=_params(("arbitrary", "arbitrary")),
        name="in_proj_rwkv",
    )(h, w, mu.reshape(1, n).astype(F32))


def _in_attn_kernel(h_ref, w_ref, rope_ref, o_ref, *, qk_cols, n_tiles):
    j = pl.program_id(1)
    tn = o_ref.shape[1]
    half = ROPE_DIM // 2
    q_scale = DIFF_QK_DIM ** -0.5 * math.log2(math.e)

    def tile(jt):
        h = h_ref[...]
        tables = {}
        for c0 in range(0, tn, V7X_MXU_DIM):
            col = jt * tn + c0
            p = _dot(h, w_ref[:, c0:c0 + V7X_MXU_DIM])
            if col >= 2 * qk_cols:
                o_ref[:, c0:c0 + V7X_MXU_DIM] = p.astype(o_ref.dtype)
                continue
            scale = q_scale if col < qk_cols else 1.0
            if scale not in tables:
                tables[scale] = [rope_ref[:, k * V7X_LANES:(k + 1) * V7X_LANES] * scale
                                 for k in range(3)]
            cos, sin_lo, sin_hi = tables[scale]
            for g0 in range(0, V7X_MXU_DIM, V7X_LANES):
                xg = p[:, g0:g0 + V7X_LANES]
                rot = (xg * cos + pltpu.roll(xg, V7X_LANES - half, 1) * sin_lo
                       + pltpu.roll(xg, half, 1) * sin_hi)
                o_ref[:, c0 + g0:c0 + g0 + V7X_LANES] = rot.astype(o_ref.dtype)

    for jt in range(n_tiles):
        pl.when(j == jt)(functools.partial(tile, jt))


def in_proj_attn(h, w, rope, seq, qk_cols):
    t, d = h.shape
    n = w.shape[1]
    tm = _pick(seq, (1024, 512, 256, 128))
    tn = IN_ATTN_TN
    assert qk_cols % V7X_MXU_DIM == 0 and n % tn == 0 and tn % V7X_MXU_DIM == 0
    return pl.pallas_call(
        functools.partial(_in_attn_kernel, qk_cols=qk_cols, n_tiles=n // tn),
        grid=(t // tm, n // tn),
        in_specs=[
            pl.BlockSpec((tm, d), lambda i, j: (i, 0)),
            pl.BlockSpec((d, tn), lambda i, j: (0, j)),
            pl.BlockSpec((tm, 3 * V7X_LANES), lambda i, j: (i, 0)),
        ],
        out_specs=pl.BlockSpec((tm, tn), lambda i, j: (i, j)),
        out_shape=jax.ShapeDtypeStruct((t, n), BF16),
        compiler_params=_params(("parallel", "arbitrary")),
        name="in_proj_attn",
    )(h, w, rope)


def _rope_tables(positions):
    half = ROPE_DIM // 2
    inv_freq = ROPE_THETA ** (-jnp.arange(0, ROPE_DIM, 2, dtype=F32) / ROPE_DIM)
    ang = positions.astype(F32).reshape(-1, 1) * inv_freq
    feats = jnp.concatenate([jnp.cos(ang), jnp.sin(ang), jnp.ones((ang.shape[0], 1), F32)], axis=-1)
    select = np.zeros((2 * half + 1, 3 * V7X_LANES), np.float32)
    for lane in range(V7X_LANES):
        ll = lane % DIFF_QK_DIM
        if ll < ROPE_DIM:
            select[ll % half, lane] = 1.0
        else:
            select[2 * half, lane] = 1.0
        if ll < half:
            select[half + ll, V7X_LANES + lane] = -1.0
        elif ll < ROPE_DIM:
            select[half + ll - half, 2 * V7X_LANES + lane] = 1.0
    return jnp.dot(feats, jnp.asarray(select), precision=lax.Precision.HIGHEST)


def _scan_masks():
    c, g, gc = SCAN_CHUNK, SCAN_GROUP_HEADS, SCAN_GROUP_CH
    row = lax.broadcasted_iota(jnp.int32, (g * c, gc), 0)
    col = lax.broadcasted_iota(jnp.int32, (g * c, gc), 1)
    same_block = (row // c) == (col // c)
    rowc = lax.broadcasted_iota(jnp.int32, (c, gc), 0)
    colc = lax.broadcasted_iota(jnp.int32, (c, gc), 1) % c
    strict_cat = rowc > colc
    incl_cat = rowc >= colc
    level0_cat = strict_cat & ((rowc // 2) == (colc // 2))
    eye_cat = jnp.where(rowc == colc, 1.0, 0.0).astype(F32)
    rt = lax.broadcasted_iota(jnp.int32, (c, c), 0)
    ct = lax.broadcasted_iota(jnp.int32, (c, c), 1)
    tril = jnp.where(rt >= ct, 1.0, 0.0).astype(BF16)
    return same_block, strict_cat, incl_cat, level0_cat, eye_cat, tril


def _scan_constants():
    idx = jnp.arange(SCAN_GROUP_HEADS * SCAN_CHUNK)
    t, s = idx[:, None], idx[None, :]
    mats = [t // 2 == s // 2]
    b = 2
    while b < SCAN_CHUNK:
        mats.append((t // (2 * b) == s // (2 * b)) & (t % (2 * b) >= b) & (s % (2 * b) < b))
        b *= 2
    mats.append(t // RWKV_HEAD_DIM == s // RWKV_HEAD_DIM)
    return jnp.stack(mats).astype(BF16)


def _sigmoid(x):
    return 0.5 * jnp.tanh(0.5 * x) + 0.5


def _head_sum(x, ones_bd, split=False):
    hi = x.astype(BF16)
    total = _dot(hi, ones_bd)
    if split:
        total = total + _dot((x - hi.astype(F32)).astype(BF16), ones_bd)
    return total


def _scan_chunk(ops, states, masks, const_ref):
    c, g, gc = SCAN_CHUNK, SCAN_GROUP_HEADS, SCAN_GROUP_CH
    same_block, strict_cat, incl_cat, level0_cat, eye_cat, tril = masks
    n_levels = const_ref.shape[0] - 1
    groups = range(len(ops))

    def tile(x):
        return jnp.concatenate([x] * g, axis=0)

    def bdexp(x):
        return jnp.where(same_block, tile(x), jnp.zeros_like(tile(x)))

    cums = []
    for lw, *_ in ops:
        lw_hi = lw.astype(BF16)
        lw_lo = (lw - lw_hi.astype(F32)).astype(BF16)
        cums.append(_dot(tril, lw_hi) + _dot(tril, lw_lo))

    a_b, b_b, k_b, r_b, end_b, v_b, w_end = [], [], [], [], [], [], []
    for (lw, r, k, v, kk, a), cum in zip(ops, cums):
        w_inc = jnp.exp(cum)
        w_inv = jnp.exp(-cum)
        w_prev = jnp.exp(cum - lw)
        w_last = w_inc[c - 1:c, :]
        b_f = kk * a * w_inv
        k_f = k * w_inv
        a_b.append((-(kk * w_prev)).astype(BF16))
        b_b.append(b_f.astype(BF16))
        k_b.append(k_f.astype(BF16))
        r_b.append((r * w_inc).astype(BF16))
        end_b.append(jnp.concatenate([(b_f * w_last).astype(BF16), (k_f * w_last).astype(BF16)], axis=0))
        v_b.append(v.astype(BF16))
        w_end.append(w_last)

    ar_b = [jnp.concatenate([a_b[i], r_b[i]], axis=0) for i in groups]
    l_cat = [_dot_nt(ar_b[i], jnp.concatenate([bdexp(b_b[i]), bdexp(k_b[i])], axis=0))
             for i in groups]
    l_ab = [jnp.where(strict_cat, lc[:c, :gc], 0.0).astype(BF16) for lc in l_cat]

    l_bd = [bdexp(l) for l in l_ab]
    t_cat = [eye_cat + jnp.where(level0_cat, l, jnp.zeros_like(l)).astype(F32) for l in l_ab]
    for lvl in range(1, n_levels):
        t_b = [t.astype(BF16) for t in t_cat]
        half = [_dot(t_b[i], l_bd[i] * const_ref[lvl]).astype(BF16) for i in groups]
        t_cat = [t_cat[i] + _dot(half[i], bdexp(t_b[i])) for i in groups]
    t_cat = [t.astype(BF16) for t in t_cat]

    l_rb = [jnp.where(incl_cat, lc[c:, :gc], 0.0).astype(BF16) for lc in l_cat]
    l_k = [jnp.concatenate([jnp.where(strict_cat, lc[:c, gc:], 0.0).astype(BF16),
                            jnp.where(incl_cat, lc[c:, gc:], 0.0).astype(BF16)], axis=0)
           for lc in l_cat]
    from_v = [_dot(l_k[i], bdexp(v_b[i])) for i in groups]

    from_state = [_dot_nt(ar_b[i], states[i].astype(BF16)) for i in groups]
    x_b = [(from_state[i][:c] + from_v[i][:c]).astype(BF16) for i in groups]
    u_b = [_dot(t_cat[i], bdexp(x_b[i])).astype(BF16) for i in groups]
    upd = [_dot_tn(jnp.concatenate([u_b[i], v_b[i]], axis=0), end_b[i]) for i in groups]
    new_states = [jnp.where(same_block, states[i] * w_end[i] + upd[i], 0.0) for i in groups]
    outs = [from_state[i][c:] + from_v[i][c:] + _dot(l_rb[i], bdexp(u_b[i])) for i in groups]
    return outs, new_states


_VEC_W0, _VEC_A0, _VEC_KK, _VEC_KA, _VEC_RK, _VEC_GNW, _VEC_GNB, _VEC_V0 = range(8)


def _rwkv_kernel(*refs, n_chunks, n_groups, has_vfirst):
    if has_vfirst:
        (const_ref, vec_ref, w2_ref, a2_ref, g2_ref, v2_ref, r_ref, k_ref, v_ref, lora_ref, vf_ref,
         o_ref, state_ref, lw_s, a_s, kk_s, k2_s, v_s, o_s) = refs
    else:
        (const_ref, vec_ref, w2_ref, a2_ref, g2_ref, r_ref, k_ref, v_ref, lora_ref,
         o_ref, state_ref, lw_s, a_s, kk_s, k2_s, v_s, o_s) = refs
        v2_ref = vf_ref = None

    @pl.when(pl.program_id(1) == 0)
    def _():
        state_ref[...] = jnp.zeros_like(state_ref)

    c, gc = SCAN_CHUNK, SCAN_GROUP_CH
    n_seqs = r_ref.shape[0]
    ones_bd = const_ref[const_ref.shape[0] - 1]
    vec = lambda row, sl: vec_ref[row:row + 1, sl]

    o1 = LORA_WA
    o2 = o1 + LORA_GM

    for bi in range(n_seqs):
        tanh_wl = jnp.tanh(lora_ref[bi, :, 0:o1]).astype(BF16)
        al = lora_ref[bi, :, 0:o1].astype(BF16)
        for gi in range(n_groups):
            sl = slice(gi * gc, (gi + 1) * gc)
            zw = vec(_VEC_W0, sl) + _dot(tanh_wl, w2_ref[:, sl])
            lw_s[bi, :, sl] = -math.exp(-0.5) * _sigmoid(zw)
            a = _sigmoid(vec(_VEC_A0, sl) + _dot(al, a2_ref[:, sl]))
            a_s[bi, :, sl] = a
            v = v_ref[bi, :, sl]
            if has_vfirst:
                mix = _sigmoid(vec(_VEC_V0, sl)
                               + _dot(lora_ref[bi, :, o1:o2].astype(BF16), v2_ref[:, sl]))
                v = v + (vf_ref[bi, :, sl] - v) * mix
            v_s[bi, :, sl] = v
            k = k_ref[bi, :, sl]
            kk = k * vec(_VEC_KK, sl)
            ss = _head_sum(kk * kk, ones_bd)
            kk_s[bi, :, sl] = kk * lax.rsqrt(jnp.maximum(ss, 1e-24))
            k2_s[bi, :, sl] = k * (1.0 + (a - 1.0) * vec(_VEC_KA, sl))

    masks = _scan_masks()
    problems = [(bi, gi) for bi in range(n_seqs) for gi in range(n_groups)]

    def body(ci, carry):
        t0 = pl.multiple_of(ci * c, c)
        sls = [(bi, pl.ds(t0, c), slice(gi * gc, (gi + 1) * gc)) for bi, gi in problems]
        ops = [(lw_s[sl], r_ref[sl], k2_s[sl], v_s[sl], kk_s[sl], a_s[sl]) for sl in sls]
        outs, new_states = _scan_chunk(ops, [state_ref[pi] for pi in range(len(problems))],
                                       masks, const_ref)
        for pi in range(len(problems)):
            o_s[sls[pi]] = outs[pi]
            state_ref[pi] = new_states[pi]
        return carry

    lax.fori_loop(0, n_chunks, body, 0, unroll=True)

    inv_n = 1.0 / RWKV_HEAD_DIM
    for bi in range(n_seqs):
        sig_gl = _sigmoid(lora_ref[bi, :, o1:o2]).astype(BF16)
        for gi in range(n_groups):
            sl = slice(gi * gc, (gi + 1) * gc)
            o = o_s[bi, :, sl]
            dev = o - _head_sum(o, ones_bd) * inv_n
            var = _head_sum(dev * dev, ones_bd) * inv_n
            normed = dev * lax.rsqrt(var + GN_EPS) * vec(_VEC_GNW, sl) + vec(_VEC_GNB, sl)
            bonus = _head_sum(r_ref[bi, :, sl] * k2_s[bi, :, sl] * vec(_VEC_RK, sl), ones_bd,
                              split=True) * v_s[bi, :, sl]
            gate = _dot(sig_gl, g2_ref[:, sl])
            o_ref[bi, :, sl] = ((normed + bonus) * gate).astype(o_ref.dtype)


def rwkv7_mixer(pr, pr_first, vecs, w2p, a2p, g2p, v2p, batch, seq, width):
    t, n = pr.shape
    tb = _pick(seq, (256, 128, 64))
    nb = seq // tb
    n_seqs = 2 if batch % 2 == 0 else 1
    n_groups = width // SCAN_GROUP_CH
    has_vfirst = pr_first is not None
    consts = _scan_constants()
    lora_col = 3 * width // LORA_BLOCK
    row_spec = lambda col: pl.BlockSpec((n_seqs, tb, width), lambda b, s: (b, s, col))
    full = lambda arr: pl.BlockSpec(arr.shape, lambda b, s: (0,) * arr.ndim)
    weights = [w2p, a2p, g2p] + ([v2p] if has_vfirst else [])
    in_specs = ([full(consts), full(vecs)] + [full(w) for w in weights]
                + [row_spec(0), row_spec(1), row_spec(2),
                   pl.BlockSpec((n_seqs, tb, LORA_BLOCK), lambda b, s: (b, s, lora_col))]
                + ([row_spec(2)] if has_vfirst else []))
    pr3 = pr.reshape(batch, seq, n)
    args = ([consts, vecs] + weights + [pr3, pr3, pr3, pr3]
            + ([pr_first.reshape(batch, seq, n)] if has_vfirst else []))
    out = pl.pallas_call(
        functools.partial(_rwkv_kernel, n_chunks=tb // SCAN_CHUNK, n_groups=n_groups,
                          has_vfirst=has_vfirst),
        grid=(batch // n_seqs, nb),
        in_specs=in_specs,
        out_specs=pl.BlockSpec((n_seqs, tb, width), lambda b, s: (b, s, 0)),
        out_shape=jax.ShapeDtypeStruct((batch, seq, width), BF16),
        scratch_shapes=([pltpu.VMEM((n_seqs * n_groups, SCAN_GROUP_CH, SCAN_GROUP_CH), F32)]
                        + [pltpu.VMEM((n_seqs, tb, width), F32)] * 6),
        compiler_params=_params(("parallel", "arbitrary")),
        name="rwkv7_mixer",
    )(*args)
    return out.reshape(t, width)


def _diff_attn_kernel(lam_ref, q_ref, k_ref, v_ref, w_ref, o_ref,
                      vt_ref, m_ref, acc_ref, s0_ref, s1_ref, *, tq, scale_out, single_step):
    step = pl.program_id(2)
    n_heads = q_ref.shape[1] // DIFF_V_DIM
    n_tiles = q_ref.shape[0] // tq

    def build_vt():
        for hh in range(n_heads):
            r0 = hh * ATTN_VT_ROWS
            for c0 in range(0, v_ref.shape[0], tq):
                vt_ref[r0:r0 + DIFF_V_DIM, c0:c0 + tq] = (
                    v_ref[c0:c0 + tq, hh * DIFF_V_DIM:(hh + 1) * DIFF_V_DIM].astype(F32).T.astype(BF16))
            vt_ref[r0 + DIFF_V_DIM:r0 + ATTN_VT_ROWS, :] = jnp.ones(
                (ATTN_VT_ROWS - DIFF_V_DIM, vt_ref.shape[1]), BF16)

    lane = lax.broadcasted_iota(jnp.int32, (tq, DIFF_V_DIM), 1)
    stream_heads = [hh for hh in range(n_heads) for _ in range(2)]

    def q_streams(tile):
        streams = []
        for hh in range(n_heads):
            q = q_ref[tile * tq:(tile + 1) * tq, hh * DIFF_V_DIM:(hh + 1) * DIFF_V_DIM]
            zero = jnp.zeros_like(q)
            streams.append(jnp.where(lane < DIFF_QK_DIM, q, zero))
            streams.append(jnp.where(lane >= DIFF_QK_DIM, q, zero))
        return streams

    tk = tq // 2
    s_bufs = (s0_ref, s1_ref)

    every_query = slice(0, tq)
    late_queries = slice(tk, tq)

    def scores(qs, j, slot, cols=every_query):
        k0 = pl.multiple_of(j * tk, tk)
        for si, (hh, qm) in enumerate(zip(stream_heads, qs)):
            kb = k_ref[pl.ds(k0, tk), hh * DIFF_V_DIM:(hh + 1) * DIFF_V_DIM]
            s_bufs[slot][si, :, cols] = _dot_nt(kb, qm[cols])

    def softmax_pv(j, slot, diag=False, cols=every_query):
        k0 = pl.multiple_of(j * tk, tk)
        for si, hh in enumerate(stream_heads):
            s = s_bufs[slot][si, :, cols]
            if diag:
                kpos = lax.broadcasted_iota(jnp.int32, s.shape, 0)
                qpos = lax.broadcasted_iota(jnp.int32, s.shape, 1)
                s = jnp.where(kpos <= qpos, s, -jnp.inf)
            m_old = m_ref[si, :, cols]
            m_new = jnp.maximum(m_old, jnp.max(s, axis=0, keepdims=True))
            alpha = jnp.exp2(m_old - m_new)
            p = jnp.exp2(s - m_new)
            m_ref[si, :, cols] = m_new
            vtb = vt_ref[hh * ATTN_VT_ROWS:(hh + 1) * ATTN_VT_ROWS, pl.ds(k0, tk)]
            acc_ref[si, :, cols] = acc_ref[si, :, cols] * alpha + _dot(vtb, p.astype(BF16))

    lam = lam_ref[0]

    def finish(tile):
        for hh in range(n_heads):
            a1, a2 = acc_ref[2 * hh], acc_ref[2 * hh + 1]
            nv = DIFF_V_DIM
            o_t = a1[:nv] * (1.0 / a1[nv:nv + 1]) - a2[:nv] * (lam / a2[nv:nv + 1])
            o = o_t.T
            ms = jnp.mean(o * o, axis=-1, keepdims=True)
            o_ref[tile * tq:(tile + 1) * tq, hh * DIFF_V_DIM:(hh + 1) * DIFF_V_DIM] = (
                o * lax.rsqrt(ms + SUBLN_EPS) * w_ref[...] * scale_out).astype(o_ref.dtype)

    qs = q_streams(0)
    scores(qs, 0, 0)
    if single_step:
        build_vt()
    else:
        pl.when(step == 0)(build_vt)
    for tile in range(n_tiles):
        qi = step * n_tiles + tile
        m_ref[...] = jnp.full_like(m_ref, -jnp.inf)
        acc_ref[...] = jnp.zeros_like(acc_ref)

        def body(jj, carry, qs=qs):
            scores(qs, 2 * jj + 1, 1)
            softmax_pv(2 * jj, 0)
            scores(qs, 2 * jj + 2, 0)
            softmax_pv(2 * jj + 1, 1)
            return carry

        lax.fori_loop(0, qi, body, 0)
        scores(qs, 2 * qi + 1, 1, cols=late_queries)
        softmax_pv(2 * qi, 0, diag=True)
        if tile + 1 < n_tiles:
            qs = q_streams(tile + 1)
            scores(qs, 0, 0)
        softmax_pv(2 * qi + 1, 1, diag=True, cols=late_queries)
        finish(tile)


def diff_attention(qkv, lam, subln_w, lambda_init, batch, seq, heads):
    t = qkv.shape[0]
    tq = _pick(seq, (1024, 512, 256, 128))
    nq = seq // tq
    hb = 2 if heads % 2 == 0 else 1
    wb = hb * DIFF_V_DIM
    n_hb = heads // hb
    tiles = _pick(nq, (4, 2))
    steps = nq // tiles
    q_rows = pl.BlockSpec((tiles * tq, wb), lambda b, h, i: (b * steps + i, h))
    return pl.pallas_call(
        functools.partial(_diff_attn_kernel, tq=tq, scale_out=1.0 - lambda_init,
                          single_step=steps == 1),
        grid=(batch, n_hb, steps),
        in_specs=[
            pl.BlockSpec(memory_space=pltpu.SMEM),
            q_rows,
            pl.BlockSpec((seq, wb), lambda b, h, i: (b, n_hb + h)),
            pl.BlockSpec((seq, wb), lambda b, h, i: (b, 2 * n_hb + h)),
            pl.BlockSpec((1, DIFF_V_DIM), lambda b, h, i: (0, 0)),
        ],
        out_specs=q_rows,
        out_shape=jax.ShapeDtypeStruct((t, heads * DIFF_V_DIM), BF16),
        scratch_shapes=[
            pltpu.VMEM((hb * ATTN_VT_ROWS, seq), BF16),
            pltpu.VMEM((2 * hb, 1, tq), F32),
            pltpu.VMEM((2 * hb, ATTN_VT_ROWS, tq), F32),
            pltpu.VMEM((2 * hb, tq // 2, tq), F32),
            pltpu.VMEM((2 * hb, tq // 2, tq), F32),
        ],
        compiler_params=_params(("parallel", "parallel", "arbitrary")),
        name="diff_attention",
    )(lam.reshape(1).astype(F32), qkv, qkv, qkv, subln_w.reshape(1, DIFF_V_DIM).astype(F32))


def _residual_norm_epilogue(m, g_ref, res_ref, gn_ref, x_ref, h_ref, rows=slice(None)):
    ms = jnp.mean(m * m, axis=-1, keepdims=True)
    x = res_ref[rows, :] + m * lax.rsqrt(ms + NORM_EPS) * g_ref[...]
    x_ref[rows, :] = x
    if h_ref is not None:
        ms_x = jnp.mean(x * x, axis=-1, keepdims=True)
        h_ref[rows, :] = (x * lax.rsqrt(ms_x + NORM_EPS) * gn_ref[...]).astype(h_ref.dtype)


def _out_proj_kernel(a1_ref, a2_ref, w_ref, g_ref, res_ref, gn_ref, x_ref, h_ref):
    k1 = a1_ref.shape[1]
    tm = a1_ref.shape[0]
    for r0 in range(0, tm, V7X_MXU_DIM):
        rows = slice(r0, min(r0 + V7X_MXU_DIM, tm))
        m = _dot(a1_ref[rows, :], w_ref[0:k1, :]) + _dot(a2_ref[rows, :], w_ref[k1:, :])
        _residual_norm_epilogue(m, g_ref, res_ref, gn_ref, x_ref, h_ref, rows)


def out_proj(a1, a2, w, layer, g, res, g_next):
    t, k1 = a1.shape
    k2 = a2.shape[1]
    d = w.shape[2]
    tm = _pick(t, (512, 256, 128))
    row = pl.BlockSpec((tm, d), lambda i: (i, 0))
    vec = pl.BlockSpec((1, d), lambda i: (0, 0))
    return pl.pallas_call(
        _out_proj_kernel,
        grid=(t // tm,),
        in_specs=[
            pl.BlockSpec((tm, k1), lambda i: (i, 0)),
            pl.BlockSpec((tm, k2), lambda i: (i, 0)),
            pl.BlockSpec((None, k1 + k2, d), lambda i: (layer, 0, 0)),
            vec, row, vec,
        ],
        out_specs=[row, row],
        out_shape=[jax.ShapeDtypeStruct((t, d), F32), jax.ShapeDtypeStruct((t, d), BF16)],
        compiler_params=_params(("parallel",)),
        name="out_proj",
    )(a1, a2, w, g.reshape(1, d).astype(F32), res, g_next.reshape(1, d).astype(F32))


def _ffn_up_kernel(h_ref, wg_ref, wu_ref, cw_ref, o_ref, carry_ref, *, tiles_per_seq):
    i, j = pl.program_id(0), pl.program_id(1)

    @pl.when(i % tiles_per_seq == 0)
    def _():
        carry_ref[j] = jnp.zeros(carry_ref.shape[1:], F32)

    h = h_ref[...]
    tm = h.shape[0]
    chunks = [slice(c0, c0 + V7X_MXU_DIM) for c0 in range(0, o_ref.shape[1], V7X_MXU_DIM)]
    gates = [_dot(h, wg_ref[:, cs].astype(BF16)) for cs in chunks]
    for cs, gate in zip(chunks, gates):
        carry = carry_ref[j, :, cs]
        conv = (_rows_from_prev(gate, 2, carry) * cw_ref[0:1, cs]
                + _rows_from_prev(gate, 1, carry) * cw_ref[1:2, cs]
                + gate * cw_ref[2:3, cs] + cw_ref[3:4, cs])
        carry_ref[j, :, cs] = gate[tm - V7X_SUBLANES:, :]
        inner = math.sqrt(2.0 / math.pi) * (conv + 0.044715 * (conv * conv * conv))
        act = 0.5 * conv * (1.0 + jnp.tanh(inner))
        o_ref[:, cs] = (act * _dot(h, wu_ref[:, cs].astype(BF16))).astype(o_ref.dtype)


def ffn_up(h, w_up, layer, conv_wb, seq):
    t, d = h.shape
    f = w_up.shape[2] // 2
    tm = _pick(seq, (1024, 512, 256, 128))
    tn = _pick(f, (512, 256))
    nj = f // tn
    return pl.pallas_call(
        functools.partial(_ffn_up_kernel, tiles_per_seq=seq // tm),
        grid=(t // tm, nj),
        in_specs=[
            pl.BlockSpec((tm, d), lambda i, j: (i, 0)),
            pl.BlockSpec((None, d, tn), lambda i, j: (layer, 0, j)),
            pl.BlockSpec((None, d, tn), lambda i, j: (layer, 0, j + nj)),
            pl.BlockSpec((CONV_WIDTH + 1, tn), lambda i, j: (0, j)),
        ],
        out_specs=pl.BlockSpec((tm, tn), lambda i, j: (i, j)),
        out_shape=jax.ShapeDtypeStruct((t, f), BF16),
        scratch_shapes=[pltpu.VMEM((nj, V7X_SUBLANES, tn), F32)],
        compiler_params=_params(("arbitrary", "arbitrary")),
        name="ffn_up",
    )(h, w_up, w_up, conv_wb)


def _ffn_down_kernel(a_ref, w_ref, g_ref, res_ref, gn_ref, x_ref, *maybe_h_ref):
    m = _dot(a_ref[...], w_ref[...])
    _residual_norm_epilogue(m, g_ref, res_ref, gn_ref, x_ref, maybe_h_ref[0] if maybe_h_ref else None)


def ffn_down(a, w, layer, g, res, g_next):
    t, kdim = a.shape
    d = w.shape[2]
    tm = _pick(t, (256, 128))
    row = pl.BlockSpec((tm, d), lambda i: (i, 0))
    vec = pl.BlockSpec((1, d), lambda i: (0, 0))
    has_next = g_next is not None
    gn = (g_next if has_next else g).reshape(1, d).astype(F32)
    outs = pl.pallas_call(
        _ffn_down_kernel,
        grid=(t // tm,),
        in_specs=[
            pl.BlockSpec((tm, kdim), lambda i: (i, 0)),
            pl.BlockSpec((None, kdim, d), lambda i: (layer, 0, 0), pipeline_mode=pl.Buffered(1)),
            vec, row, vec,
        ],
        out_specs=[row, row] if has_next else [row],
        out_shape=([jax.ShapeDtypeStruct((t, d), F32)]
                   + ([jax.ShapeDtypeStruct((t, d), BF16)] if has_next else [])),
        compiler_params=_params(("parallel",)),
        name="ffn_down",
    )(a, w, g.reshape(1, d).astype(F32), res, gn)
    return (outs[0], outs[1]) if has_next else (outs[0], None)


def _pad_cols(w, width):
    return jnp.pad(w, [(0, 0)] * (w.ndim - 1) + [(0, width - w.shape[-1])])


def kernel(x, positions, pre_mix_norm, post_mix_norm, pre_ffn_norm, post_ffn_norm, w_in, w_mv_down, shift_mu, shift_mu_mv, w0, w2, a0, a2, g2, k_k, k_a, r_k, gn_w, gn_b, v0, v2, lam_q1, lam_k1, lam_q2, lam_k2, subln_w, w_out, w_up, conv_w, conv_b, w_down):
    batch, seq, d_model = x.shape
    t = batch * seq
    depth = w_in.shape[0]
    width = w0.shape[1]
    diff_heads = (d_model - width) // DIFF_V_DIM
    rwkv_cols = shift_mu.shape[1]
    qk_cols = diff_heads * 2 * DIFF_QK_DIM
    rwkv_n = 3 * width + LORA_BLOCK
    assert rwkv_n % IN_RWKV_TN == 0 and (3 * width) % LORA_BLOCK == 0

    rope = _rope_tables(positions)
    w_out_b = w_out.astype(BF16)
    w_down_b = w_down.astype(BF16)
    xf = x.reshape(t, d_model)
    h = rms_norm_bf16(xf, pre_mix_norm[0])
    pr_first = None
    for l in range(depth):
        o_r, o_wl = 0, width
        o_k = o_wl + DECAY_LORA
        o_v = o_k + width
        o_al = o_v + width
        o_gl = o_al + AAA_LORA
        if l == 0:
            mv_w = jnp.zeros((d_model, MV_LORA), F32)
            mv_mu = jnp.zeros((MV_LORA,), F32)
        else:
            mv_w = w_mv_down[l - 1]
            mv_mu = shift_mu_mv[l - 1]
        arrange = lambda m, mv: jnp.concatenate(
            [m[..., o_r:o_wl], m[..., o_k:o_v], m[..., o_v:o_al],
             m[..., o_wl:o_k], m[..., o_al:o_gl],
             _pad_cols(jnp.concatenate([m[..., o_gl:rwkv_cols], mv], axis=-1), LORA_GM)], axis=-1)
        w_rwkv = _pad_cols(arrange(w_in[l], mv_w), rwkv_n).astype(BF16)
        mu_rwkv = _pad_cols(arrange(shift_mu[l], mv_mu), rwkv_n)
        pr = in_proj_rwkv(h, w_rwkv, mu_rwkv, seq)

        vecs = jnp.stack([w0[l], a0[l], k_k[l], k_a[l], r_k[l].reshape(width), gn_w[l], gn_b[l],
                          v0[l - 1] if l > 0 else jnp.zeros((width,), F32)]).astype(F32)
        lora_rows = lambda w, start, rows: jnp.pad(
            w, ((start, rows - start - w.shape[0]), (0, 0))).astype(BF16)
        w2p = lora_rows(w2[l], 0, LORA_WA)
        a2p = lora_rows(a2[l], DECAY_LORA, LORA_WA)
        g2p = lora_rows(g2[l], 0, LORA_GM)
        v2p = lora_rows(v2[l - 1], GATE_LORA, LORA_GM) if l > 0 else None
        o_rwkv = rwkv7_mixer(pr, pr_first, vecs, w2p, a2p, g2p, v2p, batch, seq, width)
        if l == 0:
            pr_first = pr

        qkv = in_proj_attn(h, w_in[l][:, rwkv_cols:].astype(BF16), rope, seq, qk_cols)
        lambda_init = 0.8 - 0.6 * math.exp(-0.3 * l)
        lam = (jnp.exp(jnp.sum(lam_q1[l] * lam_k1[l])) - jnp.exp(jnp.sum(lam_q2[l] * lam_k2[l]))
               + lambda_init)
        o_diff = diff_attention(qkv, lam, subln_w[l], lambda_init, batch, seq, diff_heads)

        xf, h = out_proj(o_rwkv, o_diff, w_out_b, l, post_mix_norm[l], xf, pre_ffn_norm[l])

        conv_wb = jnp.concatenate([conv_w[l], conv_b[l][None, :]], axis=0).astype(F32)
        act = ffn_up(h, w_up, l, conv_wb, seq)
        g_next = pre_mix_norm[l + 1] if l + 1 < depth else None
        xf, h = ffn_down(act, w_down_b, l, post_ffn_norm[l], xf, g_next)
    return xf.reshape(batch, seq, d_model)
```

```python
import functools
import math

import jax
import jax.numpy as jnp
import numpy as np
from jax import lax
from jax.experimental import pallas as pl
from jax.experimental.pallas import tpu as pltpu

F32 = jnp.float32
BF16 = jnp.bfloat16

RWKV_HEAD_DIM = 64
DIFF_QK_DIM = 64
DIFF_V_DIM = 128
DECAY_LORA = 64
AAA_LORA = 64
MV_LORA = 32
GATE_LORA = 160
CONV_WIDTH = 3
ROPE_THETA = 500000.0
ROPE_DIM = DIFF_QK_DIM // 4
NORM_EPS = 1e-6
GN_EPS = 64e-5
SUBLN_EPS = 1e-5

V7X_LANES = 128
V7X_SUBLANES = 8
V7X_MXU_DIM = 256
V7X_VMEM_LIMIT_BYTES = 56 * 1024 * 1024

SCAN_CHUNK = 64
SCAN_GROUP_HEADS = V7X_MXU_DIM // RWKV_HEAD_DIM
SCAN_GROUP_CH = SCAN_GROUP_HEADS * RWKV_HEAD_DIM
assert SCAN_CHUNK == RWKV_HEAD_DIM

LORA_WA = 128
LORA_GM = 256
LORA_BLOCK = 512
IN_RWKV_TN = 1792
IN_ATTN_TN = 1536
ATTN_VT_ROWS = DIFF_V_DIM + 16


def _pick(n, candidates):
    for c in candidates:
        if n % c == 0:
            return c
    return n


def _params(semantics):
    return pltpu.CompilerParams(dimension_semantics=semantics,
                                vmem_limit_bytes=V7X_VMEM_LIMIT_BYTES)


def _dot(a, b):
    return jnp.dot(a, b, preferred_element_type=F32)


def _dot_nt(a, b):
    return lax.dot_general(a, b, (((1,), (1,)), ((), ())), preferred_element_type=F32)


def _dot_tn(a, b):
    return lax.dot_general(a, b, (((0,), (0,)), ((), ())), preferred_element_type=F32)


def _rms_normed(x_ref, g_ref):
    x = x_ref[...]
    ms = jnp.mean(x * x, axis=-1, keepdims=True)
    return (x * lax.rsqrt(ms + NORM_EPS) * g_ref[...]).astype(BF16)


def _rows_from_prev(x, n, carry):
    rolled = pltpu.roll(x, n, 0)
    row = lax.broadcasted_iota(jnp.int32, x.shape, 0)
    for r in range(n):
        src = V7X_SUBLANES - n + r
        rolled = jnp.where(row == r, carry[src:src + 1, :], rolled)
    return rolled


def _rms_norm_kernel(x_ref, g_ref, o_ref):
    o_ref[...] = _rms_normed(x_ref, g_ref)


def rms_norm_bf16(x, g):
    t, d = x.shape
    tm = _pick(t, (1024, 512, 256, 128))
    return pl.pallas_call(
        _rms_norm_kernel,
        grid=(t // tm,),
        in_specs=[pl.BlockSpec((tm, d), lambda i: (i, 0)), pl.BlockSpec((1, d), lambda i: (0, 0))],
        out_specs=pl.BlockSpec((tm, d), lambda i: (i, 0)),
        out_shape=jax.ShapeDtypeStruct((t, d), BF16),
        compiler_params=_params(("parallel",)),
        name="rms_norm",
    )(x, g.reshape(1, d).astype(F32))


def _in_rwkv_kernel(h_ref, w_ref, mu_ref, o_ref, carry_ref, *, tiles_per_seq):
    i, j = pl.program_id(0), pl.program_id(1)

    @pl.when(i % tiles_per_seq == 0)
    def _():
        carry_ref[j] = jnp.zeros(carry_ref.shape[1:], F32)

    h = h_ref[...]
    tm = h.shape[0]
    for c0 in range(0, o_ref.shape[1], V7X_MXU_DIM):
        cs = slice(c0, c0 + V7X_MXU_DIM)
        p = _dot(h, w_ref[:, cs])
        prev = _rows_from_prev(p, 1, carry_ref[j, :, cs])
        o_ref[:, cs] = p + mu_ref[:, cs] * (prev - p)
        carry_ref[j, :, cs] = p[tm - V7X_SUBLANES:, :]


def in_proj_rwkv(h, w, mu, seq):
    t, d = h.shape
    n = w.shape[1]
    tm = _pick(seq, (1024, 512, 256, 128))
    tn = IN_RWKV_TN
    nj = n // tn
    return pl.pallas_call(
        functools.partial(_in_rwkv_kernel, tiles_per_seq=seq // tm),
        grid=(t // tm, nj),
        in_specs=[
            pl.BlockSpec((tm, d), lambda i, j: (i, 0)),
            pl.BlockSpec((d, tn), lambda i, j: (0, j)),
            pl.BlockSpec((1, tn), lambda i, j: (0, j)),
        ],
        out_specs=pl.BlockSpec((tm, tn), lambda i, j: (i, j)),
        out_shape=jax.ShapeDtypeStruct((t, n), F32),
        scratch_shapes=[pltpu.VMEM((nj, V7X_SUBLANES, tn), F32)],
        compiler_params=_params(("arbitrary", "arbitrary")),
        name="in_proj_rwkv",
    )(h, w, mu.reshape(1, n).astype(F32))


def _in_attn_kernel(h_ref, w_ref, rope_ref, o_ref, *, qk_cols, n_tiles):
    j = pl.program_id(1)
    tn = o_ref.shape[1]
    half = ROPE_DIM // 2
    q_scale = DIFF_QK_DIM ** -0.5 * math.log2(math.e)

    def tile(jt):
        h = h_ref[...]
        tables = {}
        for c0 in range(0, tn, V7X_MXU_DIM):
            col = jt * tn + c0
            p = _dot(h, w_ref[:, c0:c0 + V7X_MXU_DIM])
            if col >= 2 * qk_cols:
                o_ref[:, c0:c0 + V7X_MXU_DIM] = p.astype(o_ref.dtype)
                continue
            scale = q_scale if col < qk_cols else 1.0
            if scale not in tables:
                tables[scale] = [rope_ref[:, k * V7X_LANES:(k + 1) * V7X_LANES] * scale
                                 for k in range(3)]
            cos, sin_lo, sin_hi = tables[scale]
            for g0 in range(0, V7X_MXU_DIM, V7X_LANES):
                xg = p[:, g0:g0 + V7X_LANES]
                rot = (xg * cos + pltpu.roll(xg, V7X_LANES - half, 1) * sin_lo
                       + pltpu.roll(xg, half, 1) * sin_hi)
                o_ref[:, c0 + g0:c0 + g0 + V7X_LANES] = rot.astype(o_ref.dtype)

    for jt in range(n_tiles):
        pl.when(j == jt)(functools.partial(tile, jt))


def in_proj_attn(h, w, rope, seq, qk_cols):
    t, d = h.shape
    n = w.shape[1]
    tm = _pick(seq, (1024, 512, 256, 128))
    tn = IN_ATTN_TN
    assert qk_cols % V7X_MXU_DIM == 0 and n % tn == 0 and tn % V7X_MXU_DIM == 0
    return pl.pallas_call(
        functools.partial(_in_attn_kernel, qk_cols=qk_cols, n_tiles=n // tn),
        grid=(t // tm, n // tn),
        in_specs=[
            pl.BlockSpec((tm, d), lambda i, j: (i, 0)),
            pl.BlockSpec((d, tn), lambda i, j: (0, j)),
            pl.BlockSpec((tm, 3 * V7X_LANES), lambda i, j: (i, 0)),
        ],
        out_specs=pl.BlockSpec((tm, tn), lambda i, j: (i, j)),
        out_shape=jax.ShapeDtypeStruct((t, n), BF16),
        compiler_params=_params(("parallel", "arbitrary")),
        name="in_proj_attn",
    )(h, w, rope)


def _rope_tables(positions):
    half = ROPE_DIM // 2
    inv_freq = ROPE_THETA ** (-jnp.arange(0, ROPE_DIM, 2, dtype=F32) / ROPE_DIM)
    ang = positions.astype(F32).reshape(-1, 1) * inv_freq
    feats = jnp.concatenate([jnp.cos(ang), jnp.sin(ang), jnp.ones((ang.shape[0], 1), F32)], axis=-1)
    select = np.zeros((2 * half + 1, 3 * V7X_LANES), np.float32)
    for lane in range(V7X_LANES):
        ll = lane % DIFF_QK_DIM
        if ll < ROPE_DIM:
            select[ll % half, lane] = 1.0
        else:
            select[2 * half, lane] = 1.0
        if ll < half:
            select[half + ll, V7X_LANES + lane] = -1.0
        elif ll < ROPE_DIM:
            select[half + ll - half, 2 * V7X_LANES + lane] = 1.0
    return jnp.dot(feats, jnp.asarray(select), precision=lax.Precision.HIGHEST)


def _scan_masks():
    c, g, gc = SCAN_CHUNK, SCAN_GROUP_HEADS, SCAN_GROUP_CH
    row = lax.broadcasted_iota(jnp.int32, (g * c, gc), 0)
    col = lax.broadcasted_iota(jnp.int32, (g * c, gc), 1)
    same_block = (row // c) == (col // c)
    rowc = lax.broadcasted_iota(jnp.int32, (c, gc), 0)
    colc = lax.broadcasted_iota(jnp.int32, (c, gc), 1) % c
    strict_cat = rowc > colc
    incl_cat = rowc >= colc
    level0_cat = strict_cat & ((rowc // 2) == (colc // 2))
    eye_cat = jnp.where(rowc == colc, 1.0, 0.0).astype(F32)
    rt = lax.broadcasted_iota(jnp.int32, (c, c), 0)
    ct = lax.broadcasted_iota(jnp.int32, (c, c), 1)
    tril = jnp.where(rt >= ct, 1.0, 0.0).astype(BF16)
    return same_block, strict_cat, incl_cat, level0_cat, eye_cat, tril


def _scan_constants():
    idx = jnp.arange(SCAN_GROUP_HEADS * SCAN_CHUNK)
    t, s = idx[:, None], idx[None, :]
    mats = [t // 2 == s // 2]
    b = 2
    while b < SCAN_CHUNK:
        mats.append((t // (2 * b) == s // (2 * b)) & (t % (2 * b) >= b) & (s % (2 * b) < b))
        b *= 2
    mats.append(t // RWKV_HEAD_DIM == s // RWKV_HEAD_DIM)
    return jnp.stack(mats).astype(BF16)


def _sigmoid(x):
    return 0.5 * jnp.tanh(0.5 * x) + 0.5


def _head_sum(x, ones_bd, split=False):
    hi = x.astype(BF16)
    total = _dot(hi, ones_bd)
    if split:
        total = total + _dot((x - hi.astype(F32)).astype(BF16), ones_bd)
    return total


def _scan_chunk(ops, states, masks, const_ref):
    c, g, gc = SCAN_CHUNK, SCAN_GROUP_HEADS, SCAN_GROUP_CH
    same_block, strict_cat, incl_cat, level0_cat, eye_cat, tril = masks
    n_levels = const_ref.shape[0] - 1
    groups = range(len(ops))

    def tile(x):
        return jnp.concatenate([x] * g, axis=0)

    def bdexp(x):
        return jnp.where(same_block, tile(x), jnp.zeros_like(tile(x)))

    cums = []
    for lw, *_ in ops:
        lw_hi = lw.astype(BF16)
        lw_lo = (lw - lw_hi.astype(F32)).astype(BF16)
        cums.append(_dot(tril, lw_hi) + _dot(tril, lw_lo))

    a_b, b_b, k_b, r_b, end_b, v_b, w_end = [], [], [], [], [], [], []
    for (lw, r, k, v, kk, a), cum in zip(ops, cums):
        w_inc = jnp.exp(cum)
        w_inv = jnp.exp(-cum)
        w_prev = jnp.exp(cum - lw)
        w_last = w_inc[c - 1:c, :]
        b_f = kk * a * w_inv
        k_f = k * w_inv
        a_b.append((-(kk * w_prev)).astype(BF16))
        b_b.append(b_f.astype(BF16))
        k_b.append(k_f.astype(BF16))
        r_b.append((r * w_inc).astype(BF16))
        end_b.append(jnp.concatenate([(b_f * w_last).astype(BF16), (k_f * w_last).astype(BF16)], axis=0))
        v_b.append(v.astype(BF16))
        w_end.append(w_last)

    ar_b = [jnp.concatenate([a_b[i], r_b[i]], axis=0) for i in groups]
    l_cat = [_dot_nt(ar_b[i], jnp.concatenate([bdexp(b_b[i]), bdexp(k_b[i])], axis=0))
             for i in groups]
    l_ab = [jnp.where(strict_cat, lc[:c, :gc], 0.0).astype(BF16) for lc in l_cat]

    l_bd = [bdexp(l) for l in l_ab]
    t_cat = [eye_cat + jnp.where(level0_cat, l, jnp.zeros_like(l)).astype(F32) for l in l_ab]
    for lvl in range(1, n_levels):
        t_b = [t.astype(BF16) for t in t_cat]
        half = [_dot(t_b[i], l_bd[i] * const_ref[lvl]).astype(BF16) for i in groups]
        t_cat = [t_cat[i] + _dot(half[i], bdexp(t_b[i])) for i in groups]
    t_cat = [t.astype(BF16) for t in t_cat]

    l_rb = [jnp.where(incl_cat, lc[c:, :gc], 0.0).astype(BF16) for lc in l_cat]
    l_k = [jnp.concatenate([jnp.where(strict_cat, lc[:c, gc:], 0.0).astype(BF16),
                            jnp.where(incl_cat, lc[c:, gc:], 0.0).astype(BF16)], axis=0)
           for lc in l_cat]
    from_v = [_dot(l_k[i], bdexp(v_b[i])) for i in groups]

    from_state = [_dot_nt(ar_b[i], states[i].astype(BF16)) for i in groups]
    x_b = [(from_state[i][:c] + from_v[i][:c]).astype(BF16) for i in groups]
    u_b = [_dot(t_cat[i], bdexp(x_b[i])).astype(BF16) for i in groups]
    upd = [_dot_tn(jnp.concatenate([u_b[i], v_b[i]], axis=0), end_b[i]) for i in groups]
    new_states = [jnp.where(same_block, states[i] * w_end[i] + upd[i], 0.0) for i in groups]
    outs = [from_state[i][c:] + from_v[i][c:] + _dot(l_rb[i], bdexp(u_b[i])) for i in groups]
    return outs, new_states


_VEC_W0, _VEC_A0, _VEC_KK, _VEC_KA, _VEC_RK, _VEC_GNW, _VEC_GNB, _VEC_V0 = range(8)


def _rwkv_kernel(*refs, n_chunks, n_groups, has_vfirst):
    if has_vfirst:
        (const_ref, vec_ref, w2_ref, a2_ref, g2_ref, v2_ref, r_ref, k_ref, v_ref, lora_ref, vf_ref,
         o_ref, state_ref, lw_s, a_s, kk_s, k2_s, v_s, o_s) = refs
    else:
        (const_ref, vec_ref, w2_ref, a2_ref, g2_ref, r_ref, k_ref, v_ref, lora_ref,
         o_ref, state_ref, lw_s, a_s, kk_s, k2_s, v_s, o_s) = refs
        v2_ref = vf_ref = None

    @pl.when(pl.program_id(1) == 0)
    def _():
        state_ref[...] = jnp.zeros_like(state_ref)

    c, gc = SCAN_CHUNK, SCAN_GROUP_CH
    n_seqs = r_ref.shape[0]
    ones_bd = const_ref[const_ref.shape[0] - 1]
    vec = lambda row, sl: vec_ref[row:row + 1, sl]

    o1 = LORA_WA
    o2 = o1 + LORA_GM

    for bi in range(n_seqs):
        tanh_wl = jnp.tanh(lora_ref[bi, :, 0:o1]).astype(BF16)
        al = lora_ref[bi, :, 0:o1].astype(BF16)
        for gi in range(n_groups):
            sl = slice(gi * gc, (gi + 1) * gc)
            zw = vec(_VEC_W0, sl) + _dot(tanh_wl, w2_ref[:, sl])
            lw_s[bi, :, sl] = -math.exp(-0.5) * _sigmoid(zw)
            a = _sigmoid(vec(_VEC_A0, sl) + _dot(al, a2_ref[:, sl]))
            a_s[bi, :, sl] = a
            v = v_ref[bi, :, sl]
            if has_vfirst:
                mix = _sigmoid(vec(_VEC_V0, sl)
                               + _dot(lora_ref[bi, :, o1:o2].astype(BF16), v2_ref[:, sl]))
                v = v + (vf_ref[bi, :, sl] - v) * mix
            v_s[bi, :, sl] = v
            k = k_ref[bi, :, sl]
            kk = k * vec(_VEC_KK, sl)
            ss = _head_sum(kk * kk, ones_bd)
            kk_s[bi, :, sl] = kk * lax.rsqrt(jnp.maximum(ss, 1e-24))
            k2_s[bi, :, sl] = k * (1.0 + (a - 1.0) * vec(_VEC_KA, sl))

    masks = _scan_masks()
    problems = [(bi, gi) for bi in range(n_seqs) for gi in range(n_groups)]

    def body(ci, carry):
        t0 = pl.multiple_of(ci * c, c)
        sls = [(bi, pl.ds(t0, c), slice(gi * gc, (gi + 1) * gc)) for bi, gi in problems]
        ops = [(lw_s[sl], r_ref[sl], k2_s[sl], v_s[sl], kk_s[sl], a_s[sl]) for sl in sls]
        outs, new_states = _scan_chunk(ops, [state_ref[pi] for pi in range(len(problems))],
                                       masks, const_ref)
        for pi in range(len(problems)):
            o_s[sls[pi]] = outs[pi]
            state_ref[pi] = new_states[pi]
        return carry

    lax.fori_loop(0, n_chunks, body, 0, unroll=True)

    inv_n = 1.0 / RWKV_HEAD_DIM
    for bi in range(n_seqs):
        sig_gl = _sigmoid(lora_ref[bi, :, o1:o2]).astype(BF16)
        for gi in range(n_groups):
            sl = slice(gi * gc, (gi + 1) * gc)
            o = o_s[bi, :, sl]
            dev = o - _head_sum(o, ones_bd) * inv_n
            var = _head_sum(dev * dev, ones_bd) * inv_n
            normed = dev * lax.rsqrt(var + GN_EPS) * vec(_VEC_GNW, sl) + vec(_VEC_GNB, sl)
            bonus = _head_sum(r_ref[bi, :, sl] * k2_s[bi, :, sl] * vec(_VEC_RK, sl), ones_bd,
                              split=True) * v_s[bi, :, sl]
            gate = _dot(sig_gl, g2_ref[:, sl])
            o_ref[bi, :, sl] = ((normed + bonus) * gate).astype(o_ref.dtype)


def rwkv7_mixer(pr, pr_first, vecs, w2p, a2p, g2p, v2p, batch, seq, width):
    t, n = pr.shape
    tb = _pick(seq, (256, 128, 64))
    nb = seq // tb
    n_seqs = 2 if batch % 2 == 0 else 1
    n_groups = width // SCAN_GROUP_CH
    has_vfirst = pr_first is not None
    consts = _scan_constants()
    lora_col = 3 * width // LORA_BLOCK
    row_spec = lambda col: pl.BlockSpec((n_seqs, tb, width), lambda b, s: (b, s, col))
    full = lambda arr: pl.BlockSpec(arr.shape, lambda b, s: (0,) * arr.ndim)
    weights = [w2p, a2p, g2p] + ([v2p] if has_vfirst else [])
    in_specs = ([full(consts), full(vecs)] + [full(w) for w in weights]
                + [row_spec(0), row_spec(1), row_spec(2),
                   pl.BlockSpec((n_seqs, tb, LORA_BLOCK), lambda b, s: (b, s, lora_col))]
                + ([row_spec(2)] if has_vfirst else []))
    pr3 = pr.reshape(batch, seq, n)
    args = ([consts, vecs] + weights + [pr3, pr3, pr3, pr3]
            + ([pr_first.reshape(batch, seq, n)] if has_vfirst else []))
    out = pl.pallas_call(
        functools.partial(_rwkv_kernel, n_chunks=tb // SCAN_CHUNK, n_groups=n_groups,
                          has_vfirst=has_vfirst),
        grid=(batch // n_seqs, nb),
        in_specs=in_specs,
        out_specs=pl.BlockSpec((n_seqs, tb, width), lambda b, s: (b, s, 0)),
        out_shape=jax.ShapeDtypeStruct((batch, seq, width), BF16),
        scratch_shapes=([pltpu.VMEM((n_seqs * n_groups, SCAN_GROUP_CH, SCAN_GROUP_CH), F32)]
                        + [pltpu.VMEM((n_seqs, tb, width), F32)] * 6),
        compiler_params=_params(("parallel", "arbitrary")),
        name="rwkv7_mixer",
    )(*args)
    return out.reshape(t, width)


def _diff_attn_kernel(lam_ref, q_ref, k_ref, v_ref, w_ref, o_ref,
                      vt_ref, m_ref, acc_ref, s0_ref, s1_ref, *, tq, scale_out, single_step):
    step = pl.program_id(2)
    n_heads = q_ref.shape[1] // DIFF_V_DIM
    n_tiles = q_ref.shape[0] // tq

    def build_vt():
        for hh in range(n_heads):
            r0 = hh * ATTN_VT_ROWS
            for c0 in range(0, v_ref.shape[0], tq):
                vt_ref[r0:r0 + DIFF_V_DIM, c0:c0 + tq] = (
                    v_ref[c0:c0 + tq, hh * DIFF_V_DIM:(hh + 1) * DIFF_V_DIM].astype(F32).T.astype(BF16))
            vt_ref[r0 + DIFF_V_DIM:r0 + ATTN_VT_ROWS, :] = jnp.ones(
                (ATTN_VT_ROWS - DIFF_V_DIM, vt_ref.shape[1]), BF16)

    lane = lax.broadcasted_iota(jnp.int32, (tq, DIFF_V_DIM), 1)
    stream_heads = [hh for hh in range(n_heads) for _ in range(2)]

    def q_streams(tile):
        streams = []
        for hh in range(n_heads):
            q = q_ref[tile * tq:(tile + 1) * tq, hh * DIFF_V_DIM:(hh + 1) * DIFF_V_DIM]
            zero = jnp.zeros_like(q)
            streams.append(jnp.where(lane < DIFF_QK_DIM, q, zero))
            streams.append(jnp.where(lane >= DIFF_QK_DIM, q, zero))
        return streams

    tk = tq // 2
    s_bufs = (s0_ref, s1_ref)

    every_query = slice(0, tq)
    late_queries = slice(tk, tq)

    def scores(qs, j, slot, cols=every_query):
        k0 = pl.multiple_of(j * tk, tk)
        for si, (hh, qm) in enumerate(zip(stream_heads, qs)):
            kb = k_ref[pl.ds(k0, tk), hh * DIFF_V_DIM:(hh + 1) * DIFF_V_DIM]
            s_bufs[slot][si, :, cols] = _dot_nt(kb, qm[cols])

    def softmax_pv(j, slot, diag=False, cols=every_query):
        k0 = pl.multiple_of(j * tk, tk)
        for si, hh in enumerate(stream_heads):
            s = s_bufs[slot][si, :, cols]
            if diag:
                kpos = lax.broadcasted_iota(jnp.int32, s.shape, 0)
                qpos = lax.broadcasted_iota(jnp.int32, s.shape, 1)
                s = jnp.where(kpos <= qpos, s, -jnp.inf)
            m_old = m_ref[si, :, cols]
            m_new = jnp.maximum(m_old, jnp.max(s, axis=0, keepdims=True))
            alpha = jnp.exp2(m_old - m_new)
            p = jnp.exp2(s - m_new)
            m_ref[si, :, cols] = m_new
            vtb = vt_ref[hh * ATTN_VT_ROWS:(hh + 1) * ATTN_VT_ROWS, pl.ds(k0, tk)]
            acc_ref[si, :, cols] = acc_ref[si, :, cols] * alpha + _dot(vtb, p.astype(BF16))

    lam = lam_ref[0]

    def finish(tile):
        for hh in range(n_heads):
            a1, a2 = acc_ref[2 * hh], acc_ref[2 * hh + 1]
            nv = DIFF_V_DIM
            o_t = a1[:nv] * (1.0 / a1[nv:nv + 1]) - a2[:nv] * (lam / a2[nv:nv + 1])
            o = o_t.T
            ms = jnp.mean(o * o, axis=-1, keepdims=True)
            o_ref[tile * tq:(tile + 1) * tq, hh * DIFF_V_DIM:(hh + 1) * DIFF_V_DIM] = (
                o * lax.rsqrt(ms + SUBLN_EPS) * w_ref[...] * scale_out).astype(o_ref.dtype)

    qs = q_streams(0)
    scores(qs, 0, 0)
    if single_step:
        build_vt()
    else:
        pl.when(step == 0)(build_vt)
    for tile in range(n_tiles):
        qi = step * n_tiles + tile
        m_ref[...] = jnp.full_like(m_ref, -jnp.inf)
        acc_ref[...] = jnp.zeros_like(acc_ref)

        def body(jj, carry, qs=qs):
            scores(qs, 2 * jj + 1, 1)
            softmax_pv(2 * jj, 0)
            scores(qs, 2 * jj + 2, 0)
            softmax_pv(2 * jj + 1, 1)
            return carry

        lax.fori_loop(0, qi, body, 0)
        scores(qs, 2 * qi + 1, 1, cols=late_queries)
        softmax_pv(2 * qi, 0, diag=True)
        if tile + 1 < n_tiles:
            qs = q_streams(tile + 1)
            scores(qs, 0, 0)
        softmax_pv(2 * qi + 1, 1, diag=True, cols=late_queries)
        finish(tile)


def diff_attention(qkv, lam, subln_w, lambda_init, batch, seq, heads):
    t = qkv.shape[0]
    tq = _pick(seq, (1024, 512, 256, 128))
    nq = seq // tq
    hb = 2 if heads % 2 == 0 else 1
    wb = hb * DIFF_V_DIM
    n_hb = heads // hb
    tiles = _pick(nq, (4, 2))
    steps = nq // tiles
    q_rows = pl.BlockSpec((tiles * tq, wb), lambda b, h, i: (b * steps + i, h))
    return pl.pallas_call(
        functools.partial(_diff_attn_kernel, tq=tq, scale_out=1.0 - lambda_init,
                          single_step=steps == 1),
        grid=(batch, n_hb, steps),
        in_specs=[
            pl.BlockSpec(memory_space=pltpu.SMEM),
            q_rows,
            pl.BlockSpec((seq, wb), lambda b, h, i: (b, n_hb + h)),
            pl.BlockSpec((seq, wb), lambda b, h, i: (b, 2 * n_hb + h)),
            pl.BlockSpec((1, DIFF_V_DIM), lambda b, h, i: (0, 0)),
        ],
        out_specs=q_rows,
        out_shape=jax.ShapeDtypeStruct((t, heads * DIFF_V_DIM), BF16),
        scratch_shapes=[
            pltpu.VMEM((hb * ATTN_VT_ROWS, seq), BF16),
            pltpu.VMEM((2 * hb, 1, tq), F32),
            pltpu.VMEM((2 * hb, ATTN_VT_ROWS, tq), F32),
            pltpu.VMEM((2 * hb, tq // 2, tq), F32),
            pltpu.VMEM((2 * hb, tq // 2, tq), F32),
        ],
        compiler_params=_params(("parallel", "parallel", "arbitrary")),
        name="diff_attention",
    )(lam.reshape(1).astype(F32), qkv, qkv, qkv, subln_w.reshape(1, DIFF_V_DIM).astype(F32))


def _residual_norm_epilogue(m, g_ref, res_ref, gn_ref, x_ref, h_ref, rows=slice(None)):
    ms = jnp.mean(m * m, axis=-1, keepdims=True)
    x = res_ref[rows, :] + m * lax.rsqrt(ms + NORM_EPS) * g_ref[...]
    x_ref[rows, :] = x
    if h_ref is not None:
        ms_x = jnp.mean(x * x, axis=-1, keepdims=True)
        h_ref[rows, :] = (x * lax.rsqrt(ms_x + NORM_EPS) * gn_ref[...]).astype(h_ref.dtype)


def _out_proj_kernel(a1_ref, a2_ref, w_ref, g_ref, res_ref, gn_ref, x_ref, h_ref):
    k1 = a1_ref.shape[1]
    tm = a1_ref.shape[0]
    chunk = V7X_MXU_DIM // 2
    for r0 in range(0, tm, chunk):
        rows = slice(r0, min(r0 + chunk, tm))
        m = _dot(a1_ref[rows, :], w_ref[0:k1, :]) + _dot(a2_ref[rows, :], w_ref[k1:, :])
        _residual_norm_epilogue(m, g_ref, res_ref, gn_ref, x_ref, h_ref, rows)


def out_proj(a1, a2, w, layer, g, res, g_next):
    t, k1 = a1.shape
    k2 = a2.shape[1]
    d = w.shape[2]
    tm = _pick(t, (512, 256, 128))
    row = pl.BlockSpec((tm, d), lambda i: (i, 0))
    vec = pl.BlockSpec((1, d), lambda i: (0, 0))
    return pl.pallas_call(
        _out_proj_kernel,
        grid=(t // tm,),
        in_specs=[
            pl.BlockSpec((tm, k1), lambda i: (i, 0)),
            pl.BlockSpec((tm, k2), lambda i: (i, 0)),
            pl.BlockSpec((None, k1 + k2, d), lambda i: (layer, 0, 0)),
            vec, row, vec,
        ],
        out_specs=[row, row],
        out_shape=[jax.ShapeDtypeStruct((t, d), F32), jax.ShapeDtypeStruct((t, d), BF16)],
        compiler_params=_params(("parallel",)),
        name="out_proj",
    )(a1, a2, w, g.reshape(1, d).astype(F32), res, g_next.reshape(1, d).astype(F32))


def _ffn_up_kernel(h_ref, wg_ref, wu_ref, cw_ref, o_ref, carry_ref, *, tiles_per_seq):
    i, j = pl.program_id(0), pl.program_id(1)

    @pl.when(i % tiles_per_seq == 0)
    def _():
        carry_ref[j] = jnp.zeros(carry_ref.shape[1:], F32)

    h = h_ref[...]
    tm = h.shape[0]
    chunks = [slice(c0, c0 + V7X_MXU_DIM) for c0 in range(0, o_ref.shape[1], V7X_MXU_DIM)]
    gates = [_dot(h, wg_ref[:, cs].astype(BF16)) for cs in chunks]
    for cs, gate in zip(chunks, gates):
        carry = carry_ref[j, :, cs]
        conv = (_rows_from_prev(gate, 2, carry) * cw_ref[0:1, cs]
                + _rows_from_prev(gate, 1, carry) * cw_ref[1:2, cs]
                + gate * cw_ref[2:3, cs] + cw_ref[3:4, cs])
        carry_ref[j, :, cs] = gate[tm - V7X_SUBLANES:, :]
        inner = math.sqrt(2.0 / math.pi) * (conv + 0.044715 * (conv * conv * conv))
        act = 0.5 * conv * (1.0 + jnp.tanh(inner))
        o_ref[:, cs] = (act * _dot(h, wu_ref[:, cs].astype(BF16))).astype(o_ref.dtype)


def ffn_up(h, w_up, layer, conv_wb, seq):
    t, d = h.shape
    f = w_up.shape[2] // 2
    tm = _pick(seq, (1024, 512, 256, 128))
    tn = _pick(f, (512, 256))
    nj = f // tn
    return pl.pallas_call(
        functools.partial(_ffn_up_kernel, tiles_per_seq=seq // tm),
        grid=(t // tm, nj),
        in_specs=[
            pl.BlockSpec((tm, d), lambda i, j: (i, 0)),
            pl.BlockSpec((None, d, tn), lambda i, j: (layer, 0, j)),
            pl.BlockSpec((None, d, tn), lambda i, j: (layer, 0, j + nj)),
            pl.BlockSpec((CONV_WIDTH + 1, tn), lambda i, j: (0, j)),
        ],
        out_specs=pl.BlockSpec((tm, tn), lambda i, j: (i, j)),
        out_shape=jax.ShapeDtypeStruct((t, f), BF16),
        scratch_shapes=[pltpu.VMEM((nj, V7X_SUBLANES, tn), F32)],
        compiler_params=_params(("arbitrary", "arbitrary")),
        name="ffn_up",
    )(h, w_up, w_up, conv_wb)


def _ffn_down_kernel(a_ref, w_ref, g_ref, res_ref, gn_ref, x_ref, *maybe_h_ref):
    m = _dot(a_ref[...], w_ref[...])
    _residual_norm_epilogue(m, g_ref, res_ref, gn_ref, x_ref, maybe_h_ref[0] if maybe_h_ref else None)


def ffn_down(a, w, layer, g, res, g_next):
    t, kdim = a.shape
    d = w.shape[2]
    tm = _pick(t, (256, 128))
    row = pl.BlockSpec((tm, d), lambda i: (i, 0))
    vec = pl.BlockSpec((1, d), lambda i: (0, 0))
    has_next = g_next is not None
    gn = (g_next if has_next else g).reshape(1, d).astype(F32)
    outs = pl.pallas_call(
        _ffn_down_kernel,
        grid=(t // tm,),
        in_specs=[
            pl.BlockSpec((tm, kdim), lambda i: (i, 0)),
            pl.BlockSpec((None, kdim, d), lambda i: (layer, 0, 0), pipeline_mode=pl.Buffered(1)),
            vec, row, vec,
        ],
        out_specs=[row, row] if has_next else [row],
        out_shape=([jax.ShapeDtypeStruct((t, d), F32)]
                   + ([jax.ShapeDtypeStruct((t, d), BF16)] if has_next else [])),
        compiler_params=_params(("parallel",)),
        name="ffn_down",
    )(a, w, g.reshape(1, d).astype(F32), res, gn)
    return (outs[0], outs[1]) if has_next else (outs[0], None)


def _pad_cols(w, width):
    return jnp.pad(w, [(0, 0)] * (w.ndim - 1) + [(0, width - w.shape[-1])])


def kernel(x, positions, pre_mix_norm, post_mix_norm, pre_ffn_norm, post_ffn_norm, w_in, w_mv_down, shift_mu, shift_mu_mv, w0, w2, a0, a2, g2, k_k, k_a, r_k, gn_w, gn_b, v0, v2, lam_q1, lam_k1, lam_q2, lam_k2, subln_w, w_out, w_up, conv_w, conv_b, w_down):
    batch, seq, d_model = x.shape
    t = batch * seq
    depth = w_in.shape[0]
    width = w0.shape[1]
    diff_heads = (d_model - width) // DIFF_V_DIM
    rwkv_cols = shift_mu.shape[1]
    qk_cols = diff_heads * 2 * DIFF_QK_DIM
    rwkv_n = 3 * width + LORA_BLOCK
    assert rwkv_n % IN_RWKV_TN == 0 and (3 * width) % LORA_BLOCK == 0

    rope = _rope_tables(positions)
    w_out_b = w_out.astype(BF16)
    w_down_b = w_down.astype(BF16)
    xf = x.reshape(t, d_model)
    h = rms_norm_bf16(xf, pre_mix_norm[0])
    pr_first = None
    for l in range(depth):
        o_r, o_wl = 0, width
        o_k = o_wl + DECAY_LORA
        o_v = o_k + width
        o_al = o_v + width
        o_gl = o_al + AAA_LORA
        if l == 0:
            mv_w = jnp.zeros((d_model, MV_LORA), F32)
            mv_mu = jnp.zeros((MV_LORA,), F32)
        else:
            mv_w = w_mv_down[l - 1]
            mv_mu = shift_mu_mv[l - 1]
        arrange = lambda m, mv: jnp.concatenate(
            [m[..., o_r:o_wl], m[..., o_k:o_v], m[..., o_v:o_al],
             m[..., o_wl:o_k], m[..., o_al:o_gl],
             _pad_cols(jnp.concatenate([m[..., o_gl:rwkv_cols], mv], axis=-1), LORA_GM)], axis=-1)
        w_rwkv = _pad_cols(arrange(w_in[l], mv_w), rwkv_n).astype(BF16)
        mu_rwkv = _pad_cols(arrange(shift_mu[l], mv_mu), rwkv_n)
        pr = in_proj_rwkv(h, w_rwkv, mu_rwkv, seq)

        vecs = jnp.stack([w0[l], a0[l], k_k[l], k_a[l], r_k[l].reshape(width), gn_w[l], gn_b[l],
                          v0[l - 1] if l > 0 else jnp.zeros((width,), F32)]).astype(F32)
        lora_rows = lambda w, start, rows: jnp.pad(
            w, ((start, rows - start - w.shape[0]), (0, 0))).astype(BF16)
        w2p = lora_rows(w2[l], 0, LORA_WA)
        a2p = lora_rows(a2[l], DECAY_LORA, LORA_WA)
        g2p = lora_rows(g2[l], 0, LORA_GM)
        v2p = lora_rows(v2[l - 1], GATE_LORA, LORA_GM) if l > 0 else None
        o_rwkv = rwkv7_mixer(pr, pr_first, vecs, w2p, a2p, g2p, v2p, batch, seq, width)
        if l == 0:
            pr_first = pr

        qkv = in_proj_attn(h, w_in[l][:, rwkv_cols:].astype(BF16), rope, seq, qk_cols)
        lambda_init = 0.8 - 0.6 * math.exp(-0.3 * l)
        lam = (jnp.exp(jnp.sum(lam_q1[l] * lam_k1[l])) - jnp.exp(jnp.sum(lam_q2[l] * lam_k2[l]))
               + lambda_init)
        o_diff = diff_attention(qkv, lam, subln_w[l], lambda_init, batch, seq, diff_heads)

        xf, h = out_proj(o_rwkv, o_diff, w_out_b, l, post_mix_norm[l], xf, pre_ffn_norm[l])

        conv_wb = jnp.concatenate([conv_w[l], conv_b[l][None, :]], axis=0).astype(F32)
        act = ffn_up(h, w_up, l, conv_wb, seq)
        g_next = pre_mix_norm[l + 1] if l + 1 < depth else None
        xf, h = ffn_down(act, w_down_b, l, post_ffn_norm[l], xf, g_next)
    return xf.reshape(batch, seq, d_model)
```
